```python
import jax, jax.numpy as jnp
from jax import lax
import numpy as np

D_MODEL = 2048
BATCH = 4
SEQ = 2048
DEPTH = 1
DEC_BATCH = 128
DEC_SEQ = 8
PAST_LEN = 16384
PAGE_SIZE = 128

HEAD_DIM = 64
ATTN_WIDTH = D_MODEL // 2
N_Q_HEADS = ATTN_WIDTH // HEAD_DIM
N_KV_HEADS = N_Q_HEADS // 4
Q_PER_KV = N_Q_HEADS // N_KV_HEADS
KV_WIDTH = N_KV_HEADS * HEAD_DIM
WINDOW = 128
BLOCK = WINDOW
SG_WIDTH = D_MODEL - ATTN_WIDTH
N_SG_HEADS = 8
SG_HEAD_DIM = SG_WIDTH // N_SG_HEADS
CHUNK = 128
D_FF = ((8 * D_MODEL // 3 + 255) // 256) * 256
IN_WIDTH = ATTN_WIDTH + 2 * KV_WIDTH + 2 * SG_WIDTH
ROPE_THETA = 10000.0
EPS = 1e-6

kernel_name = "hymba_swa_sink_gmlp_decode_step"


def rmsnorm(x, g):
    xf = x.astype(jnp.float32)
    r = lax.rsqrt(jnp.mean(xf * xf, axis=-1, keepdims=True) + EPS)
    return (xf * r * g.astype(jnp.float32)).astype(x.dtype)


def rope(x, pos):
    half = HEAD_DIM // 2
    inv = ROPE_THETA ** (-jnp.arange(half, dtype=jnp.float32) / half)
    ang = pos.astype(jnp.float32)[:, None] * inv[None, :]
    cos = jnp.cos(ang)[:, None, :]
    sin = jnp.sin(ang)[:, None, :]
    x1 = x[..., :half].astype(jnp.float32)
    x2 = x[..., half:].astype(jnp.float32)
    out = jnp.concatenate([x1 * cos - x2 * sin, x2 * cos + x1 * sin], axis=-1)
    return out.astype(x.dtype)


def project(xn, pos, w_in, q_norm, k_norm, sg_norm):
    b, s, _ = xn.shape
    proj = jnp.einsum('bsd,de->bse', xn, w_in)
    cuts = [ATTN_WIDTH, ATTN_WIDTH + KV_WIDTH, ATTN_WIDTH + 2 * KV_WIDTH,
            ATTN_WIDTH + 2 * KV_WIDTH + SG_WIDTH]
    q, k, v, u, g = jnp.split(proj, cuts, axis=-1)
    q = rope(rmsnorm(q.reshape(b, s, N_Q_HEADS, HEAD_DIM), q_norm), pos)
    k = rope(rmsnorm(k.reshape(b, s, N_KV_HEADS, HEAD_DIM), k_norm), pos)
    v = v.reshape(b, s, N_KV_HEADS, HEAD_DIM)
    u = jax.nn.gelu(u)
    g = rmsnorm(jax.nn.gelu(g), sg_norm)
    return q, k, v, u, g


def sink_attention(q, k, v, mask, sinks):
    scale = HEAD_DIM ** -0.5
    s = jnp.einsum('...qhgd,...khd->...hgqk', q.astype(jnp.float32), k.astype(jnp.float32)) * scale
    s = jnp.where(mask, s, -jnp.inf)
    sink = sinks.astype(jnp.float32).reshape(N_KV_HEADS, Q_PER_KV, 1, 1)
    m = jnp.maximum(jnp.max(s, axis=-1, keepdims=True), sink)
    p = jnp.exp(s - m)
    denom = jnp.sum(p, axis=-1, keepdims=True) + jnp.exp(sink - m)
    o = jnp.einsum('...hgqk,...khd->...qhgd', p / denom, v.astype(jnp.float32))
    return o.astype(v.dtype)


def attention_prompt(q, k, v, sinks):
    b, s = q.shape[:2]
    nb = s // BLOCK
    qb = q.reshape(b, nb, BLOCK, N_KV_HEADS, Q_PER_KV, HEAD_DIM)

    def band(t):
        pad = jnp.zeros((b, BLOCK) + t.shape[2:], t.dtype)
        tb = jnp.concatenate([pad, t], axis=1).reshape((b, nb + 1, BLOCK) + t.shape[2:])
        return jnp.concatenate([tb[:, :-1], tb[:, 1:]], axis=2)

    qi = jnp.arange(BLOCK)[:, None]
    kj = jnp.arange(2 * BLOCK)[None, :]
    diff = BLOCK + qi - kj
    key_pos = (jnp.arange(nb)[:, None, None] - 1) * BLOCK + kj[None]
    mask = (diff >= 0) & (diff < WINDOW) & (key_pos >= 0)
    o = sink_attention(qb, band(k), band(v), mask[:, None, None], sinks)
    return o.reshape(b, s, ATTN_WIDTH)


def attention_sample(q, k, v, ck, cv, sinks):
    b, s = q.shape[:2]
    k_all = jnp.concatenate([ck, k], axis=1)
    v_all = jnp.concatenate([cv, v], axis=1)
    qi = jnp.arange(s)[:, None]
    kj = jnp.arange(WINDOW + s)[None, :]
    diff = WINDOW + qi - kj
    mask = (diff >= 0) & (diff < WINDOW)
    o = sink_attention(q.reshape(b, s, N_KV_HEADS, Q_PER_KV, HEAD_DIM), k_all, v_all, mask, sinks)
    return o.reshape(b, s, ATTN_WIDTH), k_all[:, -WINDOW:], v_all[:, -WINDOW:]


def spatial_gate(u, g, sg_w, sg_b):
    b, s, _ = u.shape
    rows = min(s, CHUNK)
    nc = s // rows
    causal = jnp.tril(jnp.ones((rows, rows), dtype=bool))
    w = jnp.where(causal, sg_w[:, :rows, :rows], 0.0)
    gc = g.reshape(b, nc, rows, N_SG_HEADS, SG_HEAD_DIM)
    mixed = jnp.einsum('hij,bnjhd->bnihd', w, gc) + sg_b[:, :rows].T[:, :, None]
    return u * mixed.reshape(b, s, SG_WIDTH)


def merge_and_ffn(h, a, sgo, attn_out_norm, sg_out_norm, w_o, ffn_norm, w_gate, w_up, w_down):
    mix = jnp.concatenate([rmsnorm(a, attn_out_norm), rmsnorm(sgo, sg_out_norm)], axis=-1)
    h = h + jnp.einsum('bse,ed->bsd', mix, w_o)
    hn = rmsnorm(h, ffn_norm)
    act = jax.nn.silu(jnp.einsum('bsd,df->bsf', hn, w_gate)) * jnp.einsum('bsd,df->bsf', hn, w_up)
    return h + jnp.einsum('bsf,fd->bsd', act, w_down)


def setup_inputs(seed: int = 0) -> dict:
    key = jax.random.key(seed)
    ks = jax.random.split(key, 20)
    f32 = jnp.float32
    L = DEPTH

    def nrm(k, shape, scale=1.0):
        return jax.random.normal(k, shape, f32) * scale

    return {
        "x_prompt": nrm(ks[0], (BATCH, SEQ, D_MODEL)),
        "x_sample": nrm(ks[1], (DEC_BATCH, DEC_SEQ, D_MODEL)),
        "cache_k_win": nrm(ks[2], (L, DEC_BATCH, WINDOW, N_KV_HEADS, HEAD_DIM)),
        "cache_v_win": nrm(ks[3], (L, DEC_BATCH, WINDOW, N_KV_HEADS, HEAD_DIM)),
        "attn_norm": 1.0 + nrm(ks[4], (L, D_MODEL), 0.02),
        "w_in": nrm(ks[5], (L, D_MODEL, IN_WIDTH), D_MODEL ** -0.5),
        "q_norm": 1.0 + nrm(ks[6], (L, HEAD_DIM), 0.02),
        "k_norm": 1.0 + nrm(ks[7], (L, HEAD_DIM), 0.02),
        "sinks": nrm(ks[8], (L, N_Q_HEADS), 0.5),
        "sg_norm": 1.0 + nrm(ks[9], (L, SG_WIDTH), 0.02),
        "sg_w": nrm(ks[10], (L, N_SG_HEADS, CHUNK, CHUNK), CHUNK ** -0.5),
        "sg_b": 1.0 + nrm(ks[11], (L, N_SG_HEADS, CHUNK), 0.1),
        "attn_out_norm": 1.0 + nrm(ks[12], (L, ATTN_WIDTH), 0.02),
        "sg_out_norm": 1.0 + nrm(ks[13], (L, SG_WIDTH), 0.02),
        "w_o": nrm(ks[14], (L, D_MODEL, D_MODEL), D_MODEL ** -0.5),
        "ffn_norm": 1.0 + nrm(ks[15], (L, D_MODEL), 0.02),
        "w_gate": nrm(ks[16], (L, D_MODEL, D_FF), D_MODEL ** -0.5),
        "w_up": nrm(ks[17], (L, D_MODEL, D_FF), D_MODEL ** -0.5),
        "w_down": nrm(ks[18], (L, D_FF, D_MODEL), D_FF ** -0.5),
    }


def reference(x_prompt, x_sample, cache_k_win, cache_v_win, attn_norm, w_in, q_norm, k_norm,
              sinks, sg_norm, sg_w, sg_b, attn_out_norm, sg_out_norm, w_o, ffn_norm,
              w_gate, w_up, w_down):
    pos_p = jnp.arange(x_prompt.shape[1], dtype=jnp.int32)
    pos_s = PAST_LEN + jnp.arange(x_sample.shape[1], dtype=jnp.int32)
    hp, hs = x_prompt, x_sample
    kp_list, vp_list, ks_list, vs_list, gs_list = [], [], [], [], []
    for l in range(DEPTH):
        q, k, v, u, g = project(rmsnorm(hp, attn_norm[l]), pos_p, w_in[l], q_norm[l], k_norm[l], sg_norm[l])
        a = attention_prompt(q, k, v, sinks[l])
        sgo = spatial_gate(u, g, sg_w[l], sg_b[l])
        hp = merge_and_ffn(hp, a, sgo, attn_out_norm[l], sg_out_norm[l], w_o[l], ffn_norm[l],
                           w_gate[l], w_up[l], w_down[l])
        kp_list.append(k[:, -WINDOW:])
        vp_list.append(v[:, -WINDOW:])
        q, k, v, u, g = project(rmsnorm(hs, attn_norm[l]), pos_s, w_in[l], q_norm[l], k_norm[l], sg_norm[l])
        a, k_new, v_new = attention_sample(q, k, v, cache_k_win[l], cache_v_win[l], sinks[l])
        sgo = spatial_gate(u, g, sg_w[l], sg_b[l])
        hs = merge_and_ffn(hs, a, sgo, attn_out_norm[l], sg_out_norm[l], w_o[l], ffn_norm[l],
                           w_gate[l], w_up[l], w_down[l])
        ks_list.append(k_new)
        vs_list.append(v_new)
        gs_list.append(g)
    k_win_prompt = jnp.stack(kp_list)
    v_win_prompt = jnp.stack(vp_list)
    k_win_sample = jnp.stack(ks_list)
    v_win_sample = jnp.stack(vs_list)
    sg_v_sample = jnp.stack(gs_list)
    return (hp, hs, k_win_prompt, v_win_prompt, k_win_sample, v_win_sample, sg_v_sample)
```

```python
import functools

import jax
import jax.numpy as jnp
from jax import lax
from jax.experimental import pallas as pl
from jax.experimental.pallas import tpu as pltpu

HEAD_DIM = 64
N_KV_HEADS = 4
Q_PER_KV = 4
N_Q_HEADS = N_KV_HEADS * Q_PER_KV
KV_WIDTH = N_KV_HEADS * HEAD_DIM
ATTN_WIDTH = N_Q_HEADS * HEAD_DIM
WINDOW = 128
N_SG_HEADS = 8
SG_HEAD_DIM = 128
SG_WIDTH = N_SG_HEADS * SG_HEAD_DIM
CHUNK = 128
PAST_LEN = 16384
ROPE_THETA = 10000.0
EPS = 1e-6

LANES = 128
VMEM_LIMIT_BYTES = 56 * 1024 * 1024

ROW_TILE = 512
FF_TILE = 512
SEQS_PER_STEP = 16

F32 = jnp.float32
BF16 = jnp.bfloat16


def _rms(x, gain_row):
    r = lax.rsqrt(jnp.mean(x * x, axis=-1, keepdims=True) + EPS)
    return x * r * gain_row


def _head_norm_rope(t, gain_row, cos, sin_signed):
    lane = lax.broadcasted_iota(jnp.int32, (t.shape[0], LANES), 1)
    lo_head = lane < HEAD_DIM
    first_half = (lane % HEAD_DIM) < (HEAD_DIM // 2)
    outs = []
    for j in range(t.shape[1] // LANES):
        blk = t[:, LANES * j:LANES * (j + 1)]
        sq = blk * blk
        s_all = jnp.sum(sq, axis=-1, keepdims=True)
        s_lo = jnp.sum(jnp.where(lo_head, sq, 0.0), axis=-1, keepdims=True)
        s_hi = s_all - s_lo
        r = jnp.where(lo_head,
                      lax.rsqrt(s_lo * (1.0 / HEAD_DIM) + EPS),
                      lax.rsqrt(s_hi * (1.0 / HEAD_DIM) + EPS))
        y = blk * r * gain_row
        rot = jnp.where(first_half,
                        pltpu.roll(y, LANES - HEAD_DIM // 2, 1),
                        pltpu.roll(y, HEAD_DIM // 2, 1))
        outs.append(y * cos + rot * sin_signed)
    return jnp.concatenate(outs, axis=1)


def _inproj_body(x_ref, an_ref, w_ref, qn_ref, kn_ref, sgn_ref, cos_ref, sin_ref,
                 q_ref, k_ref, v_ref, u_ref, g_ref):
    xn = _rms(x_ref[...], an_ref[...]).astype(BF16)
    cos = cos_ref[...]
    sin = sin_ref[...]
    c0, c1, c2, c3 = ATTN_WIDTH, ATTN_WIDTH + KV_WIDTH, ATTN_WIDTH + 2 * KV_WIDTH, \
        ATTN_WIDTH + 2 * KV_WIDTH + SG_WIDTH

    q = jnp.dot(xn, w_ref[:, 0:c0], preferred_element_type=F32)
    q_ref[...] = (_head_norm_rope(q, qn_ref[...], cos, sin) * (HEAD_DIM ** -0.5)).astype(q_ref.dtype)

    k = jnp.dot(xn, w_ref[:, c0:c1], preferred_element_type=F32)
    k_ref[...] = _head_norm_rope(k, kn_ref[...], cos, sin)

    v_ref[...] = jnp.dot(xn, w_ref[:, c1:c2], preferred_element_type=F32)

    u = jnp.dot(xn, w_ref[:, c2:c3], preferred_element_type=F32)
    u_ref[...] = jax.nn.gelu(u).astype(u_ref.dtype)

    g = jax.nn.gelu(jnp.dot(xn, w_ref[:, c3:], preferred_element_type=F32))
    g_ref[...] = _rms(g, sgn_ref[...]).astype(g_ref.dtype)


def _inproj(x, attn_norm, w_in, qn2, kn2, sg_norm, cos_t, sin_t, q_dtype, g_dtype):
    rows, d_model = x.shape
    tm = min(ROW_TILE, rows)
    n_tab = cos_t.shape[0] // tm
    const = lambda i: (0, 0)
    row = lambda i: (i, 0)
    tab = lambda i: (i % n_tab, 0)
    return pl.pallas_call(
        _inproj_body,
        grid=(rows // tm,),
        in_specs=[
            pl.BlockSpec((tm, d_model), row),
            pl.BlockSpec((1, d_model), const),
            pl.BlockSpec(w_in.shape, const, pipeline_mode=pl.Buffered(1)),
            pl.BlockSpec((1, LANES), const),
            pl.BlockSpec((1, LANES), const),
            pl.BlockSpec((1, SG_WIDTH), const),
            pl.BlockSpec((tm, LANES), tab),
            pl.BlockSpec((tm, LANES), tab),
        ],
        out_specs=[
            pl.BlockSpec((tm, ATTN_WIDTH), row),
            pl.BlockSpec((tm, KV_WIDTH), row),
            pl.BlockSpec((tm, KV_WIDTH), row),
            pl.BlockSpec((tm, SG_WIDTH), row),
            pl.BlockSpec((tm, SG_WIDTH), row),
        ],
        out_shape=[
            jax.ShapeDtypeStruct((rows, ATTN_WIDTH), q_dtype),
            jax.ShapeDtypeStruct((rows, KV_WIDTH), F32),
            jax.ShapeDtypeStruct((rows, KV_WIDTH), F32),
            jax.ShapeDtypeStruct((rows, SG_WIDTH), BF16),
            jax.ShapeDtypeStruct((rows, SG_WIDTH), g_dtype),
        ],
        compiler_params=pltpu.CompilerParams(
            dimension_semantics=("parallel",), vmem_limit_bytes=VMEM_LIMIT_BYTES),
        name="inproj",
    )(x, attn_norm, w_in, qn2, kn2, sg_norm, cos_t, sin_t)


def _softmax_sink(s, mask, sink):
    s = jnp.where(mask, s, -jnp.inf)
    m = jnp.maximum(jnp.max(s, axis=-1, keepdims=True), sink)
    p = jnp.exp(s - m)
    denom = jnp.sum(p, axis=-1, keepdims=True) + jnp.exp(sink - m)
    return p, 1.0 / denom


def _spatial_gate(u, g, w_ref, bias, rows_per_seq):
    r = lax.broadcasted_iota(jnp.int32, (CHUNK, CHUNK), 0)
    c = lax.broadcasted_iota(jnp.int32, (CHUNK, CHUNK), 1)
    causal = (r // rows_per_seq == c // rows_per_seq) & (c % rows_per_seq <= r % rows_per_seq)
    gb = g.astype(BF16)
    outs = []
    for h in range(N_SG_HEADS):
        w = jnp.where(causal, w_ref[h], 0.0).astype(BF16)
        cols = slice(SG_HEAD_DIM * h, SG_HEAD_DIM * (h + 1))
        parts = []
        for ch in range(u.shape[0] // CHUNK):
            rws = slice(CHUNK * ch, CHUNK * (ch + 1))
            mixed = jnp.dot(w, gb[rws, cols], preferred_element_type=F32) + bias[:, cols]
            parts.append(u[rws, cols].astype(F32) * mixed)
        outs.append(jnp.concatenate(parts, axis=0))
    return jnp.concatenate(outs, axis=1)


def _out_proj(x, a, sgo, aon, son, wo_ref):
    mix = jnp.concatenate([_rms(a, aon), _rms(sgo, son)], axis=1).astype(BF16)
    return x + jnp.dot(mix, wo_ref[...], preferred_element_type=F32)


def _prompt_mix_body(sinks_ref, q_ref, k_ref, v_ref, u_ref, g_ref, x_ref, sgw_ref, bias_ref,
                     aon_ref, son_ref, wo_ref, h_ref, a_scr):
    j = pl.program_id(1)
    blocks = q_ref.shape[0] // WINDOW
    quarter = lax.broadcasted_iota(jnp.int32, (WINDOW, KV_WIDTH), 1) // HEAD_DIM
    qi = lax.broadcasted_iota(jnp.int32, (WINDOW, 2 * WINDOW), 0)
    kj = lax.broadcasted_iota(jnp.int32, (WINDOW, 2 * WINDOW), 1)

    def block(r, carry):
        n = j * blocks + r
        start = pl.multiple_of(jnp.maximum(n - 1, 0) * WINDOW, WINDOW)
        diff = (n * WINDOW - start) + qi - kj
        mask = (diff >= 0) & (diff < WINDOW)
        q = q_ref[pl.ds(pl.multiple_of(r * WINDOW, WINDOW), WINDOW), :]
        kb = k_ref[pl.ds(start, 2 * WINDOW), :].astype(BF16)
        vb = v_ref[pl.ds(start, 2 * WINDOW), :].astype(BF16)
        groups = [jnp.zeros((WINDOW, KV_WIDTH), F32) for _ in range(Q_PER_KV)]
        for h in range(N_KV_HEADS):
            sel = quarter == h
            lhs = jnp.concatenate(
                [jnp.where(sel, q[:, KV_WIDTH * g:KV_WIDTH * (g + 1)], 0) for g in range(Q_PER_KV)],
                axis=0)
            s = lax.dot_general(lhs, kb, (((1,), (1,)), ((), ())), preferred_element_type=F32)
            ps, invs = [], []
            for g in range(Q_PER_KV):
                p, inv = _softmax_sink(s[WINDOW * g:WINDOW * (g + 1)], mask,
                                       sinks_ref[Q_PER_KV * h + g])
                ps.append(p.astype(BF16))
                invs.append(inv)
            o = jnp.dot(jnp.concatenate(ps, axis=0), vb, preferred_element_type=F32)
            for g in range(Q_PER_KV):
                groups[g] = groups[g] + jnp.where(sel, o[WINDOW * g:WINDOW * (g + 1)] * invs[g], 0.0)
        a_scr[pl.ds(pl.multiple_of(r * WINDOW, WINDOW), WINDOW), :] = jnp.concatenate(groups, axis=1)
        return carry

    lax.fori_loop(0, blocks, block, 0)

    sgo = _spatial_gate(u_ref[...], g_ref[...], sgw_ref, bias_ref[...], CHUNK)
    h_ref[...] = _out_proj(x_ref[...], a_scr[...], sgo, aon_ref[...], son_ref[...], wo_ref)


def _prompt_mix(sinks, q, k, v, u, g, x, sg_w, bias, aon, son, wo, batch, seq):
    tm = ROW_TILE
    tiles = seq // tm
    d_model = x.shape[1]
    row = lambda b, j: (b * tiles + j, 0)
    per_seq = lambda b, j: (b, 0)
    const2 = lambda b, j: (0, 0)
    const3 = lambda b, j: (0, 0, 0)
    return pl.pallas_call(
        _prompt_mix_body,
        grid=(batch, tiles),
        in_specs=[
            pl.BlockSpec(memory_space=pltpu.SMEM),
            pl.BlockSpec((tm, ATTN_WIDTH), row),
            pl.BlockSpec((seq, KV_WIDTH), per_seq),
            pl.BlockSpec((seq, KV_WIDTH), per_seq),
            pl.BlockSpec((tm, SG_WIDTH), row),
            pl.BlockSpec((tm, SG_WIDTH), row),
            pl.BlockSpec((tm, d_model), row),
            pl.BlockSpec(sg_w.shape, const3),
            pl.BlockSpec(bias.shape, const2),
            pl.BlockSpec((1, ATTN_WIDTH), const2),
            pl.BlockSpec((1, SG_WIDTH), const2),
            pl.BlockSpec(wo.shape, const2, pipeline_mode=pl.Buffered(1)),
        ],
        out_specs=pl.BlockSpec((tm, d_model), row),
        out_shape=jax.ShapeDtypeStruct(x.shape, F32),
        scratch_shapes=[pltpu.VMEM((tm, ATTN_WIDTH), F32)],
        compiler_params=pltpu.CompilerParams(
            dimension_semantics=("parallel", "arbitrary"), vmem_limit_bytes=VMEM_LIMIT_BYTES),
        name="prompt_mix",
    )(sinks, q, k, v, u, g, x, sg_w, bias, aon, son, wo)


def _sample_mix_body(q_ref, k_ref, v_ref, ck_ref, cv_ref, u_ref, g_ref, x_ref, sink_ref, sgw_ref,
                     bias_ref, aon_ref, son_ref, wo_ref, h_ref, kw_ref, vw_ref, a_scr, *, dec_seq):
    rows = N_Q_HEADS * dec_seq
    keys = WINDOW + dec_seq
    quarter = lax.broadcasted_iota(jnp.int32, (dec_seq, KV_WIDTH), 1) // HEAD_DIM
    t = lax.broadcasted_iota(jnp.int32, (rows, keys), 0) % dec_seq
    kj = lax.broadcasted_iota(jnp.int32, (rows, keys), 1)
    mask = (kj > t) & (kj <= WINDOW + t)
    sink = sink_ref[...]

    def one_seq(b, carry):
        r0 = pl.multiple_of(b * dec_seq, dec_seq)
        q = q_ref[pl.ds(r0, dec_seq), :]
        k_new = k_ref[pl.ds(r0, dec_seq), :]
        v_new = v_ref[pl.ds(r0, dec_seq), :]
        k_all = jnp.concatenate([ck_ref[b], k_new], axis=0)
        v_all = jnp.concatenate([cv_ref[b], v_new], axis=0)
        kw_ref[b] = k_all[dec_seq:]
        vw_ref[b] = v_all[dec_seq:]
        lhs = jnp.concatenate(
            [jnp.where(quarter == h, q[:, KV_WIDTH * g:KV_WIDTH * (g + 1)], 0.0)
             for h in range(N_KV_HEADS) for g in range(Q_PER_KV)], axis=0).astype(BF16)
        s = lax.dot_general(lhs, k_all.astype(BF16), (((1,), (1,)), ((), ())),
                            preferred_element_type=F32)
        p, inv = _softmax_sink(s, mask, sink)
        o = jnp.dot(p.astype(BF16), v_all.astype(BF16), preferred_element_type=F32) * inv
        groups = []
        for g in range(Q_PER_KV):
            acc = jnp.zeros((dec_seq, KV_WIDTH), F32)
            for h in range(N_KV_HEADS):
                piece = o[(Q_PER_KV * h + g) * dec_seq:(Q_PER_KV * h + g + 1) * dec_seq]
                acc = acc + jnp.where(quarter == h, piece, 0.0)
            groups.append(acc)
        a_scr[pl.ds(r0, dec_seq), :] = jnp.concatenate(groups, axis=1)
        return carry

    lax.fori_loop(0, ck_ref.shape[0], one_seq, 0)

    sgo = _spatial_gate(u_ref[...], g_ref[...], sgw_ref, bias_ref[...], dec_seq)
    h_ref[...] = _out_proj(x_ref[...], a_scr[...], sgo, aon_ref[...], son_ref[...], wo_ref)


def _sample_mix(q, k, v, ck, cv, u, g, x, sink_col, sg_w, bias, aon, son, wo, dec_seq):
    nseq = ck.shape[0]
    sb = SEQS_PER_STEP
    tm = sb * dec_seq
    d_model = x.shape[1]
    row = lambda i: (i, 0)
    seq3 = lambda i: (i, 0, 0)
    const2 = lambda i: (0, 0)
    const3 = lambda i: (0, 0, 0)
    return pl.pallas_call(
        functools.partial(_sample_mix_body, dec_seq=dec_seq),
        grid=(nseq // sb,),
        in_specs=[
            pl.BlockSpec((tm, ATTN_WIDTH), row),
            pl.BlockSpec((tm, KV_WIDTH), row),
            pl.BlockSpec((tm, KV_WIDTH), row),
            pl.BlockSpec((sb, WINDOW, KV_WIDTH), seq3),
            pl.BlockSpec((sb, WINDOW, KV_WIDTH), seq3),
            pl.BlockSpec((tm, SG_WIDTH), row),
            pl.BlockSpec((tm, SG_WIDTH), row),
            pl.BlockSpec((tm, d_model), row),
            pl.BlockSpec(sink_col.shape, const2),
            pl.BlockSpec(sg_w.shape, const3),
            pl.BlockSpec(bias.shape, const2),
            pl.BlockSpec((1, ATTN_WIDTH), const2),
            pl.BlockSpec((1, SG_WIDTH), const2),
            pl.BlockSpec(wo.shape, const2, pipeline_mode=pl.Buffered(1)),
        ],
        out_specs=[
            pl.BlockSpec((tm, d_model), row),
            pl.BlockSpec((sb, WINDOW, KV_WIDTH), seq3),
            pl.BlockSpec((sb, WINDOW, KV_WIDTH), seq3),
        ],
        out_shape=[
            jax.ShapeDtypeStruct(x.shape, F32),
            jax.ShapeDtypeStruct(ck.shape, F32),
            jax.ShapeDtypeStruct(cv.shape, F32),
        ],
        scratch_shapes=[pltpu.VMEM((tm, ATTN_WIDTH), F32)],
        compiler_params=pltpu.CompilerParams(
            dimension_semantics=("parallel",), vmem_limit_bytes=VMEM_LIMIT_BYTES),
        name="sample_mix",
    )(q, k, v, ck, cv, u, g, x, sink_col, sg_w, bias, aon, son, wo)


def _ffn_body(h_ref, fn_ref, wg_ref, wu_ref, wd_ref, y_ref, hn_scr):
    @pl.when(pl.program_id(1) == 0)
    def _():
        h = h_ref[...]
        hn_scr[...] = _rms(h, fn_ref[...]).astype(BF16)
        y_ref[...] = h

    hn = hn_scr[...]
    gate = jnp.dot(hn, wg_ref[...], preferred_element_type=F32)
    up = jnp.dot(hn, wu_ref[...], preferred_element_type=F32)
    act = (jax.nn.silu(gate) * up).astype(BF16)
    y_ref[...] += jnp.dot(act, wd_ref[...], preferred_element_type=F32)


def _ffn(h, ffn_norm, wg, wu, wd):
    rows, d_model = h.shape
    d_ff = wg.shape[1]
    tm = min(ROW_TILE, rows)
    tf = FF_TILE
    return pl.pallas_call(
        _ffn_body,
        grid=(rows // tm, d_ff // tf),
        in_specs=[
            pl.BlockSpec((tm, d_model), lambda i, j: (i, 0)),
            pl.BlockSpec((1, d_model), lambda i, j: (0, 0)),
            pl.BlockSpec((d_model, tf), lambda i, j: (0, j)),
            pl.BlockSpec((d_model, tf), lambda i, j: (0, j)),
            pl.BlockSpec((tf, d_model), lambda i, j: (j, 0)),
        ],
        out_specs=pl.BlockSpec((tm, d_model), lambda i, j: (i, 0)),
        out_shape=jax.ShapeDtypeStruct(h.shape, F32),
        scratch_shapes=[pltpu.VMEM((tm, d_model), BF16)],
        compiler_params=pltpu.CompilerParams(
            dimension_semantics=("parallel", "arbitrary"), vmem_limit_bytes=VMEM_LIMIT_BYTES),
        name="ffn",
    )(h, ffn_norm, wg, wu, wd)


def _rope_tables(pos):
    half = HEAD_DIM // 2
    inv = ROPE_THETA ** (-jnp.arange(half, dtype=F32) / half)
    ang = pos.astype(F32)[:, None] * inv[None, :]
    cos, sin = jnp.cos(ang), jnp.sin(ang)
    reps = LANES // half
    cos_t = jnp.tile(cos, (1, reps))
    sin_t = jnp.tile(jnp.concatenate([-sin, sin], axis=1), (1, reps // 2))
    return cos_t, sin_t


def _swap_head_order(t, axis):
    shp = t.shape
    t = t.reshape(shp[:axis] + (N_KV_HEADS, Q_PER_KV, HEAD_DIM) + shp[axis + 1:])
    t = jnp.swapaxes(t, axis, axis + 1)
    return t.reshape(shp)


def kernel(x_prompt, x_sample, cache_k_win, cache_v_win, attn_norm, w_in, q_norm, k_norm, sinks,
           sg_norm, sg_w, sg_b, attn_out_norm, sg_out_norm, w_o, ffn_norm, w_gate, w_up, w_down):
    assert w_in.shape[0] == 1, "single-layer step only"
    batch, seq, d_model = x_prompt.shape
    dec_batch, dec_seq, _ = x_sample.shape

    w_in_p = jnp.concatenate(
        [_swap_head_order(w_in[0][:, :ATTN_WIDTH], 1), w_in[0][:, ATTN_WIDTH:]], axis=1).astype(BF16)
    w_o_p = jnp.concatenate(
        [_swap_head_order(w_o[0][:ATTN_WIDTH], 0), w_o[0][ATTN_WIDTH:]], axis=0).astype(BF16)
    aon = _swap_head_order(attn_out_norm[0], 0)[None, :]
    son = sg_out_norm
    wg, wu, wd = w_gate[0].astype(BF16), w_up[0].astype(BF16), w_down[0].astype(BF16)
    qn2 = jnp.tile(q_norm, (1, LANES // HEAD_DIM))
    kn2 = jnp.tile(k_norm, (1, LANES // HEAD_DIM))
    sink_vec = sinks[0]

    cos_p, sin_p = _rope_tables(jnp.arange(seq, dtype=jnp.int32))
    cos_s, sin_s = _rope_tables(PAST_LEN + jnp.arange(dec_seq, dtype=jnp.int32))
    tm_s = min(ROW_TILE, dec_batch * dec_seq)
    cos_s = jnp.tile(cos_s, (tm_s // dec_seq, 1))
    sin_s = jnp.tile(sin_s, (tm_s // dec_seq, 1))

    bias_p = jnp.repeat(sg_b[0][:, :CHUNK].T, SG_HEAD_DIM, axis=1)
    seqs = CHUNK // dec_seq
    bias_s = jnp.repeat(jnp.tile(sg_b[0][:, :dec_seq].T, (seqs, 1)), SG_HEAD_DIM, axis=1)
    sgw_p = sg_w[0][:, :CHUNK, :CHUNK]
    sgw_s = jnp.tile(sg_w[0][:, :dec_seq, :dec_seq], (1, seqs, seqs))
    sink_col = jnp.repeat(sink_vec, dec_seq)[:, None]

    xp = x_prompt.reshape(batch * seq, d_model)
    q, k, v, u, g = _inproj(xp, attn_norm, w_in_p, qn2, kn2, sg_norm, cos_p, sin_p, BF16, BF16)
    hp = _prompt_mix(sink_vec, q, k, v, u, g, xp, sgw_p, bias_p, aon, son, w_o_p, batch, seq)
    y_prompt = _ffn(hp, ffn_norm, wg, wu, wd).reshape(x_prompt.shape)
    k_win_prompt = k.reshape(batch, seq, N_KV_HEADS, HEAD_DIM)[None, :, seq - WINDOW:]
    v_win_prompt = v.reshape(batch, seq, N_KV_HEADS, HEAD_DIM)[None, :, seq - WINDOW:]

    xs = x_sample.reshape(dec_batch * dec_seq, d_model)
    q, k, v, u, g = _inproj(xs, attn_norm, w_in_p, qn2, kn2, sg_norm, cos_s, sin_s, F32, F32)
    ck = cache_k_win[0].reshape(dec_batch, WINDOW, KV_WIDTH)
    cv = cache_v_win[0].reshape(dec_batch, WINDOW, KV_WIDTH)
    hs, kw, vw = _sample_mix(q, k, v, ck, cv, u, g, xs, sink_col, sgw_s, bias_s, aon, son, w_o_p,
                             dec_seq)
    y_sample = _ffn(hs, ffn_norm, wg, wu, wd).reshape(x_sample.shape)
    k_win_sample = kw.reshape(cache_k_win.shape)
    v_win_sample = vw.reshape(cache_v_win.shape)
    sg_v_sample = g.reshape(1, dec_batch, dec_seq, SG_WIDTH)

    return (y_prompt, y_sample, k_win_prompt, v_win_prompt, k_win_sample, v_win_sample, sg_v_sample)
```

```python
import functools

import jax
import jax.numpy as jnp
from jax import lax
from jax.experimental import pallas as pl
from jax.experimental.pallas import tpu as pltpu

HEAD_DIM = 64
N_KV_HEADS = 4
Q_PER_KV = 4
N_Q_HEADS = N_KV_HEADS * Q_PER_KV
KV_WIDTH = N_KV_HEADS * HEAD_DIM
ATTN_WIDTH = N_Q_HEADS * HEAD_DIM
WINDOW = 128
N_SG_HEADS = 8
SG_HEAD_DIM = 128
SG_WIDTH = N_SG_HEADS * SG_HEAD_DIM
CHUNK = 128
PAST_LEN = 16384
ROPE_THETA = 10000.0
EPS = 1e-6

LANES = 128
VMEM_LIMIT_BYTES = 56 * 1024 * 1024

ROW_TILE = 512
FF_TILE = 512
SEQS_PER_STEP = 16

F32 = jnp.float32
BF16 = jnp.bfloat16


def _rms(x, gain_row):
    r = lax.rsqrt(jnp.mean(x * x, axis=-1, keepdims=True) + EPS)
    return x * r * gain_row


def _swap_heads(t):
    lane = lax.broadcasted_iota(jnp.int32, (t.shape[0], LANES), 1)
    lo = lane < HEAD_DIM
    outs = []
    for j in range(ATTN_WIDTH // LANES):
        halves = []
        for half in range(2):
            c = 2 * j + half
            a = Q_PER_KV * (c % N_KV_HEADS) + c // N_KV_HEADS
            src = t[:, LANES * (a // 2):LANES * (a // 2 + 1)]
            halves.append(src if a % 2 == half else pltpu.roll(src, HEAD_DIM, 1))
        outs.append(jnp.where(lo, halves[0], halves[1]))
    return jnp.concatenate(outs, axis=1)


def _head_norm_rope(t, gain_row, cos, sin_signed):
    lane = lax.broadcasted_iota(jnp.int32, (t.shape[0], LANES), 1)
    lo_head = lane < HEAD_DIM
    first_half = (lane % HEAD_DIM) < (HEAD_DIM // 2)
    outs = []
    for j in range(t.shape[1] // LANES):
        blk = t[:, LANES * j:LANES * (j + 1)]
        sq = blk * blk
        s_all = jnp.sum(sq, axis=-1, keepdims=True)
        s_lo = jnp.sum(jnp.where(lo_head, sq, 0.0), axis=-1, keepdims=True)
        s_hi = s_all - s_lo
        r = jnp.where(lo_head,
                      lax.rsqrt(s_lo * (1.0 / HEAD_DIM) + EPS),
                      lax.rsqrt(s_hi * (1.0 / HEAD_DIM) + EPS))
        y = blk * r * gain_row
        rot = jnp.where(first_half,
                        pltpu.roll(y, LANES - HEAD_DIM // 2, 1),
                        pltpu.roll(y, HEAD_DIM // 2, 1))
        outs.append(y * cos + rot * sin_signed)
    return jnp.concatenate(outs, axis=1)


def _inproj_body(x_ref, an_ref, w_ref, qn_ref, kn_ref, sgn_ref, cos_ref, sin_ref,
                 q_ref, k_ref, v_ref, u_ref, g_ref, *win_refs, tiles_per_seq):
    xn = _rms(x_ref[...], an_ref[...]).astype(BF16)
    cos = cos_ref[...]
    sin = sin_ref[...]
    c0, c1, c2, c3 = ATTN_WIDTH, ATTN_WIDTH + KV_WIDTH, ATTN_WIDTH + 2 * KV_WIDTH, \
        ATTN_WIDTH + 2 * KV_WIDTH + SG_WIDTH

    q = jnp.dot(xn, w_ref[:, 0:c0], preferred_element_type=F32)
    q = _head_norm_rope(q, qn_ref[...], cos, sin) * (HEAD_DIM ** -0.5)
    q_ref[...] = _swap_heads(q).astype(q_ref.dtype)

    k = _head_norm_rope(jnp.dot(xn, w_ref[:, c0:c1], preferred_element_type=F32),
                        kn_ref[...], cos, sin)
    k_ref[...] = k.astype(k_ref.dtype)
    v = jnp.dot(xn, w_ref[:, c1:c2], preferred_element_type=F32)
    v_ref[...] = v.astype(v_ref.dtype)

    if win_refs:
        kwin_ref, vwin_ref = win_refs

        @pl.when(pl.program_id(0) % tiles_per_seq == tiles_per_seq - 1)
        def _():
            kwin_ref[0] = k[k.shape[0] - WINDOW:].T
            vwin_ref[0] = v[v.shape[0] - WINDOW:].T

    u = jnp.dot(xn, w_ref[:, c2:c3], preferred_element_type=F32)
    u_ref[...] = jax.nn.gelu(u).astype(u_ref.dtype)

    g = jax.nn.gelu(jnp.dot(xn, w_ref[:, c3:], preferred_element_type=F32))
    g_ref[...] = _rms(g, sgn_ref[...]).astype(g_ref.dtype)


def _inproj(x, attn_norm, w_in, qn2, kn2, sg_norm, cos_t, sin_t, act_dtype, seq):
    rows, d_model = x.shape
    tm = min(ROW_TILE, rows)
    n_tab = cos_t.shape[0] // tm
    const = lambda i: (0, 0)
    row = lambda i: (i, 0)
    tab = lambda i: (i % n_tab, 0)
    out_specs = [
        pl.BlockSpec((tm, ATTN_WIDTH), row),
        pl.BlockSpec((tm, KV_WIDTH), row),
        pl.BlockSpec((tm, KV_WIDTH), row),
        pl.BlockSpec((tm, SG_WIDTH), row),
        pl.BlockSpec((tm, SG_WIDTH), row),
    ]
    out_shape = [
        jax.ShapeDtypeStruct((rows, ATTN_WIDTH), act_dtype),
        jax.ShapeDtypeStruct((rows, KV_WIDTH), act_dtype),
        jax.ShapeDtypeStruct((rows, KV_WIDTH), act_dtype),
        jax.ShapeDtypeStruct((rows, SG_WIDTH), BF16),
        jax.ShapeDtypeStruct((rows, SG_WIDTH), act_dtype),
    ]
    tiles_per_seq = 1
    if seq is not None:
        tiles_per_seq = seq // tm
        win = lambda i: (i // tiles_per_seq, 0, 0)
        out_specs += [pl.BlockSpec((1, KV_WIDTH, WINDOW), win)] * 2
        out_shape += [jax.ShapeDtypeStruct((rows // seq, KV_WIDTH, WINDOW), F32)] * 2
    return pl.pallas_call(
        functools.partial(_inproj_body, tiles_per_seq=tiles_per_seq),
        grid=(rows // tm,),
        in_specs=[
            pl.BlockSpec((tm, d_model), row),
            pl.BlockSpec((1, d_model), const),
            pl.BlockSpec(w_in.shape, const, pipeline_mode=pl.Buffered(1)),
            pl.BlockSpec((1, LANES), const),
            pl.BlockSpec((1, LANES), const),
            pl.BlockSpec((1, SG_WIDTH), const),
            pl.BlockSpec((tm, LANES), tab),
            pl.BlockSpec((tm, LANES), tab),
        ],
        out_specs=out_specs,
        out_shape=out_shape,
        compiler_params=pltpu.CompilerParams(
            dimension_semantics=("arbitrary",), vmem_limit_bytes=VMEM_LIMIT_BYTES),
        name="inproj",
    )(x, attn_norm, w_in, qn2, kn2, sg_norm, cos_t, sin_t)


def _softmax_sink(s, mask, sink):
    s = jnp.where(mask, s, -jnp.inf)
    m = jnp.maximum(jnp.max(s, axis=-1, keepdims=True), sink)
    p = jnp.exp(s - m)
    denom = jnp.sum(p, axis=-1, keepdims=True) + jnp.exp(sink - m)
    return p, 1.0 / denom


def _spatial_gate(u, g, w_ref, bias, rows_per_seq):
    r = lax.broadcasted_iota(jnp.int32, (CHUNK, CHUNK), 0)
    c = lax.broadcasted_iota(jnp.int32, (CHUNK, CHUNK), 1)
    causal = (r // rows_per_seq == c // rows_per_seq) & (c % rows_per_seq <= r % rows_per_seq)
    gb = g.astype(BF16)
    outs = []
    for h in range(N_SG_HEADS):
        w = jnp.where(causal, w_ref[h], 0.0).astype(BF16)
        cols = slice(SG_HEAD_DIM * h, SG_HEAD_DIM * (h + 1))
        parts = []
        for ch in range(u.shape[0] // CHUNK):
            rws = slice(CHUNK * ch, CHUNK * (ch + 1))
            mixed = jnp.dot(w, gb[rws, cols], preferred_element_type=F32) + bias[:, cols]
            parts.append(u[rws, cols].astype(F32) * mixed)
        outs.append(jnp.concatenate(parts, axis=0))
    return jnp.concatenate(outs, axis=1)


def _normed_mix(a_swapped, sgo, aon, son):
    a = _swap_heads(a_swapped)
    return jnp.concatenate([_rms(a, aon), _rms(sgo, son)], axis=1).astype(BF16)


def _prompt_mix_body(sinks_ref, q_ref, k_ref, v_ref, u_ref, g_ref, sgw_ref, bias_ref,
                     aon_ref, son_ref, wg32, wu32, wd32, wo32,
                     mix_ref, wg16, wu16, wd16, wo16, a_scr):
    wg16[...] = wg32[...].astype(BF16)
    wu16[...] = wu32[...].astype(BF16)
    wd16[...] = wd32[...].astype(BF16)
    wo16[...] = wo32[...].astype(BF16)

    j = pl.program_id(1)
    blocks = q_ref.shape[0] // WINDOW
    quarter = lax.broadcasted_iota(jnp.int32, (WINDOW, KV_WIDTH), 1) // HEAD_DIM
    qi = lax.broadcasted_iota(jnp.int32, (WINDOW, 2 * WINDOW), 0)
    kj = lax.broadcasted_iota(jnp.int32, (WINDOW, 2 * WINDOW), 1)

    def block(r, carry):
        n = j * blocks + r
        start = pl.multiple_of(jnp.maximum(n - 1, 0) * WINDOW, WINDOW)
        diff = (n * WINDOW - start) + qi - kj
        mask = (diff >= 0) & (diff < WINDOW)
        q = q_ref[pl.ds(pl.multiple_of(r * WINDOW, WINDOW), WINDOW), :]
        kb = k_ref[pl.ds(start, 2 * WINDOW), :]
        vb = v_ref[pl.ds(start, 2 * WINDOW), :]
        groups = [jnp.zeros((WINDOW, KV_WIDTH), F32) for _ in range(Q_PER_KV)]
        for h in range(N_KV_HEADS):
            sel = quarter == h
            lhs = jnp.concatenate(
                [jnp.where(sel, q[:, KV_WIDTH * g:KV_WIDTH * (g + 1)], 0) for g in range(Q_PER_KV)],
                axis=0)
            s = lax.dot_general(lhs, kb, (((1,), (1,)), ((), ())), preferred_element_type=F32)
            ps, invs = [], []
            for g in range(Q_PER_KV):
                p, inv = _softmax_sink(s[WINDOW * g:WINDOW * (g + 1)], mask,
                                       sinks_ref[Q_PER_KV * h + g])
                ps.append(p.astype(BF16))
                invs.append(inv)
            o = jnp.dot(jnp.concatenate(ps, axis=0), vb, preferred_element_type=F32)
            for g in range(Q_PER_KV):
                groups[g] = groups[g] + jnp.where(sel, o[WINDOW * g:WINDOW * (g + 1)] * invs[g], 0.0)
        a_scr[pl.ds(pl.multiple_of(r * WINDOW, WINDOW), WINDOW), :] = jnp.concatenate(groups, axis=1)
        return carry

    lax.fori_loop(0, blocks, block, 0)

    sgo = _spatial_gate(u_ref[...], g_ref[...], sgw_ref, bias_ref[...], CHUNK)
    mix_ref[...] = _normed_mix(a_scr[...], sgo, aon_ref[...], son_ref[...])


def _prompt_mix(sinks, q, k, v, u, g, sg_w, bias, aon, son, wg, wu, wd, wo, batch, seq):
    tm = ROW_TILE
    tiles = seq // tm
    steps = batch * tiles
    d_model = wo.shape[1]
    row = lambda b, j: (b * tiles + j, 0)
    per_seq = lambda b, j: (b, 0)
    const2 = lambda b, j: (0, 0)
    const3 = lambda b, j: (0, 0, 0)
    slab = lambda w: pl.BlockSpec((w.shape[0] // steps, w.shape[1]), row)
    bf16_like = lambda w: jax.ShapeDtypeStruct(w.shape, BF16)
    return pl.pallas_call(
        _prompt_mix_body,
        grid=(batch, tiles),
        in_specs=[
            pl.BlockSpec(memory_space=pltpu.SMEM),
            pl.BlockSpec((tm, ATTN_WIDTH), row),
            pl.BlockSpec((seq, KV_WIDTH), per_seq),
            pl.BlockSpec((seq, KV_WIDTH), per_seq),
            pl.BlockSpec((tm, SG_WIDTH), row),
            pl.BlockSpec((tm, SG_WIDTH), row),
            pl.BlockSpec(sg_w.shape, const3),
            pl.BlockSpec(bias.shape, const2),
            pl.BlockSpec((1, ATTN_WIDTH), const2),
            pl.BlockSpec((1, SG_WIDTH), const2),
            slab(wg), slab(wu), slab(wd), slab(wo),
        ],
        out_specs=[pl.BlockSpec((tm, d_model), row), slab(wg), slab(wu), slab(wd), slab(wo)],
        out_shape=[jax.ShapeDtypeStruct((batch * seq, d_model), BF16),
                   bf16_like(wg), bf16_like(wu), bf16_like(wd), bf16_like(wo)],
        scratch_shapes=[pltpu.VMEM((tm, ATTN_WIDTH), F32)],
        compiler_params=pltpu.CompilerParams(
            dimension_semantics=("arbitrary", "arbitrary"), vmem_limit_bytes=VMEM_LIMIT_BYTES),
        name="prompt_mix",
    )(sinks, q, k, v, u, g, sg_w, bias, aon, son, wg, wu, wd, wo)


def _sample_mix_body(q_ref, k_ref, v_ref, ck_ref, cv_ref, u_ref, g_ref, sink_ref, sgw_ref,
                     bias_ref, aon_ref, son_ref, mix_ref, kw_ref, vw_ref, a_scr, *, dec_seq):
    rows = N_Q_HEADS * dec_seq
    quarter = lax.broadcasted_iota(jnp.int32, (dec_seq, KV_WIDTH), 1) // HEAD_DIM
    t = lax.broadcasted_iota(jnp.int32, (rows, 2 * WINDOW), 0) % dec_seq
    kj = lax.broadcasted_iota(jnp.int32, (rows, 2 * WINDOW), 1)
    pos_lane = lax.broadcasted_iota(jnp.int32, (KV_WIDTH, WINDOW), 1)
    sink = sink_ref[...]
    k_new_t = k_ref[...].T
    v_new_t = v_ref[...].T

    def one_seq(b, carry):
        r0 = pl.multiple_of(b * dec_seq, dec_seq)
        q = q_ref[pl.ds(r0, dec_seq), :]
        ck = ck_ref[b]
        cv = cv_ref[b]
        own = kj - WINDOW - b * dec_seq
        mask = ((kj < WINDOW) & (kj > t)) | ((own >= 0) & (own <= t))
        lhs = jnp.concatenate(
            [jnp.where(quarter == h, q[:, KV_WIDTH * g:KV_WIDTH * (g + 1)], 0.0)
             for h in range(N_KV_HEADS) for g in range(Q_PER_KV)], axis=0).astype(BF16)
        k_all = jnp.concatenate([ck, k_new_t], axis=1).astype(BF16)
        v_all = jnp.concatenate([cv, v_new_t], axis=1).astype(BF16)
        s = jnp.dot(lhs, k_all, preferred_element_type=F32)
        p, inv = _softmax_sink(s, mask, sink)
        o = lax.dot_general(p.astype(BF16), v_all, (((1,), (1,)), ((), ())),
                            preferred_element_type=F32) * inv
        groups = []
        for g in range(Q_PER_KV):
            acc = jnp.zeros((dec_seq, KV_WIDTH), F32)
            for h in range(N_KV_HEADS):
                piece = o[(Q_PER_KV * h + g) * dec_seq:(Q_PER_KV * h + g + 1) * dec_seq]
                acc = acc + jnp.where(quarter == h, piece, 0.0)
            groups.append(acc)
        a_scr[pl.ds(r0, dec_seq), :] = jnp.concatenate(groups, axis=1)
        shift = WINDOW - dec_seq - b * dec_seq
        keep = pos_lane < WINDOW - dec_seq
        kw_ref[b] = jnp.where(keep, pltpu.roll(ck, WINDOW - dec_seq, 1), pltpu.roll(k_new_t, shift, 1))
        vw_ref[b] = jnp.where(keep, pltpu.roll(cv, WINDOW - dec_seq, 1), pltpu.roll(v_new_t, shift, 1))
        return carry

    lax.fori_loop(0, ck_ref.shape[0], one_seq, 0)

    sgo = _spatial_gate(u_ref[...], g_ref[...], sgw_ref, bias_ref[...], dec_seq)
    mix_ref[...] = _normed_mix(a_scr[...], sgo, aon_ref[...], son_ref[...])


def _sample_mix(q, k, v, ck, cv, u, g, sink_col, sg_w, bias, aon, son, dec_seq):
    nseq = ck.shape[0]
    sb = SEQS_PER_STEP
    tm = sb * dec_seq
    assert tm == WINDOW == CHUNK and dec_seq % 8 == 0
    row = lambda i: (i, 0)
    seq3 = lambda i: (i, 0, 0)
    const2 = lambda i: (0, 0)
    const3 = lambda i: (0, 0, 0)
    return pl.pallas_call(
        functools.partial(_sample_mix_body, dec_seq=dec_seq),
        grid=(nseq // sb,),
        in_specs=[
            pl.BlockSpec((tm, ATTN_WIDTH), row),
            pl.BlockSpec((tm, KV_WIDTH), row),
            pl.BlockSpec((tm, KV_WIDTH), row),
            pl.BlockSpec((sb, KV_WIDTH, WINDOW), seq3),
            pl.BlockSpec((sb, KV_WIDTH, WINDOW), seq3),
            pl.BlockSpec((tm, SG_WIDTH), row),
            pl.BlockSpec((tm, SG_WIDTH), row),
            pl.BlockSpec(sink_col.shape, const2),
            pl.BlockSpec(sg_w.shape, const3),
            pl.BlockSpec(bias.shape, const2),
            pl.BlockSpec((1, ATTN_WIDTH), const2),
            pl.BlockSpec((1, SG_WIDTH), const2),
        ],
        out_specs=[
            pl.BlockSpec((tm, ATTN_WIDTH + SG_WIDTH), row),
            pl.BlockSpec((sb, KV_WIDTH, WINDOW), seq3),
            pl.BlockSpec((sb, KV_WIDTH, WINDOW), seq3),
        ],
        out_shape=[
            jax.ShapeDtypeStruct((nseq * dec_seq, ATTN_WIDTH + SG_WIDTH), BF16),
            jax.ShapeDtypeStruct(ck.shape, F32),
            jax.ShapeDtypeStruct(cv.shape, F32),
        ],
        scratch_shapes=[pltpu.VMEM((tm, ATTN_WIDTH), F32)],
        compiler_params=pltpu.CompilerParams(
            dimension_semantics=("arbitrary",), vmem_limit_bytes=VMEM_LIMIT_BYTES),
        name="sample_mix",
    )(q, k, v, ck, cv, u, g, sink_col, sg_w, bias, aon, son)


def _tail_body(x_ref, mix_ref, wo_ref, fn_ref, wg_ref, wu_ref, wd_ref, y_ref, hn_scr):
    @pl.when(pl.program_id(1) == 0)
    def _():
        h = x_ref[...] + jnp.dot(mix_ref[...], wo_ref[...], preferred_element_type=F32)
        hn_scr[...] = _rms(h, fn_ref[...]).astype(BF16)
        y_ref[...] = h

    hn = hn_scr[...]
    gate = jnp.dot(hn, wg_ref[...], preferred_element_type=F32)
    up = jnp.dot(hn, wu_ref[...], preferred_element_type=F32)
    act = (jax.nn.silu(gate) * up).astype(BF16)
    y_ref[...] += jnp.dot(act, wd_ref[...], preferred_element_type=F32)


def _tail(x, mix, wo, ffn_norm, wg, wu, wd):
    rows, d_model = x.shape
    d_ff = wg.shape[1]
    tm = min(ROW_TILE, rows)
    tf = FF_TILE
    return pl.pallas_call(
        _tail_body,
        grid=(rows // tm, d_ff // tf),
        in_specs=[
            pl.BlockSpec((tm, d_model), lambda i, j: (i, 0)),
            pl.BlockSpec((tm, mix.shape[1]), lambda i, j: (i, 0)),
            pl.BlockSpec(wo.shape, lambda i, j: (0, 0), pipeline_mode=pl.Buffered(1)),
            pl.BlockSpec((1, d_model), lambda i, j: (0, 0)),
            pl.BlockSpec((d_model, tf), lambda i, j: (0, j)),
            pl.BlockSpec((d_model, tf), lambda i, j: (0, j)),
            pl.BlockSpec((tf, d_model), lambda i, j: (j, 0)),
        ],
        out_specs=pl.BlockSpec((tm, d_model), lambda i, j: (i, 0)),
        out_shape=jax.ShapeDtypeStruct(x.shape, F32),
        scratch_shapes=[pltpu.VMEM((tm, d_model), BF16)],
        compiler_params=pltpu.CompilerParams(
            dimension_semantics=("arbitrary", "arbitrary"), vmem_limit_bytes=VMEM_LIMIT_BYTES),
        name="tail",
    )(x, mix, wo, ffn_norm, wg, wu, wd)


def _rope_tables(pos):
    half = HEAD_DIM // 2
    inv = ROPE_THETA ** (-jnp.arange(half, dtype=F32) / half)
    ang = pos.astype(F32)[:, None] * inv[None, :]
    cos, sin = jnp.cos(ang), jnp.sin(ang)
    reps = LANES // half
    cos_t = jnp.tile(cos, (1, reps))
    sin_t = jnp.tile(jnp.concatenate([-sin, sin], axis=1), (1, reps // 2))
    return cos_t, sin_t


def _windows_to_native(w):
    return jnp.transpose(w, (0, 2, 3, 1)).reshape(w.shape[0], KV_WIDTH, WINDOW)


def _windows_from_native(w):
    return jnp.transpose(w.reshape(w.shape[0], N_KV_HEADS, HEAD_DIM, WINDOW), (0, 3, 1, 2))[None]


def kernel(x_prompt, x_sample, cache_k_win, cache_v_win, attn_norm, w_in, q_norm, k_norm, sinks,
           sg_norm, sg_w, sg_b, attn_out_norm, sg_out_norm, w_o, ffn_norm, w_gate, w_up, w_down):
    assert w_in.shape[0] == 1, "single-layer step only"
    batch, seq, d_model = x_prompt.shape
    dec_batch, dec_seq, _ = x_sample.shape

    w_in_b = w_in[0].astype(BF16)
    qn2 = jnp.tile(q_norm, (1, LANES // HEAD_DIM))
    kn2 = jnp.tile(k_norm, (1, LANES // HEAD_DIM))
    sink_vec = sinks[0]

    cos_p, sin_p = _rope_tables(jnp.arange(seq, dtype=jnp.int32))
    cos_s, sin_s = _rope_tables(PAST_LEN + jnp.arange(dec_seq, dtype=jnp.int32))
    tm_s = min(ROW_TILE, dec_batch * dec_seq)
    cos_s = jnp.tile(cos_s, (tm_s // dec_seq, 1))
    sin_s = jnp.tile(sin_s, (tm_s // dec_seq, 1))

    bias_p = jnp.repeat(sg_b[0][:, :CHUNK].T, SG_HEAD_DIM, axis=1)
    seqs = CHUNK // dec_seq
    bias_s = jnp.repeat(jnp.tile(sg_b[0][:, :dec_seq].T, (seqs, 1)), SG_HEAD_DIM, axis=1)
    sgw_p = sg_w[0][:, :CHUNK, :CHUNK]
    sgw_s = jnp.tile(sg_w[0][:, :dec_seq, :dec_seq], (1, seqs, seqs))
    sink_col = jnp.repeat(sink_vec, dec_seq)[:, None]

    xp = x_prompt.reshape(batch * seq, d_model)
    q, k, v, u, g, kwin, vwin = _inproj(xp, attn_norm, w_in_b, qn2, kn2, sg_norm, cos_p, sin_p,
                                        BF16, seq)
    mix_p, wg, wu, wd, wo = _prompt_mix(sink_vec, q, k, v, u, g, sgw_p, bias_p, attn_out_norm,
                                        sg_out_norm, w_gate[0], w_up[0], w_down[0], w_o[0],
                                        batch, seq)
    y_prompt = _tail(xp, mix_p, wo, ffn_norm, wg, wu, wd).reshape(x_prompt.shape)
    k_win_prompt = _windows_from_native(kwin)
    v_win_prompt = _windows_from_native(vwin)

    xs = x_sample.reshape(dec_batch * dec_seq, d_model)
    q, k, v, u, g = _inproj(xs, attn_norm, w_in_b, qn2, kn2, sg_norm, cos_s, sin_s, F32, None)
    mix_s, kw, vw = _sample_mix(q, k, v, _windows_to_native(cache_k_win[0]),
                                _windows_to_native(cache_v_win[0]), u, g, sink_col, sgw_s, bias_s,
                                attn_out_norm, sg_out_norm, dec_seq)
    y_sample = _tail(xs, mix_s, wo, ffn_norm, wg, wu, wd).reshape(x_sample.shape)
    k_win_sample = _windows_from_native(kw)
    v_win_sample = _windows_from_native(vw)
    sg_v_sample = g.reshape(1, dec_batch, dec_seq, SG_WIDTH)

    return (y_prompt, y_sample, k_win_prompt, v_win_prompt, k_win_sample, v_win_sample, sg_v_sample)
```

```python
import functools

import jax
import jax.numpy as jnp
from jax import lax
from jax.experimental import pallas as pl
from jax.experimental.pallas import tpu as pltpu

HEAD_DIM = 64
N_KV_HEADS = 4
Q_PER_KV = 4
N_Q_HEADS = N_KV_HEADS * Q_PER_KV
KV_WIDTH = N_KV_HEADS * HEAD_DIM
ATTN_WIDTH = N_Q_HEADS * HEAD_DIM
WINDOW = 128
N_SG_HEADS = 8
SG_HEAD_DIM = 128
SG_WIDTH = N_SG_HEADS * SG_HEAD_DIM
CHUNK = 128
PAST_LEN = 16384
ROPE_THETA = 10000.0
EPS = 1e-6

LANES = 128
VMEM_LIMIT_BYTES = 56 * 1024 * 1024

ROW_TILE = 512
FF_TILE = 512
SEQS_PER_STEP = 16

F32 = jnp.float32
BF16 = jnp.bfloat16


def _rms(x, gain_row):
    r = lax.rsqrt(jnp.mean(x * x, axis=-1, keepdims=True) + EPS)
    return x * r * gain_row


def _swap_heads(t):
    lane = lax.broadcasted_iota(jnp.int32, (t.shape[0], LANES), 1)
    lo = lane < HEAD_DIM
    outs = []
    for j in range(ATTN_WIDTH // LANES):
        halves = []
        for half in range(2):
            c = 2 * j + half
            a = Q_PER_KV * (c % N_KV_HEADS) + c // N_KV_HEADS
            src = t[:, LANES * (a // 2):LANES * (a // 2 + 1)]
            halves.append(src if a % 2 == half else pltpu.roll(src, HEAD_DIM, 1))
        outs.append(jnp.where(lo, halves[0], halves[1]))
    return jnp.concatenate(outs, axis=1)


def _head_norm_rope(t, gain_row, cos, sin_signed):
    lane = lax.broadcasted_iota(jnp.int32, (t.shape[0], LANES), 1)
    lo_head = lane < HEAD_DIM
    first_half = (lane % HEAD_DIM) < (HEAD_DIM // 2)
    outs = []
    for j in range(t.shape[1] // LANES):
        blk = t[:, LANES * j:LANES * (j + 1)]
        sq = blk * blk
        s_all = jnp.sum(sq, axis=-1, keepdims=True)
        s_lo = jnp.sum(jnp.where(lo_head, sq, 0.0), axis=-1, keepdims=True)
        s_hi = s_all - s_lo
        r = jnp.where(lo_head,
                      lax.rsqrt(s_lo * (1.0 / HEAD_DIM) + EPS),
                      lax.rsqrt(s_hi * (1.0 / HEAD_DIM) + EPS))
        y = blk * r * gain_row
        rot = jnp.where(first_half,
                        pltpu.roll(y, LANES - HEAD_DIM // 2, 1),
                        pltpu.roll(y, HEAD_DIM // 2, 1))
        outs.append(y * cos + rot * sin_signed)
    return jnp.concatenate(outs, axis=1)


def _inproj_body(x_ref, an_ref, w_ref, qn_ref, kn_ref, sgn_ref, cos_ref, sin_ref,
                 q_ref, k_ref, v_ref, u_ref, g_ref, *win_refs, tiles_per_seq):
    xn = _rms(x_ref[...], an_ref[...]).astype(BF16)
    cos = cos_ref[...]
    sin = sin_ref[...]
    c0, c1, c2, c3 = ATTN_WIDTH, ATTN_WIDTH + KV_WIDTH, ATTN_WIDTH + 2 * KV_WIDTH, \
        ATTN_WIDTH + 2 * KV_WIDTH + SG_WIDTH

    q = jnp.dot(xn, w_ref[:, 0:c0], preferred_element_type=F32)
    q = _head_norm_rope(q, qn_ref[...], cos, sin) * (HEAD_DIM ** -0.5)
    q_ref[...] = _swap_heads(q).astype(q_ref.dtype)

    k = _head_norm_rope(jnp.dot(xn, w_ref[:, c0:c1], preferred_element_type=F32),
                        kn_ref[...], cos, sin)
    k_ref[...] = k.astype(k_ref.dtype)
    v = jnp.dot(xn, w_ref[:, c1:c2], preferred_element_type=F32)
    v_ref[...] = v.astype(v_ref.dtype)

    if win_refs:
        kwin_ref, vwin_ref = win_refs

        @pl.when(pl.program_id(0) % tiles_per_seq == tiles_per_seq - 1)
        def _():
            kwin_ref[0] = k[k.shape[0] - WINDOW:].T
            vwin_ref[0] = v[v.shape[0] - WINDOW:].T

    u = jnp.dot(xn, w_ref[:, c2:c3], preferred_element_type=F32)
    u_ref[...] = jax.nn.gelu(u).astype(u_ref.dtype)

    g = jax.nn.gelu(jnp.dot(xn, w_ref[:, c3:], preferred_element_type=F32))
    g_ref[...] = _rms(g, sgn_ref[...]).astype(g_ref.dtype)


def _inproj(x, attn_norm, w_in, qn2, kn2, sg_norm, cos_t, sin_t, act_dtype, seq):
    rows, d_model = x.shape
    tm = min(ROW_TILE, rows)
    n_tab = cos_t.shape[0] // tm
    const = lambda i: (0, 0)
    row = lambda i: (i, 0)
    tab = lambda i: (i % n_tab, 0)
    out_specs = [
        pl.BlockSpec((tm, ATTN_WIDTH), row),
        pl.BlockSpec((tm, KV_WIDTH), row),
        pl.BlockSpec((tm, KV_WIDTH), row),
        pl.BlockSpec((tm, SG_WIDTH), row),
        pl.BlockSpec((tm, SG_WIDTH), row),
    ]
    out_shape = [
        jax.ShapeDtypeStruct((rows, ATTN_WIDTH), act_dtype),
        jax.ShapeDtypeStruct((rows, KV_WIDTH), act_dtype),
        jax.ShapeDtypeStruct((rows, KV_WIDTH), act_dtype),
        jax.ShapeDtypeStruct((rows, SG_WIDTH), BF16),
        jax.ShapeDtypeStruct((rows, SG_WIDTH), act_dtype),
    ]
    tiles_per_seq = 1
    if seq is not None:
        tiles_per_seq = seq // tm
        win = lambda i: (i // tiles_per_seq, 0, 0)
        out_specs += [pl.BlockSpec((1, KV_WIDTH, WINDOW), win)] * 2
        out_shape += [jax.ShapeDtypeStruct((rows // seq, KV_WIDTH, WINDOW), F32)] * 2
    return pl.pallas_call(
        functools.partial(_inproj_body, tiles_per_seq=tiles_per_seq),
        grid=(rows // tm,),
        in_specs=[
            pl.BlockSpec((tm, d_model), row),
            pl.BlockSpec((1, d_model), const),
            pl.BlockSpec(w_in.shape, const, pipeline_mode=pl.Buffered(1)),
            pl.BlockSpec((1, LANES), const),
            pl.BlockSpec((1, LANES), const),
            pl.BlockSpec((1, SG_WIDTH), const),
            pl.BlockSpec((tm, LANES), tab),
            pl.BlockSpec((tm, LANES), tab),
        ],
        out_specs=out_specs,
        out_shape=out_shape,
        compiler_params=pltpu.CompilerParams(
            dimension_semantics=("arbitrary",), vmem_limit_bytes=VMEM_LIMIT_BYTES),
        name="inproj",
    )(x, attn_norm, w_in, qn2, kn2, sg_norm, cos_t, sin_t)


def _softmax_sink(s, mask, sink):
    s = jnp.where(mask, s, -jnp.inf)
    m = jnp.maximum(jnp.max(s, axis=-1, keepdims=True), sink)
    p = jnp.exp(s - m)
    denom = jnp.sum(p, axis=-1, keepdims=True) + jnp.exp(sink - m)
    return p, 1.0 / denom


def _spatial_gate(u, g, w_ref, bias, rows_per_seq):
    r = lax.broadcasted_iota(jnp.int32, (CHUNK, CHUNK), 0)
    c = lax.broadcasted_iota(jnp.int32, (CHUNK, CHUNK), 1)
    causal = (r // rows_per_seq == c // rows_per_seq) & (c % rows_per_seq <= r % rows_per_seq)
    gb = g.astype(BF16)
    periodic = ((r < rows_per_seq) & (c % rows_per_seq == r)).astype(BF16)
    outs = []
    for h in range(N_SG_HEADS):
        if rows_per_seq == CHUNK:
            w = w_ref[h]
        else:
            w = jnp.dot(w_ref[h, 0:rows_per_seq, :].astype(BF16), periodic, preferred_element_type=F32)
            w = jnp.concatenate([w] * (CHUNK // rows_per_seq), axis=0)
        w = jnp.where(causal, w, 0.0).astype(BF16)
        cols = slice(SG_HEAD_DIM * h, SG_HEAD_DIM * (h + 1))
        parts = []
        for ch in range(u.shape[0] // CHUNK):
            rws = slice(CHUNK * ch, CHUNK * (ch + 1))
            mixed = jnp.dot(w, gb[rws, cols], preferred_element_type=F32) + bias[:, cols]
            parts.append(u[rws, cols].astype(F32) * mixed)
        outs.append(jnp.concatenate(parts, axis=0))
    return jnp.concatenate(outs, axis=1)


def _normed_mix(a_swapped, sgo, aon, son):
    a = _swap_heads(a_swapped)
    return jnp.concatenate([_rms(a, aon), _rms(sgo, son)], axis=1).astype(BF16)


def _prompt_mix_body(sinks_ref, q_ref, k_ref, v_ref, u_ref, g_ref, sgw_ref, bias_ref,
                     aon_ref, son_ref, wg32, wu32, wd32, wo32,
                     mix_ref, wg16, wu16, wd16, wo16, a_scr):
    wg16[...] = wg32[...].astype(BF16)
    wu16[...] = wu32[...].astype(BF16)
    wd16[...] = wd32[...].astype(BF16)
    wo16[...] = wo32[...].astype(BF16)

    j = pl.program_id(1)
    blocks = q_ref.shape[0] // WINDOW
    quarter = lax.broadcasted_iota(jnp.int32, (WINDOW, KV_WIDTH), 1) // HEAD_DIM
    qi = lax.broadcasted_iota(jnp.int32, (Q_PER_KV * WINDOW, WINDOW), 0) % WINDOW
    kj = lax.broadcasted_iota(jnp.int32, (Q_PER_KV * WINDOW, WINDOW), 1)
    from_prev = kj > qi
    row_g = lax.broadcasted_iota(jnp.int32, (Q_PER_KV * WINDOW, 1), 0) // WINDOW
    sink_cols = []
    for h in range(N_KV_HEADS):
        col = jnp.zeros((Q_PER_KV * WINDOW, 1), F32)
        for g in range(Q_PER_KV):
            col = jnp.where(row_g == g, sinks_ref[Q_PER_KV * h + g], col)
        sink_cols.append(col)

    def block(r, carry):
        n = j * blocks + r
        prev = pl.multiple_of(jnp.maximum(n - 1, 0) * WINDOW, WINDOW)
        cur = pl.multiple_of(n * WINDOW, WINDOW)
        no_prev = jnp.where(n > 0, 0.0, -jnp.inf)
        q = q_ref[pl.ds(pl.multiple_of(r * WINDOW, WINDOW), WINDOW), :]
        kb = jnp.concatenate([k_ref[pl.ds(prev, WINDOW), :], k_ref[pl.ds(cur, WINDOW), :]], axis=0)
        vb = jnp.concatenate([v_ref[pl.ds(prev, WINDOW), :], v_ref[pl.ds(cur, WINDOW), :]], axis=0)
        groups = [jnp.zeros((WINDOW, KV_WIDTH), F32) for _ in range(Q_PER_KV)]
        for h in range(N_KV_HEADS):
            sel = quarter == h
            lhs = jnp.concatenate(
                [jnp.where(sel, q[:, KV_WIDTH * g:KV_WIDTH * (g + 1)], 0) for g in range(Q_PER_KV)],
                axis=0)
            s = lax.dot_general(lhs, kb, (((1,), (1,)), ((), ())), preferred_element_type=F32)
            s = jnp.where(from_prev, s[:, :WINDOW] + no_prev, s[:, WINDOW:])
            sink = sink_cols[h]
            m = jnp.maximum(jnp.max(s, axis=-1, keepdims=True), sink)
            p = jnp.exp(s - m)
            inv = 1.0 / (jnp.sum(p, axis=-1, keepdims=True) + jnp.exp(sink - m))
            p_band = jnp.concatenate(
                [jnp.where(from_prev, p, 0.0), jnp.where(from_prev, 0.0, p)], axis=1).astype(BF16)
            o = jnp.dot(p_band, vb, preferred_element_type=F32) * inv
            for g in range(Q_PER_KV):
                groups[g] = groups[g] + jnp.where(sel, o[WINDOW * g:WINDOW * (g + 1)], 0.0)
        a_scr[pl.ds(pl.multiple_of(r * WINDOW, WINDOW), WINDOW), :] = jnp.concatenate(groups, axis=1)
        return carry

    lax.fori_loop(0, blocks, block, 0, unroll=2)

    sgo = _spatial_gate(u_ref[...], g_ref[...], sgw_ref, bias_ref[...], CHUNK)
    mix_ref[...] = _normed_mix(a_scr[...], sgo, aon_ref[...], son_ref[...])


def _prompt_mix(sinks, q, k, v, u, g, sg_w, bias, aon, son, wg, wu, wd, wo, batch, seq):
    tm = ROW_TILE
    tiles = seq // tm
    steps = batch * tiles
    d_model = wo.shape[1]
    row = lambda b, j: (b * tiles + j, 0)
    per_seq = lambda b, j: (b, 0)
    const2 = lambda b, j: (0, 0)
    const3 = lambda b, j: (0, 0, 0)
    slab = lambda w: pl.BlockSpec((w.shape[0] // steps, w.shape[1]), row)
    bf16_like = lambda w: jax.ShapeDtypeStruct(w.shape, BF16)
    return pl.pallas_call(
        _prompt_mix_body,
        grid=(batch, tiles),
        in_specs=[
            pl.BlockSpec(memory_space=pltpu.SMEM),
            pl.BlockSpec((tm, ATTN_WIDTH), row),
            pl.BlockSpec((seq, KV_WIDTH), per_seq),
            pl.BlockSpec((seq, KV_WIDTH), per_seq),
            pl.BlockSpec((tm, SG_WIDTH), row),
            pl.BlockSpec((tm, SG_WIDTH), row),
            pl.BlockSpec(sg_w.shape, const3),
            pl.BlockSpec(bias.shape, const2),
            pl.BlockSpec((1, ATTN_WIDTH), const2),
            pl.BlockSpec((1, SG_WIDTH), const2),
            slab(wg), slab(wu), slab(wd), slab(wo),
        ],
        out_specs=[pl.BlockSpec((tm, d_model), row), slab(wg), slab(wu), slab(wd), slab(wo)],
        out_shape=[jax.ShapeDtypeStruct((batch * seq, d_model), BF16),
                   bf16_like(wg), bf16_like(wu), bf16_like(wd), bf16_like(wo)],
        scratch_shapes=[pltpu.VMEM((tm, ATTN_WIDTH), F32)],
        compiler_params=pltpu.CompilerParams(
            dimension_semantics=("arbitrary", "arbitrary"), vmem_limit_bytes=VMEM_LIMIT_BYTES),
        name="prompt_mix",
    )(sinks, q, k, v, u, g, sg_w, bias, aon, son, wg, wu, wd, wo)


def _sample_mix_body(q_ref, k_ref, v_ref, ck_ref, cv_ref, u_ref, g_ref, sink_ref, sgw_ref,
                     bias_ref, aon_ref, son_ref, mix_ref, kw_ref, vw_ref, a_scr, *, dec_seq):
    rows = N_Q_HEADS * dec_seq
    quarter = lax.broadcasted_iota(jnp.int32, (dec_seq, KV_WIDTH), 1) // HEAD_DIM
    t = lax.broadcasted_iota(jnp.int32, (rows, 2 * WINDOW), 0) % dec_seq
    kj = lax.broadcasted_iota(jnp.int32, (rows, 2 * WINDOW), 1)
    pos_lane = lax.broadcasted_iota(jnp.int32, (KV_WIDTH, WINDOW), 1)
    sink = sink_ref[...]
    k_new_t = k_ref[...].T
    v_new_t = v_ref[...].T

    def one_seq(b, carry):
        r0 = pl.multiple_of(b * dec_seq, dec_seq)
        q = q_ref[pl.ds(r0, dec_seq), :]
        ck = ck_ref[b]
        cv = cv_ref[b]
        own = kj - WINDOW - b * dec_seq
        mask = ((kj < WINDOW) & (kj > t)) | ((own >= 0) & (own <= t))
        lhs = jnp.concatenate(
            [jnp.where(quarter == h, q[:, KV_WIDTH * g:KV_WIDTH * (g + 1)], 0.0)
             for h in range(N_KV_HEADS) for g in range(Q_PER_KV)], axis=0).astype(BF16)
        k_all = jnp.concatenate([ck, k_new_t], axis=1).astype(BF16)
        v_all = jnp.concatenate([cv, v_new_t], axis=1).astype(BF16)
        s = jnp.dot(lhs, k_all, preferred_element_type=F32)
        p, inv = _softmax_sink(s, mask, sink)
        o = lax.dot_general(p.astype(BF16), v_all, (((1,), (1,)), ((), ())),
                            preferred_element_type=F32) * inv
        groups = []
        for g in range(Q_PER_KV):
            acc = jnp.zeros((dec_seq, KV_WIDTH), F32)
            for h in range(N_KV_HEADS):
                piece = o[(Q_PER_KV * h + g) * dec_seq:(Q_PER_KV * h + g + 1) * dec_seq]
                acc = acc + jnp.where(quarter == h, piece, 0.0)
            groups.append(acc)
        a_scr[pl.ds(r0, dec_seq), :] = jnp.concatenate(groups, axis=1)
        shift = WINDOW - dec_seq - b * dec_seq
        keep = pos_lane < WINDOW - dec_seq
        kw_ref[b] = jnp.where(keep, pltpu.roll(ck, WINDOW - dec_seq, 1), pltpu.roll(k_new_t, shift, 1))
        vw_ref[b] = jnp.where(keep, pltpu.roll(cv, WINDOW - dec_seq, 1), pltpu.roll(v_new_t, shift, 1))
        return carry

    lax.fori_loop(0, ck_ref.shape[0], one_seq, 0, unroll=4)

    sgo = _spatial_gate(u_ref[...], g_ref[...], sgw_ref, bias_ref[...], dec_seq)
    mix_ref[...] = _normed_mix(a_scr[...], sgo, aon_ref[...], son_ref[...])


def _sample_mix(q, k, v, ck, cv, u, g, sink_col, sg_w, bias, aon, son, dec_seq):
    nseq = ck.shape[0]
    sb = SEQS_PER_STEP
    tm = sb * dec_seq
    assert tm == WINDOW == CHUNK and dec_seq % 8 == 0
    row = lambda i: (i, 0)
    seq3 = lambda i: (i, 0, 0)
    const2 = lambda i: (0, 0)
    const3 = lambda i: (0, 0, 0)
    return pl.pallas_call(
        functools.partial(_sample_mix_body, dec_seq=dec_seq),
        grid=(nseq // sb,),
        in_specs=[
            pl.BlockSpec((tm, ATTN_WIDTH), row),
            pl.BlockSpec((tm, KV_WIDTH), row),
            pl.BlockSpec((tm, KV_WIDTH), row),
            pl.BlockSpec((sb, KV_WIDTH, WINDOW), seq3),
            pl.BlockSpec((sb, KV_WIDTH, WINDOW), seq3),
            pl.BlockSpec((tm, SG_WIDTH), row),
            pl.BlockSpec((tm, SG_WIDTH), row),
            pl.BlockSpec(sink_col.shape, const2),
            pl.BlockSpec(sg_w.shape, const3),
            pl.BlockSpec(bias.shape, const2),
            pl.BlockSpec((1, ATTN_WIDTH), const2),
            pl.BlockSpec((1, SG_WIDTH), const2),
        ],
        out_specs=[
            pl.BlockSpec((tm, ATTN_WIDTH + SG_WIDTH), row),
            pl.BlockSpec((sb, KV_WIDTH, WINDOW), seq3),
            pl.BlockSpec((sb, KV_WIDTH, WINDOW), seq3),
        ],
        out_shape=[
            jax.ShapeDtypeStruct((nseq * dec_seq, ATTN_WIDTH + SG_WIDTH), BF16),
            jax.ShapeDtypeStruct(ck.shape, F32),
            jax.ShapeDtypeStruct(cv.shape, F32),
        ],
        scratch_shapes=[pltpu.VMEM((tm, ATTN_WIDTH), F32)],
        compiler_params=pltpu.CompilerParams(
            dimension_semantics=("arbitrary",), vmem_limit_bytes=VMEM_LIMIT_BYTES),
        name="sample_mix",
    )(q, k, v, ck, cv, u, g, sink_col, sg_w, bias, aon, son)


def _tail_body(x_ref, mix_ref, wo_ref, fn_ref, wg_ref, wu_ref, wd_ref, y_ref, hn_scr):
    @pl.when(pl.program_id(1) == 0)
    def _():
        h = x_ref[...] + jnp.dot(mix_ref[...], wo_ref[...], preferred_element_type=F32)
        hn_scr[...] = _rms(h, fn_ref[...]).astype(BF16)
        y_ref[...] = h

    hn = hn_scr[...]
    gate = jnp.dot(hn, wg_ref[...], preferred_element_type=F32)
    up = jnp.dot(hn, wu_ref[...], preferred_element_type=F32)
    act = (jax.nn.silu(gate) * up).astype(BF16)
    y_ref[...] += jnp.dot(act, wd_ref[...], preferred_element_type=F32)


def _tail(x, mix, wo, ffn_norm, wg, wu, wd):
    rows, d_model = x.shape
    d_ff = wg.shape[1]
    tm = min(ROW_TILE, rows)
    tf = FF_TILE
    return pl.pallas_call(
        _tail_body,
        grid=(rows // tm, d_ff // tf),
        in_specs=[
            pl.BlockSpec((tm, d_model), lambda i, j: (i, 0)),
            pl.BlockSpec((tm, mix.shape[1]), lambda i, j: (i, 0)),
            pl.BlockSpec(wo.shape, lambda i, j: (0, 0), pipeline_mode=pl.Buffered(1)),
            pl.BlockSpec((1, d_model), lambda i, j: (0, 0)),
            pl.BlockSpec((d_model, tf), lambda i, j: (0, j)),
            pl.BlockSpec((d_model, tf), lambda i, j: (0, j)),
            pl.BlockSpec((tf, d_model), lambda i, j: (j, 0)),
        ],
        out_specs=pl.BlockSpec((tm, d_model), lambda i, j: (i, 0)),
        out_shape=jax.ShapeDtypeStruct(x.shape, F32),
        scratch_shapes=[pltpu.VMEM((tm, d_model), BF16)],
        compiler_params=pltpu.CompilerParams(
            dimension_semantics=("arbitrary", "arbitrary"), vmem_limit_bytes=VMEM_LIMIT_BYTES),
        name="tail",
    )(x, mix, wo, ffn_norm, wg, wu, wd)


def _rope_tables(pos):
    half = HEAD_DIM // 2
    inv = ROPE_THETA ** (-jnp.arange(half, dtype=F32) / half)
    ang = pos.astype(F32)[:, None] * inv[None, :]
    cos, sin = jnp.cos(ang), jnp.sin(ang)
    reps = LANES // half
    cos_t = jnp.tile(cos, (1, reps))
    sin_t = jnp.tile(jnp.concatenate([-sin, sin], axis=1), (1, reps // 2))
    return cos_t, sin_t


def _windows_to_native(w):
    return jnp.transpose(w, (0, 2, 3, 1)).reshape(w.shape[0], KV_WIDTH, WINDOW)


def _windows_from_native(w):
    return jnp.transpose(w.reshape(w.shape[0], N_KV_HEADS, HEAD_DIM, WINDOW), (0, 3, 1, 2))[None]


def kernel(x_prompt, x_sample, cache_k_win, cache_v_win, attn_norm, w_in, q_norm, k_norm, sinks,
           sg_norm, sg_w, sg_b, attn_out_norm, sg_out_norm, w_o, ffn_norm, w_gate, w_up, w_down):
    assert w_in.shape[0] == 1, "single-layer step only"
    batch, seq, d_model = x_prompt.shape
    dec_batch, dec_seq, _ = x_sample.shape

    w_in_b = w_in[0].astype(BF16)
    qn2 = jnp.tile(q_norm, (1, LANES // HEAD_DIM))
    kn2 = jnp.tile(k_norm, (1, LANES // HEAD_DIM))
    sink_vec = sinks[0]

    cos_p, sin_p = _rope_tables(jnp.arange(seq, dtype=jnp.int32))
    cos_s, sin_s = _rope_tables(PAST_LEN + jnp.arange(dec_seq, dtype=jnp.int32))
    tm_s = min(ROW_TILE, dec_batch * dec_seq)
    cos_s = jnp.tile(cos_s, (tm_s // dec_seq, 1))
    sin_s = jnp.tile(sin_s, (tm_s // dec_seq, 1))

    bias_p = jnp.repeat(sg_b[0][:, :CHUNK].T, SG_HEAD_DIM, axis=1)
    seqs = CHUNK // dec_seq
    bias_s = jnp.repeat(jnp.tile(sg_b[0][:, :dec_seq].T, (seqs, 1)), SG_HEAD_DIM, axis=1)
    sgw = sg_w[0][:, :CHUNK, :CHUNK]
    sink_col = jnp.repeat(sink_vec, dec_seq)[:, None]

    xp = x_prompt.reshape(batch * seq, d_model)
    q, k, v, u, g, kwin, vwin = _inproj(xp, attn_norm, w_in_b, qn2, kn2, sg_norm, cos_p, sin_p,
                                        BF16, seq)
    mix_p, wg, wu, wd, wo = _prompt_mix(sink_vec, q, k, v, u, g, sgw, bias_p, attn_out_norm,
                                        sg_out_norm, w_gate[0], w_up[0], w_down[0], w_o[0],
                                        batch, seq)
    y_prompt = _tail(xp, mix_p, wo, ffn_norm, wg, wu, wd).reshape(x_prompt.shape)
    k_win_prompt = _windows_from_native(kwin)
    v_win_prompt = _windows_from_native(vwin)

    xs = x_sample.reshape(dec_batch * dec_seq, d_model)
    q, k, v, u, g = _inproj(xs, attn_norm, w_in_b, qn2, kn2, sg_norm, cos_s, sin_s, F32, None)
    mix_s, kw, vw = _sample_mix(q, k, v, _windows_to_native(cache_k_win[0]),
                                _windows_to_native(cache_v_win[0]), u, g, sink_col, sgw, bias_s,
                                attn_out_norm, sg_out_norm, dec_seq)
    y_sample = _tail(xs, mix_s, wo, ffn_norm, wg, wu, wd).reshape(x_sample.shape)
    k_win_sample = _windows_from_native(kw)
    v_win_sample = _windows_from_native(vw)
    sg_v_sample = g.reshape(1, dec_batch, dec_seq, SG_WIDTH)

    return (y_prompt, y_sample, k_win_prompt, v_win_prompt, k_win_sample, v_win_sample, sg_v_sample)
```

```python
import functools

import jax
import jax.numpy as jnp
from jax import lax
from jax.experimental import pallas as pl
from jax.experimental.pallas import tpu as pltpu

HEAD_DIM = 64
N_KV_HEADS = 4
Q_PER_KV = 4
N_Q_HEADS = N_KV_HEADS * Q_PER_KV
KV_WIDTH = N_KV_HEADS * HEAD_DIM
ATTN_WIDTH = N_Q_HEADS * HEAD_DIM
WINDOW = 128
N_SG_HEADS = 8
SG_HEAD_DIM = 128
SG_WIDTH = N_SG_HEADS * SG_HEAD_DIM
CHUNK = 128
PAST_LEN = 16384
ROPE_THETA = 10000.0
EPS = 1e-6

LANES = 128
VMEM_LIMIT_BYTES = 56 * 1024 * 1024

ROW_TILE = 512
INPROJ_SUB_BLOCKS = 4
TAIL_SUB_BLOCKS = 2
FF_TILE = 512
SEQS_PER_STEP = 16

F32 = jnp.float32
BF16 = jnp.bfloat16


def _rms(x, gain_row):
    r = lax.rsqrt(jnp.mean(x * x, axis=-1, keepdims=True) + EPS)
    return x * r * gain_row


def _swap_heads(t):
    lane = lax.broadcasted_iota(jnp.int32, (t.shape[0], LANES), 1)
    lo = lane < HEAD_DIM
    outs = []
    for j in range(ATTN_WIDTH // LANES):
        halves = []
        for half in range(2):
            c = 2 * j + half
            a = Q_PER_KV * (c % N_KV_HEADS) + c // N_KV_HEADS
            src = t[:, LANES * (a // 2):LANES * (a // 2 + 1)]
            halves.append(src if a % 2 == half else pltpu.roll(src, HEAD_DIM, 1))
        outs.append(jnp.where(lo, halves[0], halves[1]))
    return jnp.concatenate(outs, axis=1)


def _head_norm_rope(t, gain_row, cos, sin_signed):
    lane = lax.broadcasted_iota(jnp.int32, (t.shape[0], LANES), 1)
    lo_head = lane < HEAD_DIM
    first_half = (lane % HEAD_DIM) < (HEAD_DIM // 2)
    outs = []
    for j in range(t.shape[1] // LANES):
        blk = t[:, LANES * j:LANES * (j + 1)]
        sq = blk * blk
        s_all = jnp.sum(sq, axis=-1, keepdims=True)
        s_lo = jnp.sum(jnp.where(lo_head, sq, 0.0), axis=-1, keepdims=True)
        s_hi = s_all - s_lo
        r = jnp.where(lo_head,
                      lax.rsqrt(s_lo * (1.0 / HEAD_DIM) + EPS),
                      lax.rsqrt(s_hi * (1.0 / HEAD_DIM) + EPS))
        y = blk * r * gain_row
        rot = jnp.where(first_half,
                        pltpu.roll(y, LANES - HEAD_DIM // 2, 1),
                        pltpu.roll(y, HEAD_DIM // 2, 1))
        outs.append(y * cos + rot * sin_signed)
    return jnp.concatenate(outs, axis=1)


def _inproj_body(x_ref, an_ref, w_ref, qn_ref, kn_ref, sgn_ref, cos_ref, sin_ref,
                 q_ref, k_ref, v_ref, u_ref, g_ref, *rest, tiles_per_seq):
    proj_scr = rest[-1]
    n_sub, sub = proj_scr.shape[0], proj_scr.shape[1]
    c0, c1, c2, c3 = ATTN_WIDTH, ATTN_WIDTH + KV_WIDTH, ATTN_WIDTH + 2 * KV_WIDTH, \
        ATTN_WIDTH + 2 * KV_WIDTH + SG_WIDTH

    def project(r):
        xn = _rms(x_ref[sub * r:sub * (r + 1), :], an_ref[...]).astype(BF16)
        proj_scr[r] = jnp.dot(xn, w_ref[...], preferred_element_type=F32)

    def finish(r):
        rows = slice(sub * r, sub * (r + 1))
        cos = cos_ref[rows, :]
        sin = sin_ref[rows, :]
        q = _head_norm_rope(proj_scr[r, :, 0:c0], qn_ref[...], cos, sin) * (HEAD_DIM ** -0.5)
        q_ref[rows, :] = _swap_heads(q).astype(q_ref.dtype)
        k = _head_norm_rope(proj_scr[r, :, c0:c1], kn_ref[...], cos, sin)
        k_ref[rows, :] = k.astype(k_ref.dtype)
        v = proj_scr[r, :, c1:c2]
        v_ref[rows, :] = v.astype(v_ref.dtype)
        u_ref[rows, :] = jax.nn.gelu(proj_scr[r, :, c2:c3]).astype(u_ref.dtype)
        g = jax.nn.gelu(proj_scr[r, :, c3:])
        g_ref[rows, :] = _rms(g, sgn_ref[...]).astype(g_ref.dtype)
        return k, v

    project(0)
    for r in range(1, n_sub):
        project(r)
        finish(r - 1)
    k, v = finish(n_sub - 1)

    if len(rest) > 1:
        kwin_ref, vwin_ref = rest[0], rest[1]

        @pl.when(pl.program_id(0) % tiles_per_seq == tiles_per_seq - 1)
        def _():
            kwin_ref[0] = k[sub - WINDOW:].T
            vwin_ref[0] = v[sub - WINDOW:].T


def _inproj(x, attn_norm, w_in, qn2, kn2, sg_norm, cos_t, sin_t, act_dtype, seq):
    rows, d_model = x.shape
    tm = min(ROW_TILE, rows)
    n_tab = cos_t.shape[0] // tm
    const = lambda i: (0, 0)
    row = lambda i: (i, 0)
    tab = lambda i: (i % n_tab, 0)
    out_specs = [
        pl.BlockSpec((tm, ATTN_WIDTH), row),
        pl.BlockSpec((tm, KV_WIDTH), row),
        pl.BlockSpec((tm, KV_WIDTH), row),
        pl.BlockSpec((tm, SG_WIDTH), row),
        pl.BlockSpec((tm, SG_WIDTH), row),
    ]
    out_shape = [
        jax.ShapeDtypeStruct((rows, ATTN_WIDTH), act_dtype),
        jax.ShapeDtypeStruct((rows, KV_WIDTH), act_dtype),
        jax.ShapeDtypeStruct((rows, KV_WIDTH), act_dtype),
        jax.ShapeDtypeStruct((rows, SG_WIDTH), BF16),
        jax.ShapeDtypeStruct((rows, SG_WIDTH), act_dtype),
    ]
    tiles_per_seq = 1
    if seq is not None:
        tiles_per_seq = seq // tm
        win = lambda i: (i // tiles_per_seq, 0, 0)
        out_specs += [pl.BlockSpec((1, KV_WIDTH, WINDOW), win)] * 2
        out_shape += [jax.ShapeDtypeStruct((rows // seq, KV_WIDTH, WINDOW), F32)] * 2
    return pl.pallas_call(
        functools.partial(_inproj_body, tiles_per_seq=tiles_per_seq),
        grid=(rows // tm,),
        in_specs=[
            pl.BlockSpec((tm, d_model), row),
            pl.BlockSpec((1, d_model), const),
            pl.BlockSpec(w_in.shape, const, pipeline_mode=pl.Buffered(1)),
            pl.BlockSpec((1, LANES), const),
            pl.BlockSpec((1, LANES), const),
            pl.BlockSpec((1, SG_WIDTH), const),
            pl.BlockSpec((tm, LANES), tab),
            pl.BlockSpec((tm, LANES), tab),
        ],
        out_specs=out_specs,
        out_shape=out_shape,
        scratch_shapes=[pltpu.VMEM((INPROJ_SUB_BLOCKS, tm // INPROJ_SUB_BLOCKS, w_in.shape[1]), F32)],
        compiler_params=pltpu.CompilerParams(
            dimension_semantics=("arbitrary",), vmem_limit_bytes=VMEM_LIMIT_BYTES),
        name="inproj",
    )(x, attn_norm, w_in, qn2, kn2, sg_norm, cos_t, sin_t)


def _softmax_sink(s, mask, sink):
    s = jnp.where(mask, s, -jnp.inf)
    m = jnp.maximum(jnp.max(s, axis=-1, keepdims=True), sink)
    p = jnp.exp(s - m)
    denom = jnp.sum(p, axis=-1, keepdims=True) + jnp.exp(sink - m)
    return p, 1.0 / denom


def _spatial_gate(u, g, w_ref, bias, rows_per_seq):
    r = lax.broadcasted_iota(jnp.int32, (CHUNK, CHUNK), 0)
    c = lax.broadcasted_iota(jnp.int32, (CHUNK, CHUNK), 1)
    causal = (r // rows_per_seq == c // rows_per_seq) & (c % rows_per_seq <= r % rows_per_seq)
    gb = g.astype(BF16)
    periodic = ((r < rows_per_seq) & (c % rows_per_seq == r)).astype(BF16)
    outs = []
    for h in range(N_SG_HEADS):
        if rows_per_seq == CHUNK:
            w = w_ref[h]
        else:
            w = jnp.dot(w_ref[h, 0:rows_per_seq, :].astype(BF16), periodic, preferred_element_type=F32)
            w = jnp.concatenate([w] * (CHUNK // rows_per_seq), axis=0)
        w = jnp.where(causal, w, 0.0).astype(BF16)
        cols = slice(SG_HEAD_DIM * h, SG_HEAD_DIM * (h + 1))
        parts = []
        for ch in range(u.shape[0] // CHUNK):
            rws = slice(CHUNK * ch, CHUNK * (ch + 1))
            mixed = jnp.dot(w, gb[rws, cols], preferred_element_type=F32) + bias[:, cols]
            parts.append(u[rws, cols].astype(F32) * mixed)
        outs.append(jnp.concatenate(parts, axis=0))
    return jnp.concatenate(outs, axis=1)


def _normed_mix(a_swapped, sgo, aon, son):
    a = _swap_heads(a_swapped)
    return jnp.concatenate([_rms(a, aon), _rms(sgo, son)], axis=1).astype(BF16)


def _prompt_mix_body(sinks_ref, q_ref, k_ref, v_ref, u_ref, g_ref, sgw_ref, bias_ref,
                     aon_ref, son_ref, wg32, wu32, wd32, wo32,
                     mix_ref, wg16, wu16, wd16, wo16, a_scr):
    wg16[...] = wg32[...].astype(BF16)
    wu16[...] = wu32[...].astype(BF16)
    wd16[...] = wd32[...].astype(BF16)
    wo16[...] = wo32[...].astype(BF16)

    j = pl.program_id(1)
    blocks = q_ref.shape[0] // WINDOW
    quarter = lax.broadcasted_iota(jnp.int32, (WINDOW, KV_WIDTH), 1) // HEAD_DIM
    qi = lax.broadcasted_iota(jnp.int32, (Q_PER_KV * WINDOW, WINDOW), 0) % WINDOW
    kj = lax.broadcasted_iota(jnp.int32, (Q_PER_KV * WINDOW, WINDOW), 1)
    from_prev = kj > qi
    row_g = lax.broadcasted_iota(jnp.int32, (Q_PER_KV * WINDOW, 1), 0) // WINDOW
    sink_cols = []
    for h in range(N_KV_HEADS):
        col = jnp.zeros((Q_PER_KV * WINDOW, 1), F32)
        for g in range(Q_PER_KV):
            col = jnp.where(row_g == g, sinks_ref[Q_PER_KV * h + g], col)
        sink_cols.append(col)

    def block(r, carry):
        n = j * blocks + r
        prev = pl.multiple_of(jnp.maximum(n - 1, 0) * WINDOW, WINDOW)
        cur = pl.multiple_of(n * WINDOW, WINDOW)
        no_prev = jnp.where(n > 0, 0.0, -jnp.inf)
        q = q_ref[pl.ds(pl.multiple_of(r * WINDOW, WINDOW), WINDOW), :]
        kb = jnp.concatenate([k_ref[pl.ds(prev, WINDOW), :], k_ref[pl.ds(cur, WINDOW), :]], axis=0)
        vb = jnp.concatenate([v_ref[pl.ds(prev, WINDOW), :], v_ref[pl.ds(cur, WINDOW), :]], axis=0)
        groups = [jnp.zeros((WINDOW, KV_WIDTH), F32) for _ in range(Q_PER_KV)]
        for h in range(N_KV_HEADS):
            sel = quarter == h
            lhs = jnp.concatenate(
                [jnp.where(sel, q[:, KV_WIDTH * g:KV_WIDTH * (g + 1)], 0) for g in range(Q_PER_KV)],
                axis=0)
            s = lax.dot_general(lhs, kb, (((1,), (1,)), ((), ())), preferred_element_type=F32)
            s = jnp.where(from_prev, s[:, :WINDOW] + no_prev, s[:, WINDOW:])
            sink = sink_cols[h]
            m = jnp.maximum(jnp.max(s, axis=-1, keepdims=True), sink)
            p = jnp.exp(s - m)
            inv = 1.0 / (jnp.sum(p, axis=-1, keepdims=True) + jnp.exp(sink - m))
            p_band = jnp.concatenate(
                [jnp.where(from_prev, p, 0.0), jnp.where(from_prev, 0.0, p)], axis=1).astype(BF16)
            o = jnp.dot(p_band, vb, preferred_element_type=F32) * inv
            for g in range(Q_PER_KV):
                groups[g] = groups[g] + jnp.where(sel, o[WINDOW * g:WINDOW * (g + 1)], 0.0)
        a_scr[pl.ds(pl.multiple_of(r * WINDOW, WINDOW), WINDOW), :] = jnp.concatenate(groups, axis=1)
        return carry

    lax.fori_loop(0, blocks, block, 0, unroll=2)

    sgo = _spatial_gate(u_ref[...], g_ref[...], sgw_ref, bias_ref[...], CHUNK)
    mix_ref[...] = _normed_mix(a_scr[...], sgo, aon_ref[...], son_ref[...])


def _prompt_mix(sinks, q, k, v, u, g, sg_w, bias, aon, son, wg, wu, wd, wo, batch, seq):
    tm = ROW_TILE
    tiles = seq // tm
    steps = batch * tiles
    d_model = wo.shape[1]
    row = lambda b, j: (b * tiles + j, 0)
    per_seq = lambda b, j: (b, 0)
    const2 = lambda b, j: (0, 0)
    const3 = lambda b, j: (0, 0, 0)
    slab = lambda w: pl.BlockSpec((w.shape[0] // steps, w.shape[1]), row)
    bf16_like = lambda w: jax.ShapeDtypeStruct(w.shape, BF16)
    return pl.pallas_call(
        _prompt_mix_body,
        grid=(batch, tiles),
        in_specs=[
            pl.BlockSpec(memory_space=pltpu.SMEM),
            pl.BlockSpec((tm, ATTN_WIDTH), row),
            pl.BlockSpec((seq, KV_WIDTH), per_seq),
            pl.BlockSpec((seq, KV_WIDTH), per_seq),
            pl.BlockSpec((tm, SG_WIDTH), row),
            pl.BlockSpec((tm, SG_WIDTH), row),
            pl.BlockSpec(sg_w.shape, const3),
            pl.BlockSpec(bias.shape, const2),
            pl.BlockSpec((1, ATTN_WIDTH), const2),
            pl.BlockSpec((1, SG_WIDTH), const2),
            slab(wg), slab(wu), slab(wd), slab(wo),
        ],
        out_specs=[pl.BlockSpec((tm, d_model), row), slab(wg), slab(wu), slab(wd), slab(wo)],
        out_shape=[jax.ShapeDtypeStruct((batch * seq, d_model), BF16),
                   bf16_like(wg), bf16_like(wu), bf16_like(wd), bf16_like(wo)],
        scratch_shapes=[pltpu.VMEM((tm, ATTN_WIDTH), F32)],
        compiler_params=pltpu.CompilerParams(
            dimension_semantics=("arbitrary", "arbitrary"), vmem_limit_bytes=VMEM_LIMIT_BYTES),
        name="prompt_mix",
    )(sinks, q, k, v, u, g, sg_w, bias, aon, son, wg, wu, wd, wo)


def _sample_mix_body(q_ref, k_ref, v_ref, ck_ref, cv_ref, u_ref, g_ref, sink_ref, sgw_ref,
                     bias_ref, aon_ref, son_ref, mix_ref, kw_ref, vw_ref, a_scr, *, dec_seq):
    rows = N_Q_HEADS * dec_seq
    quarter = lax.broadcasted_iota(jnp.int32, (dec_seq, KV_WIDTH), 1) // HEAD_DIM
    t = lax.broadcasted_iota(jnp.int32, (rows, 2 * WINDOW), 0) % dec_seq
    kj = lax.broadcasted_iota(jnp.int32, (rows, 2 * WINDOW), 1)
    pos_lane = lax.broadcasted_iota(jnp.int32, (KV_WIDTH, WINDOW), 1)
    sink = sink_ref[...]
    k_new_t = k_ref[...].T
    v_new_t = v_ref[...].T

    def one_seq(b, carry):
        r0 = pl.multiple_of(b * dec_seq, dec_seq)
        q = q_ref[pl.ds(r0, dec_seq), :]
        ck = ck_ref[b]
        cv = cv_ref[b]
        own = kj - WINDOW - b * dec_seq
        mask = ((kj < WINDOW) & (kj > t)) | ((own >= 0) & (own <= t))
        lhs = jnp.concatenate(
            [jnp.where(quarter == h, q[:, KV_WIDTH * g:KV_WIDTH * (g + 1)], 0.0)
             for h in range(N_KV_HEADS) for g in range(Q_PER_KV)], axis=0).astype(BF16)
        k_all = jnp.concatenate([ck, k_new_t], axis=1).astype(BF16)
        v_all = jnp.concatenate([cv, v_new_t], axis=1).astype(BF16)
        s = jnp.dot(lhs, k_all, preferred_element_type=F32)
        p, inv = _softmax_sink(s, mask, sink)
        o = lax.dot_general(p.astype(BF16), v_all, (((1,), (1,)), ((), ())),
                            preferred_element_type=F32) * inv
        groups = []
        for g in range(Q_PER_KV):
            acc = jnp.zeros((dec_seq, KV_WIDTH), F32)
            for h in range(N_KV_HEADS):
                piece = o[(Q_PER_KV * h + g) * dec_seq:(Q_PER_KV * h + g + 1) * dec_seq]
                acc = acc + jnp.where(quarter == h, piece, 0.0)
            groups.append(acc)
        a_scr[pl.ds(r0, dec_seq), :] = jnp.concatenate(groups, axis=1)
        shift = WINDOW - dec_seq - b * dec_seq
        keep = pos_lane < WINDOW - dec_seq
        kw_ref[b] = jnp.where(keep, pltpu.roll(ck, WINDOW - dec_seq, 1), pltpu.roll(k_new_t, shift, 1))
        vw_ref[b] = jnp.where(keep, pltpu.roll(cv, WINDOW - dec_seq, 1), pltpu.roll(v_new_t, shift, 1))
        return carry

    lax.fori_loop(0, ck_ref.shape[0], one_seq, 0, unroll=4)

    sgo = _spatial_gate(u_ref[...], g_ref[...], sgw_ref, bias_ref[...], dec_seq)
    mix_ref[...] = _normed_mix(a_scr[...], sgo, aon_ref[...], son_ref[...])


def _sample_mix(q, k, v, ck, cv, u, g, sink_col, sg_w, bias, aon, son, dec_seq):
    nseq = ck.shape[0]
    sb = SEQS_PER_STEP
    tm = sb * dec_seq
    assert tm == WINDOW == CHUNK and dec_seq % 8 == 0
    row = lambda i: (i, 0)
    seq3 = lambda i: (i, 0, 0)
    const2 = lambda i: (0, 0)
    const3 = lambda i: (0, 0, 0)
    return pl.pallas_call(
        functools.partial(_sample_mix_body, dec_seq=dec_seq),
        grid=(nseq // sb,),
        in_specs=[
            pl.BlockSpec((tm, ATTN_WIDTH), row),
            pl.BlockSpec((tm, KV_WIDTH), row),
            pl.BlockSpec((tm, KV_WIDTH), row),
            pl.BlockSpec((sb, KV_WIDTH, WINDOW), seq3),
            pl.BlockSpec((sb, KV_WIDTH, WINDOW), seq3),
            pl.BlockSpec((tm, SG_WIDTH), row),
            pl.BlockSpec((tm, SG_WIDTH), row),
            pl.BlockSpec(sink_col.shape, const2),
            pl.BlockSpec(sg_w.shape, const3),
            pl.BlockSpec(bias.shape, const2),
            pl.BlockSpec((1, ATTN_WIDTH), const2),
            pl.BlockSpec((1, SG_WIDTH), const2),
        ],
        out_specs=[
            pl.BlockSpec((tm, ATTN_WIDTH + SG_WIDTH), row),
            pl.BlockSpec((sb, KV_WIDTH, WINDOW), seq3),
            pl.BlockSpec((sb, KV_WIDTH, WINDOW), seq3),
        ],
        out_shape=[
            jax.ShapeDtypeStruct((nseq * dec_seq, ATTN_WIDTH + SG_WIDTH), BF16),
            jax.ShapeDtypeStruct(ck.shape, F32),
            jax.ShapeDtypeStruct(cv.shape, F32),
        ],
        scratch_shapes=[pltpu.VMEM((tm, ATTN_WIDTH), F32)],
        compiler_params=pltpu.CompilerParams(
            dimension_semantics=("arbitrary",), vmem_limit_bytes=VMEM_LIMIT_BYTES),
        name="sample_mix",
    )(q, k, v, ck, cv, u, g, sink_col, sg_w, bias, aon, son)


def _tail_body(x_ref, mix_ref, wo_ref, fn_ref, wg_ref, wu_ref, wd_ref, y_ref, hn_scr, gu_scr, act_scr):
    n_sub, sub = act_scr.shape[0], act_scr.shape[1]
    rows = [slice(sub * r, sub * (r + 1)) for r in range(n_sub)]

    @pl.when(pl.program_id(1) == 0)
    def _():
        def out_proj(r):
            y_ref[rows[r], :] = x_ref[rows[r], :] + jnp.dot(
                mix_ref[rows[r], :], wo_ref[...], preferred_element_type=F32)

        def norm(r):
            hn_scr[rows[r], :] = _rms(y_ref[rows[r], :], fn_ref[...]).astype(BF16)

        out_proj(0)
        for r in range(1, n_sub):
            out_proj(r)
            norm(r - 1)
        norm(n_sub - 1)

    def gate_up(r):
        hn = hn_scr[rows[r], :]
        gu_scr[r, 0] = jnp.dot(hn, wg_ref[...], preferred_element_type=F32)
        gu_scr[r, 1] = jnp.dot(hn, wu_ref[...], preferred_element_type=F32)

    def activate(r):
        act_scr[r] = (jax.nn.silu(gu_scr[r, 0]) * gu_scr[r, 1]).astype(BF16)

    def down(r):
        y_ref[rows[r], :] += jnp.dot(act_scr[r], wd_ref[...], preferred_element_type=F32)

    gate_up(0)
    for r in range(1, n_sub):
        gate_up(r)
        activate(r - 1)
        down(r - 1)
    activate(n_sub - 1)
    down(n_sub - 1)


def _tail(x, mix, wo, ffn_norm, wg, wu, wd):
    rows, d_model = x.shape
    d_ff = wg.shape[1]
    tm = min(ROW_TILE, rows)
    tf = FF_TILE
    return pl.pallas_call(
        _tail_body,
        grid=(rows // tm, d_ff // tf),
        in_specs=[
            pl.BlockSpec((tm, d_model), lambda i, j: (i, 0)),
            pl.BlockSpec((tm, mix.shape[1]), lambda i, j: (i, 0)),
            pl.BlockSpec(wo.shape, lambda i, j: (0, 0), pipeline_mode=pl.Buffered(1)),
            pl.BlockSpec((1, d_model), lambda i, j: (0, 0)),
            pl.BlockSpec((d_model, tf), lambda i, j: (0, j)),
            pl.BlockSpec((d_model, tf), lambda i, j: (0, j)),
            pl.BlockSpec((tf, d_model), lambda i, j: (j, 0)),
        ],
        out_specs=pl.BlockSpec((tm, d_model), lambda i, j: (i, 0)),
        out_shape=jax.ShapeDtypeStruct(x.shape, F32),
        scratch_shapes=[pltpu.VMEM((tm, d_model), BF16),
                        pltpu.VMEM((TAIL_SUB_BLOCKS, 2, tm // TAIL_SUB_BLOCKS, tf), F32),
                        pltpu.VMEM((TAIL_SUB_BLOCKS, tm // TAIL_SUB_BLOCKS, tf), BF16)],
        compiler_params=pltpu.CompilerParams(
            dimension_semantics=("arbitrary", "arbitrary"), vmem_limit_bytes=VMEM_LIMIT_BYTES),
        name="tail",
    )(x, mix, wo, ffn_norm, wg, wu, wd)


def _rope_tables(pos):
    half = HEAD_DIM // 2
    inv = ROPE_THETA ** (-jnp.arange(half, dtype=F32) / half)
    ang = pos.astype(F32)[:, None] * inv[None, :]
    cos, sin = jnp.cos(ang), jnp.sin(ang)
    reps = LANES // half
    cos_t = jnp.tile(cos, (1, reps))
    sin_t = jnp.tile(jnp.concatenate([-sin, sin], axis=1), (1, reps // 2))
    return cos_t, sin_t


def _windows_to_native(w):
    return jnp.transpose(w, (0, 2, 3, 1)).reshape(w.shape[0], KV_WIDTH, WINDOW)


def _windows_from_native(w):
    return jnp.transpose(w.reshape(w.shape[0], N_KV_HEADS, HEAD_DIM, WINDOW), (0, 3, 1, 2))[None]


def kernel(x_prompt, x_sample, cache_k_win, cache_v_win, attn_norm, w_in, q_norm, k_norm, sinks,
           sg_norm, sg_w, sg_b, attn_out_norm, sg_out_norm, w_o, ffn_norm, w_gate, w_up, w_down):
    assert w_in.shape[0] == 1, "single-layer step only"
    batch, seq, d_model = x_prompt.shape
    dec_batch, dec_seq, _ = x_sample.shape

    w_in_b = w_in[0].astype(BF16)
    qn2 = jnp.tile(q_norm, (1, LANES // HEAD_DIM))
    kn2 = jnp.tile(k_norm, (1, LANES // HEAD_DIM))
    sink_vec = sinks[0]

    cos_p, sin_p = _rope_tables(jnp.arange(seq, dtype=jnp.int32))
    cos_s, sin_s = _rope_tables(PAST_LEN + jnp.arange(dec_seq, dtype=jnp.int32))
    tm_s = min(ROW_TILE, dec_batch * dec_seq)
    cos_s = jnp.tile(cos_s, (tm_s // dec_seq, 1))
    sin_s = jnp.tile(sin_s, (tm_s // dec_seq, 1))

    bias_p = jnp.repeat(sg_b[0][:, :CHUNK].T, SG_HEAD_DIM, axis=1)
    seqs = CHUNK // dec_seq
    bias_s = jnp.repeat(jnp.tile(sg_b[0][:, :dec_seq].T, (seqs, 1)), SG_HEAD_DIM, axis=1)
    sgw = sg_w[0][:, :CHUNK, :CHUNK]
    sink_col = jnp.repeat(sink_vec, dec_seq)[:, None]

    xp = x_prompt.reshape(batch * seq, d_model)
    q, k, v, u, g, kwin, vwin = _inproj(xp, attn_norm, w_in_b, qn2, kn2, sg_norm, cos_p, sin_p,
                                        BF16, seq)
    mix_p, wg, wu, wd, wo = _prompt_mix(sink_vec, q, k, v, u, g, sgw, bias_p, attn_out_norm,
                                        sg_out_norm, w_gate[0], w_up[0], w_down[0], w_o[0],
                                        batch, seq)
    y_prompt = _tail(xp, mix_p, wo, ffn_norm, wg, wu, wd).reshape(x_prompt.shape)
    k_win_prompt = _windows_from_native(kwin)
    v_win_prompt = _windows_from_native(vwin)

    xs = x_sample.reshape(dec_batch * dec_seq, d_model)
    q, k, v, u, g = _inproj(xs, attn_norm, w_in_b, qn2, kn2, sg_norm, cos_s, sin_s, F32, None)
    mix_s, kw, vw = _sample_mix(q, k, v, _windows_to_native(cache_k_win[0]),
                                _windows_to_native(cache_v_win[0]), u, g, sink_col, sgw, bias_s,
                                attn_out_norm, sg_out_norm, dec_seq)
    y_sample = _tail(xs, mix_s, wo, ffn_norm, wg, wu, wd).reshape(x_sample.shape)
    k_win_sample = _windows_from_native(kw)
    v_win_sample = _windows_from_native(vw)
    sg_v_sample = g.reshape(1, dec_batch, dec_seq, SG_WIDTH)

    return (y_prompt, y_sample, k_win_prompt, v_win_prompt, k_win_sample, v_win_sample, sg_v_sample)
```

```python
import functools

import jax
import jax.numpy as jnp
from jax import lax
from jax.experimental import pallas as pl
from jax.experimental.pallas import tpu as pltpu

HEAD_DIM = 64
N_KV_HEADS = 4
Q_PER_KV = 4
N_Q_HEADS = N_KV_HEADS * Q_PER_KV
KV_WIDTH = N_KV_HEADS * HEAD_DIM
ATTN_WIDTH = N_Q_HEADS * HEAD_DIM
WINDOW = 128
N_SG_HEADS = 8
SG_HEAD_DIM = 128
SG_WIDTH = N_SG_HEADS * SG_HEAD_DIM
CHUNK = 128
PAST_LEN = 16384
ROPE_THETA = 10000.0
EPS = 1e-6

LANES = 128
VMEM_LIMIT_BYTES = 56 * 1024 * 1024

ROW_TILE = 512
INPROJ_SUB_BLOCKS = 4
FF_TILE = 512
SEQS_PER_STEP = 16

F32 = jnp.float32
BF16 = jnp.bfloat16


def _rms(x, gain_row):
    r = lax.rsqrt(jnp.mean(x * x, axis=-1, keepdims=True) + EPS)
    return x * r * gain_row


def _swap_heads(t):
    lane = lax.broadcasted_iota(jnp.int32, (t.shape[0], LANES), 1)
    lo = lane < HEAD_DIM
    outs = []
    for j in range(ATTN_WIDTH // LANES):
        halves = []
        for half in range(2):
            c = 2 * j + half
            a = Q_PER_KV * (c % N_KV_HEADS) + c // N_KV_HEADS
            src = t[:, LANES * (a // 2):LANES * (a // 2 + 1)]
            halves.append(src if a % 2 == half else pltpu.roll(src, HEAD_DIM, 1))
        outs.append(jnp.where(lo, halves[0], halves[1]))
    return jnp.concatenate(outs, axis=1)


def _head_norm_rope(t, gain_row, cos, sin_signed):
    lane = lax.broadcasted_iota(jnp.int32, (t.shape[0], LANES), 1)
    lo_head = lane < HEAD_DIM
    first_half = (lane % HEAD_DIM) < (HEAD_DIM // 2)
    outs = []
    for j in range(t.shape[1] // LANES):
        blk = t[:, LANES * j:LANES * (j + 1)]
        sq = blk * blk
        s_all = jnp.sum(sq, axis=-1, keepdims=True)
        s_lo = jnp.sum(jnp.where(lo_head, sq, 0.0), axis=-1, keepdims=True)
        s_hi = s_all - s_lo
        r = jnp.where(lo_head,
                      lax.rsqrt(s_lo * (1.0 / HEAD_DIM) + EPS),
                      lax.rsqrt(s_hi * (1.0 / HEAD_DIM) + EPS))
        y = blk * r * gain_row
        rot = jnp.where(first_half,
                        pltpu.roll(y, LANES - HEAD_DIM // 2, 1),
                        pltpu.roll(y, HEAD_DIM // 2, 1))
        outs.append(y * cos + rot * sin_signed)
    return jnp.concatenate(outs, axis=1)


def _inproj_body(x_ref, an_ref, w_ref, qn_ref, kn_ref, sgn_ref, cos_ref, sin_ref, *rest,
                 tiles_per_seq, n_cast):
    casts_in, rest = rest[:n_cast], rest[n_cast:]
    q_ref, k_ref, v_ref, u_ref, g_ref = rest[:5]
    casts_out, rest = rest[5:5 + n_cast], rest[5 + n_cast:]
    proj_scr = rest[-1]
    n_sub, sub = proj_scr.shape[0], proj_scr.shape[1]
    c0, c1, c2, c3 = ATTN_WIDTH, ATTN_WIDTH + KV_WIDTH, ATTN_WIDTH + 2 * KV_WIDTH, \
        ATTN_WIDTH + 2 * KV_WIDTH + SG_WIDTH

    def project(r):
        xn = _rms(x_ref[sub * r:sub * (r + 1), :], an_ref[...]).astype(BF16)
        proj_scr[r] = jnp.dot(xn, w_ref[...], preferred_element_type=F32)

    def finish(r):
        rows = slice(sub * r, sub * (r + 1))
        cos = cos_ref[rows, :]
        sin = sin_ref[rows, :]
        q = _head_norm_rope(proj_scr[r, :, 0:c0], qn_ref[...], cos, sin) * (HEAD_DIM ** -0.5)
        q_ref[rows, :] = _swap_heads(q).astype(q_ref.dtype)
        k = _head_norm_rope(proj_scr[r, :, c0:c1], kn_ref[...], cos, sin)
        k_ref[rows, :] = k.astype(k_ref.dtype)
        v = proj_scr[r, :, c1:c2]
        v_ref[rows, :] = v.astype(v_ref.dtype)
        u_ref[rows, :] = jax.nn.gelu(proj_scr[r, :, c2:c3]).astype(u_ref.dtype)
        g = jax.nn.gelu(proj_scr[r, :, c3:])
        g_ref[rows, :] = _rms(g, sgn_ref[...]).astype(g_ref.dtype)
        return k, v

    project(0)
    for src, dst in zip(casts_in, casts_out):
        dst[...] = src[...].astype(BF16)
    for r in range(1, n_sub):
        project(r)
        finish(r - 1)
    k, v = finish(n_sub - 1)

    if len(rest) > 1:
        kwin_ref, vwin_ref = rest[0], rest[1]

        @pl.when(pl.program_id(0) % tiles_per_seq == tiles_per_seq - 1)
        def _():
            kwin_ref[0] = k[sub - WINDOW:].T
            vwin_ref[0] = v[sub - WINDOW:].T


def _inproj(x, attn_norm, w_in, qn2, kn2, sg_norm, cos_t, sin_t, act_dtype, seq, casts=()):
    rows, d_model = x.shape
    tm = min(ROW_TILE, rows)
    steps = rows // tm
    n_tab = cos_t.shape[0] // tm
    const = lambda i: (0, 0)
    row = lambda i: (i, 0)
    tab = lambda i: (i % n_tab, 0)
    slabs = [pl.BlockSpec((w.shape[0] // steps, w.shape[1]), row) for w in casts]
    out_specs = [
        pl.BlockSpec((tm, ATTN_WIDTH), row),
        pl.BlockSpec((tm, KV_WIDTH), row),
        pl.BlockSpec((tm, KV_WIDTH), row),
        pl.BlockSpec((tm, SG_WIDTH), row),
        pl.BlockSpec((tm, SG_WIDTH), row),
    ]
    out_shape = [
        jax.ShapeDtypeStruct((rows, ATTN_WIDTH), act_dtype),
        jax.ShapeDtypeStruct((rows, KV_WIDTH), act_dtype),
        jax.ShapeDtypeStruct((rows, KV_WIDTH), act_dtype),
        jax.ShapeDtypeStruct((rows, SG_WIDTH), BF16),
        jax.ShapeDtypeStruct((rows, SG_WIDTH), act_dtype),
    ]
    out_specs += slabs
    out_shape += [jax.ShapeDtypeStruct(w.shape, BF16) for w in casts]
    tiles_per_seq = 1
    if seq is not None:
        tiles_per_seq = seq // tm
        win = lambda i: (i // tiles_per_seq, 0, 0)
        out_specs += [pl.BlockSpec((1, KV_WIDTH, WINDOW), win)] * 2
        out_shape += [jax.ShapeDtypeStruct((rows // seq, KV_WIDTH, WINDOW), F32)] * 2
    return pl.pallas_call(
        functools.partial(_inproj_body, tiles_per_seq=tiles_per_seq, n_cast=len(casts)),
        grid=(steps,),
        in_specs=[
            pl.BlockSpec((tm, d_model), row),
            pl.BlockSpec((1, d_model), const),
            pl.BlockSpec(w_in.shape, const, pipeline_mode=pl.Buffered(1)),
            pl.BlockSpec((1, LANES), const),
            pl.BlockSpec((1, LANES), const),
            pl.BlockSpec((1, SG_WIDTH), const),
            pl.BlockSpec((tm, LANES), tab),
            pl.BlockSpec((tm, LANES), tab),
        ] + slabs,
        out_specs=out_specs,
        out_shape=out_shape,
        scratch_shapes=[pltpu.VMEM((INPROJ_SUB_BLOCKS, tm // INPROJ_SUB_BLOCKS, w_in.shape[1]), F32)],
        compiler_params=pltpu.CompilerParams(
            dimension_semantics=("arbitrary",), vmem_limit_bytes=VMEM_LIMIT_BYTES),
        name="inproj",
    )(x, attn_norm, w_in, qn2, kn2, sg_norm, cos_t, sin_t, *casts)


def _softmax_sink(s, mask, sink):
    s = jnp.where(mask, s, -jnp.inf)
    m = jnp.maximum(jnp.max(s, axis=-1, keepdims=True), sink)
    p = jnp.exp(s - m)
    denom = jnp.sum(p, axis=-1, keepdims=True) + jnp.exp(sink - m)
    return p, 1.0 / denom


def _spatial_gate(u, g, w_ref, bias, rows_per_seq):
    r = lax.broadcasted_iota(jnp.int32, (CHUNK, CHUNK), 0)
    c = lax.broadcasted_iota(jnp.int32, (CHUNK, CHUNK), 1)
    causal = (r // rows_per_seq == c // rows_per_seq) & (c % rows_per_seq <= r % rows_per_seq)
    gb = g.astype(BF16)
    periodic = ((r < rows_per_seq) & (c % rows_per_seq == r)).astype(BF16)
    outs = []
    for h in range(N_SG_HEADS):
        if rows_per_seq == CHUNK:
            w = w_ref[h]
        else:
            w = jnp.dot(w_ref[h, 0:rows_per_seq, :].astype(BF16), periodic, preferred_element_type=F32)
            w = jnp.concatenate([w] * (CHUNK // rows_per_seq), axis=0)
        w = jnp.where(causal, w, 0.0).astype(BF16)
        cols = slice(SG_HEAD_DIM * h, SG_HEAD_DIM * (h + 1))
        parts = []
        for ch in range(u.shape[0] // CHUNK):
            rws = slice(CHUNK * ch, CHUNK * (ch + 1))
            mixed = jnp.dot(w, gb[rws, cols], preferred_element_type=F32) + bias[:, cols]
            parts.append(u[rws, cols].astype(F32) * mixed)
        outs.append(jnp.concatenate(parts, axis=0))
    return jnp.concatenate(outs, axis=1)


def _normed_mix(a_swapped, sgo, aon, son):
    a = _swap_heads(a_swapped)
    return jnp.concatenate([_rms(a, aon), _rms(sgo, son)], axis=1).astype(BF16)


def _prompt_mix_body(sinks_ref, q_ref, k_ref, v_ref, u_ref, g_ref, sgw_ref, bias_ref,
                     aon_ref, son_ref, wd32, wo32, mix_ref, wd16, wo16, a_scr):
    wd16[...] = wd32[...].astype(BF16)
    wo16[...] = wo32[...].astype(BF16)

    j = pl.program_id(1)
    blocks = q_ref.shape[0] // WINDOW
    quarter = lax.broadcasted_iota(jnp.int32, (WINDOW, KV_WIDTH), 1) // HEAD_DIM
    qi = lax.broadcasted_iota(jnp.int32, (Q_PER_KV * WINDOW, WINDOW), 0) % WINDOW
    kj = lax.broadcasted_iota(jnp.int32, (Q_PER_KV * WINDOW, WINDOW), 1)
    from_prev = kj > qi
    row_g = lax.broadcasted_iota(jnp.int32, (Q_PER_KV * WINDOW, 1), 0) // WINDOW
    sink_cols = []
    for h in range(N_KV_HEADS):
        col = jnp.zeros((Q_PER_KV * WINDOW, 1), F32)
        for g in range(Q_PER_KV):
            col = jnp.where(row_g == g, sinks_ref[Q_PER_KV * h + g], col)
        sink_cols.append(col)

    def block(r, carry):
        n = j * blocks + r
        prev = pl.multiple_of(jnp.maximum(n - 1, 0) * WINDOW, WINDOW)
        cur = pl.multiple_of(n * WINDOW, WINDOW)
        no_prev = jnp.where(n > 0, 0.0, -jnp.inf)
        q = q_ref[pl.ds(pl.multiple_of(r * WINDOW, WINDOW), WINDOW), :]
        kb = jnp.concatenate([k_ref[pl.ds(prev, WINDOW), :], k_ref[pl.ds(cur, WINDOW), :]], axis=0)
        vb = jnp.concatenate([v_ref[pl.ds(prev, WINDOW), :], v_ref[pl.ds(cur, WINDOW), :]], axis=0)
        groups = None
        for h in range(N_KV_HEADS):
            sel = quarter == h
            lhs = jnp.concatenate(
                [jnp.where(sel, q[:, KV_WIDTH * g:KV_WIDTH * (g + 1)], 0) for g in range(Q_PER_KV)],
                axis=0)
            s = lax.dot_general(lhs, kb, (((1,), (1,)), ((), ())), preferred_element_type=F32)
            s = jnp.where(from_prev, s[:, :WINDOW] + no_prev, s[:, WINDOW:])
            sink = sink_cols[h]
            m = jnp.maximum(jnp.max(s, axis=-1, keepdims=True), sink)
            p = jnp.exp(s - m)
            inv = 1.0 / (jnp.sum(p, axis=-1, keepdims=True) + jnp.exp(sink - m))
            p_band = jnp.concatenate(
                [jnp.where(from_prev, p, 0.0), jnp.where(from_prev, 0.0, p)], axis=1).astype(BF16)
            o = jnp.dot(p_band, vb, preferred_element_type=F32) * inv
            parts = [o[WINDOW * g:WINDOW * (g + 1)] for g in range(Q_PER_KV)]
            groups = parts if groups is None else [
                jnp.where(sel, part, acc) for part, acc in zip(parts, groups)]
        a_scr[pl.ds(pl.multiple_of(r * WINDOW, WINDOW), WINDOW), :] = jnp.concatenate(groups, axis=1)
        return carry

    lax.fori_loop(0, blocks, block, 0, unroll=2)

    sgo = _spatial_gate(u_ref[...], g_ref[...], sgw_ref, bias_ref[...], CHUNK)
    mix_ref[...] = _normed_mix(a_scr[...], sgo, aon_ref[...], son_ref[...])


def _prompt_mix(sinks, q, k, v, u, g, sg_w, bias, aon, son, wd, wo, batch, seq):
    tm = ROW_TILE
    tiles = seq // tm
    steps = batch * tiles
    d_model = wo.shape[1]
    row = lambda b, j: (b * tiles + j, 0)
    per_seq = lambda b, j: (b, 0)
    const2 = lambda b, j: (0, 0)
    const3 = lambda b, j: (0, 0, 0)
    slab = lambda w: pl.BlockSpec((w.shape[0] // steps, w.shape[1]), row)
    bf16_like = lambda w: jax.ShapeDtypeStruct(w.shape, BF16)
    return pl.pallas_call(
        _prompt_mix_body,
        grid=(batch, tiles),
        in_specs=[
            pl.BlockSpec(memory_space=pltpu.SMEM),
            pl.BlockSpec((tm, ATTN_WIDTH), row),
            pl.BlockSpec((seq, KV_WIDTH), per_seq),
            pl.BlockSpec((seq, KV_WIDTH), per_seq),
            pl.BlockSpec((tm, SG_WIDTH), row),
            pl.BlockSpec((tm, SG_WIDTH), row),
            pl.BlockSpec(sg_w.shape, const3),
            pl.BlockSpec(bias.shape, const2),
            pl.BlockSpec((1, ATTN_WIDTH), const2),
            pl.BlockSpec((1, SG_WIDTH), const2),
            slab(wd), slab(wo),
        ],
        out_specs=[pl.BlockSpec((tm, d_model), row), slab(wd), slab(wo)],
        out_shape=[jax.ShapeDtypeStruct((batch * seq, d_model), BF16), bf16_like(wd), bf16_like(wo)],
        scratch_shapes=[pltpu.VMEM((tm, ATTN_WIDTH), F32)],
        compiler_params=pltpu.CompilerParams(
            dimension_semantics=("arbitrary", "arbitrary"), vmem_limit_bytes=VMEM_LIMIT_BYTES),
        name="prompt_mix",
    )(sinks, q, k, v, u, g, sg_w, bias, aon, son, wd, wo)


def _sample_mix_body(q_ref, k_ref, v_ref, ck_ref, cv_ref, u_ref, g_ref, sink_ref, sgw_ref,
                     bias_ref, aon_ref, son_ref, mix_ref, kw_ref, vw_ref, a_scr, *, dec_seq):
    rows = N_Q_HEADS * dec_seq
    quarter = lax.broadcasted_iota(jnp.int32, (dec_seq, KV_WIDTH), 1) // HEAD_DIM
    t = lax.broadcasted_iota(jnp.int32, (rows, 2 * WINDOW), 0) % dec_seq
    kj = lax.broadcasted_iota(jnp.int32, (rows, 2 * WINDOW), 1)
    pos_lane = lax.broadcasted_iota(jnp.int32, (KV_WIDTH, WINDOW), 1)
    sink = sink_ref[...]
    k_new_t = k_ref[...].T
    v_new_t = v_ref[...].T

    def one_seq(b, carry):
        r0 = pl.multiple_of(b * dec_seq, dec_seq)
        q = q_ref[pl.ds(r0, dec_seq), :]
        ck = ck_ref[b]
        cv = cv_ref[b]
        own = kj - WINDOW - b * dec_seq
        mask = ((kj < WINDOW) & (kj > t)) | ((own >= 0) & (own <= t))
        lhs = jnp.concatenate(
            [jnp.where(quarter == h, q[:, KV_WIDTH * g:KV_WIDTH * (g + 1)], 0.0)
             for h in range(N_KV_HEADS) for g in range(Q_PER_KV)], axis=0).astype(BF16)
        k_all = jnp.concatenate([ck, k_new_t], axis=1).astype(BF16)
        v_all = jnp.concatenate([cv, v_new_t], axis=1).astype(BF16)
        s = jnp.dot(lhs, k_all, preferred_element_type=F32)
        p, inv = _softmax_sink(s, mask, sink)
        o = lax.dot_general(p.astype(BF16), v_all, (((1,), (1,)), ((), ())),
                            preferred_element_type=F32) * inv
        groups = []
        for g in range(Q_PER_KV):
            piece = lambda h: o[(Q_PER_KV * h + g) * dec_seq:(Q_PER_KV * h + g + 1) * dec_seq]
            acc = piece(N_KV_HEADS - 1)
            for h in range(N_KV_HEADS - 2, -1, -1):
                acc = jnp.where(quarter == h, piece(h), acc)
            groups.append(acc)
        a_scr[pl.ds(r0, dec_seq), :] = jnp.concatenate(groups, axis=1)
        shift = WINDOW - dec_seq - b * dec_seq
        keep = pos_lane < WINDOW - dec_seq
        kw_ref[b] = jnp.where(keep, pltpu.roll(ck, WINDOW - dec_seq, 1), pltpu.roll(k_new_t, shift, 1))
        vw_ref[b] = jnp.where(keep, pltpu.roll(cv, WINDOW - dec_seq, 1), pltpu.roll(v_new_t, shift, 1))
        return carry

    lax.fori_loop(0, ck_ref.shape[0], one_seq, 0, unroll=4)

    sgo = _spatial_gate(u_ref[...], g_ref[...], sgw_ref, bias_ref[...], dec_seq)
    mix_ref[...] = _normed_mix(a_scr[...], sgo, aon_ref[...], son_ref[...])


def _sample_mix(q, k, v, ck, cv, u, g, sink_col, sg_w, bias, aon, son, dec_seq):
    nseq = ck.shape[0]
    sb = SEQS_PER_STEP
    tm = sb * dec_seq
    assert tm == WINDOW == CHUNK and dec_seq % 8 == 0
    row = lambda i: (i, 0)
    seq3 = lambda i: (i, 0, 0)
    const2 = lambda i: (0, 0)
    const3 = lambda i: (0, 0, 0)
    return pl.pallas_call(
        functools.partial(_sample_mix_body, dec_seq=dec_seq),
        grid=(nseq // sb,),
        in_specs=[
            pl.BlockSpec((tm, ATTN_WIDTH), row),
            pl.BlockSpec((tm, KV_WIDTH), row),
            pl.BlockSpec((tm, KV_WIDTH), row),
            pl.BlockSpec((sb, KV_WIDTH, WINDOW), seq3),
            pl.BlockSpec((sb, KV_WIDTH, WINDOW), seq3),
            pl.BlockSpec((tm, SG_WIDTH), row),
            pl.BlockSpec((tm, SG_WIDTH), row),
            pl.BlockSpec(sink_col.shape, const2),
            pl.BlockSpec(sg_w.shape, const3),
            pl.BlockSpec(bias.shape, const2),
            pl.BlockSpec((1, ATTN_WIDTH), const2),
            pl.BlockSpec((1, SG_WIDTH), const2),
        ],
        out_specs=[
            pl.BlockSpec((tm, ATTN_WIDTH + SG_WIDTH), row),
            pl.BlockSpec((sb, KV_WIDTH, WINDOW), seq3),
            pl.BlockSpec((sb, KV_WIDTH, WINDOW), seq3),
        ],
        out_shape=[
            jax.ShapeDtypeStruct((nseq * dec_seq, ATTN_WIDTH + SG_WIDTH), BF16),
            jax.ShapeDtypeStruct(ck.shape, F32),
            jax.ShapeDtypeStruct(cv.shape, F32),
        ],
        scratch_shapes=[pltpu.VMEM((tm, ATTN_WIDTH), F32)],
        compiler_params=pltpu.CompilerParams(
            dimension_semantics=("arbitrary",), vmem_limit_bytes=VMEM_LIMIT_BYTES),
        name="sample_mix",
    )(q, k, v, ck, cv, u, g, sink_col, sg_w, bias, aon, son)


def _tail_body(x_ref, mix_ref, wo_ref, fn_ref, wg_ref, wu_ref, wd_ref, y_ref, hn_scr):
    @pl.when(pl.program_id(1) == 0)
    def _():
        h = x_ref[...] + jnp.dot(mix_ref[...], wo_ref[...], preferred_element_type=F32)
        hn_scr[...] = _rms(h, fn_ref[...]).astype(BF16)
        y_ref[...] = h

    hn = hn_scr[...]
    gate = jnp.dot(hn, wg_ref[...], preferred_element_type=F32)
    up = jnp.dot(hn, wu_ref[...], preferred_element_type=F32)
    act = (jax.nn.silu(gate) * up).astype(BF16)
    y_ref[...] += jnp.dot(act, wd_ref[...], preferred_element_type=F32)


def _tail(x, mix, wo, ffn_norm, wg, wu, wd):
    rows, d_model = x.shape
    d_ff = wg.shape[1]
    tm = min(ROW_TILE, rows)
    tf = FF_TILE
    return pl.pallas_call(
        _tail_body,
        grid=(rows // tm, d_ff // tf),
        in_specs=[
            pl.BlockSpec((tm, d_model), lambda i, j: (i, 0)),
            pl.BlockSpec((tm, mix.shape[1]), lambda i, j: (i, 0)),
            pl.BlockSpec(wo.shape, lambda i, j: (0, 0), pipeline_mode=pl.Buffered(1)),
            pl.BlockSpec((1, d_model), lambda i, j: (0, 0)),
            pl.BlockSpec((d_model, tf), lambda i, j: (0, j)),
            pl.BlockSpec((d_model, tf), lambda i, j: (0, j)),
            pl.BlockSpec((tf, d_model), lambda i, j: (j, 0)),
        ],
        out_specs=pl.BlockSpec((tm, d_model), lambda i, j: (i, 0)),
        out_shape=jax.ShapeDtypeStruct(x.shape, F32),
        scratch_shapes=[pltpu.VMEM((tm, d_model), BF16)],
        compiler_params=pltpu.CompilerParams(
            dimension_semantics=("arbitrary", "arbitrary"), vmem_limit_bytes=VMEM_LIMIT_BYTES),
        name="tail",
    )(x, mix, wo, ffn_norm, wg, wu, wd)


def _rope_tables(pos):
    half = HEAD_DIM // 2
    inv = ROPE_THETA ** (-jnp.arange(half, dtype=F32) / half)
    ang = pos.astype(F32)[:, None] * inv[None, :]
    cos, sin = jnp.cos(ang), jnp.sin(ang)
    reps = LANES // half
    cos_t = jnp.tile(cos, (1, reps))
    sin_t = jnp.tile(jnp.concatenate([-sin, sin], axis=1), (1, reps // 2))
    return cos_t, sin_t


def _windows_to_native(w):
    return jnp.transpose(w, (0, 2, 3, 1)).reshape(w.shape[0], KV_WIDTH, WINDOW)


def _windows_from_native(w):
    return jnp.transpose(w.reshape(w.shape[0], N_KV_HEADS, HEAD_DIM, WINDOW), (0, 3, 1, 2))[None]


def kernel(x_prompt, x_sample, cache_k_win, cache_v_win, attn_norm, w_in, q_norm, k_norm, sinks,
           sg_norm, sg_w, sg_b, attn_out_norm, sg_out_norm, w_o, ffn_norm, w_gate, w_up, w_down):
    assert w_in.shape[0] == 1, "single-layer step only"
    batch, seq, d_model = x_prompt.shape
    dec_batch, dec_seq, _ = x_sample.shape

    w_in_b = w_in[0].astype(BF16)
    qn2 = jnp.tile(q_norm, (1, LANES // HEAD_DIM))
    kn2 = jnp.tile(k_norm, (1, LANES // HEAD_DIM))
    sink_vec = sinks[0]

    cos_p, sin_p = _rope_tables(jnp.arange(seq, dtype=jnp.int32))
    cos_s, sin_s = _rope_tables(PAST_LEN + jnp.arange(dec_seq, dtype=jnp.int32))
    tm_s = min(ROW_TILE, dec_batch * dec_seq)
    cos_s = jnp.tile(cos_s, (tm_s // dec_seq, 1))
    sin_s = jnp.tile(sin_s, (tm_s // dec_seq, 1))

    bias_p = jnp.repeat(sg_b[0][:, :CHUNK].T, SG_HEAD_DIM, axis=1)
    seqs = CHUNK // dec_seq
    bias_s = jnp.repeat(jnp.tile(sg_b[0][:, :dec_seq].T, (seqs, 1)), SG_HEAD_DIM, axis=1)
    sgw = sg_w[0][:, :CHUNK, :CHUNK]
    sink_col = jnp.repeat(sink_vec, dec_seq)[:, None]

    xp = x_prompt.reshape(batch * seq, d_model)
    q, k, v, u, g, wg, wu, kwin, vwin = _inproj(xp, attn_norm, w_in_b, qn2, kn2, sg_norm, cos_p,
                                                sin_p, BF16, seq, casts=(w_gate[0], w_up[0]))
    mix_p, wd, wo = _prompt_mix(sink_vec, q, k, v, u, g, sgw, bias_p, attn_out_norm, sg_out_norm,
                                w_down[0], w_o[0], batch, seq)
    y_prompt = _tail(xp, mix_p, wo, ffn_norm, wg, wu, wd).reshape(x_prompt.shape)
    k_win_prompt = _windows_from_native(kwin)
    v_win_prompt = _windows_from_native(vwin)

    xs = x_sample.reshape(dec_batch * dec_seq, d_model)
    q, k, v, u, g = _inproj(xs, attn_norm, w_in_b, qn2, kn2, sg_norm, cos_s, sin_s, F32, None)
    mix_s, kw, vw = _sample_mix(q, k, v, _windows_to_native(cache_k_win[0]),
                                _windows_to_native(cache_v_win[0]), u, g, sink_col, sgw, bias_s,
                                attn_out_norm, sg_out_norm, dec_seq)
    y_sample = _tail(xs, mix_s, wo, ffn_norm, wg, wu, wd).reshape(x_sample.shape)
    k_win_sample = _windows_from_native(kw)
    v_win_sample = _windows_from_native(vw)
    sg_v_sample = g.reshape(1, dec_batch, dec_seq, SG_WIDTH)

    return (y_prompt, y_sample, k_win_prompt, v_win_prompt, k_win_sample, v_win_sample, sg_v_sample)
```

```python
import functools

import jax
import jax.numpy as jnp
from jax import lax
from jax.experimental import pallas as pl
from jax.experimental.pallas import tpu as pltpu

HEAD_DIM = 64
N_KV_HEADS = 4
Q_PER_KV = 4
N_Q_HEADS = N_KV_HEADS * Q_PER_KV
KV_WIDTH = N_KV_HEADS * HEAD_DIM
ATTN_WIDTH = N_Q_HEADS * HEAD_DIM
WINDOW = 128
N_SG_HEADS = 8
SG_HEAD_DIM = 128
SG_WIDTH = N_SG_HEADS * SG_HEAD_DIM
CHUNK = 128
PAST_LEN = 16384
ROPE_THETA = 10000.0
EPS = 1e-6

LANES = 128
VMEM_LIMIT_BYTES = 56 * 1024 * 1024

ROW_TILE = 512
INPROJ_SUB_BLOCKS = 4
FF_TILE = 512
SEQS_PER_STEP = 16

F32 = jnp.float32
BF16 = jnp.bfloat16


def _rms(x, gain_row):
    r = lax.rsqrt(jnp.mean(x * x, axis=-1, keepdims=True) + EPS)
    return x * r * gain_row


def _swap_heads(t):
    lane = lax.broadcasted_iota(jnp.int32, (t.shape[0], LANES), 1)
    lo = lane < HEAD_DIM
    outs = []
    for j in range(ATTN_WIDTH // LANES):
        halves = []
        for half in range(2):
            c = 2 * j + half
            a = Q_PER_KV * (c % N_KV_HEADS) + c // N_KV_HEADS
            src = t[:, LANES * (a // 2):LANES * (a // 2 + 1)]
            halves.append(src if a % 2 == half else pltpu.roll(src, HEAD_DIM, 1))
        outs.append(jnp.where(lo, halves[0], halves[1]))
    return jnp.concatenate(outs, axis=1)


def _head_norm_rope(t, gain_row, cos, sin_signed):
    lane = lax.broadcasted_iota(jnp.int32, (t.shape[0], LANES), 1)
    lo_head = lane < HEAD_DIM
    first_half = (lane % HEAD_DIM) < (HEAD_DIM // 2)
    outs = []
    for j in range(t.shape[1] // LANES):
        blk = t[:, LANES * j:LANES * (j + 1)]
        sq = blk * blk
        s_all = jnp.sum(sq, axis=-1, keepdims=True)
        s_lo = jnp.sum(jnp.where(lo_head, sq, 0.0), axis=-1, keepdims=True)
        s_hi = s_all - s_lo
        r = jnp.where(lo_head,
                      lax.rsqrt(s_lo * (1.0 / HEAD_DIM) + EPS),
                      lax.rsqrt(s_hi * (1.0 / HEAD_DIM) + EPS))
        y = blk * r * gain_row
        rot = jnp.where(first_half,
                        pltpu.roll(y, LANES - HEAD_DIM // 2, 1),
                        pltpu.roll(y, HEAD_DIM // 2, 1))
        outs.append(y * cos + rot * sin_signed)
    return jnp.concatenate(outs, axis=1)


def _inproj_body(x_ref, an_ref, w_ref, qn_ref, kn_ref, sgn_ref, cos_ref, sin_ref, *rest,
                 tiles_per_seq, n_cast):
    casts_in, rest = rest[:n_cast], rest[n_cast:]
    q_ref, k_ref, v_ref, u_ref, g_ref = rest[:5]
    casts_out, rest = rest[5:5 + n_cast], rest[5 + n_cast:]
    proj_scr = rest[-1]
    n_sub, sub = proj_scr.shape[0], proj_scr.shape[1]
    c0, c1, c2, c3 = ATTN_WIDTH, ATTN_WIDTH + KV_WIDTH, ATTN_WIDTH + 2 * KV_WIDTH, \
        ATTN_WIDTH + 2 * KV_WIDTH + SG_WIDTH

    def project(r):
        xn = _rms(x_ref[sub * r:sub * (r + 1), :], an_ref[...]).astype(BF16)
        proj_scr[r] = jnp.dot(xn, w_ref[...], preferred_element_type=F32)

    def finish(r):
        rows = slice(sub * r, sub * (r + 1))
        cos = cos_ref[rows, :]
        sin = sin_ref[rows, :]
        q = _head_norm_rope(proj_scr[r, :, 0:c0], qn_ref[...], cos, sin) * (HEAD_DIM ** -0.5)
        q_ref[rows, :] = _swap_heads(q).astype(q_ref.dtype)
        k = _head_norm_rope(proj_scr[r, :, c0:c1], kn_ref[...], cos, sin)
        k_ref[rows, :] = k.astype(k_ref.dtype)
        v = proj_scr[r, :, c1:c2]
        v_ref[rows, :] = v.astype(v_ref.dtype)
        u_ref[rows, :] = jax.nn.gelu(proj_scr[r, :, c2:c3]).astype(u_ref.dtype)
        g = jax.nn.gelu(proj_scr[r, :, c3:])
        g_ref[rows, :] = _rms(g, sgn_ref[...]).astype(g_ref.dtype)
        return k, v

    project(0)
    for src, dst in zip(casts_in, casts_out):
        dst[...] = src[...].astype(BF16)
    for r in range(1, n_sub):
        project(r)
        finish(r - 1)
    k, v = finish(n_sub - 1)

    if len(rest) > 1:
        kwin_ref, vwin_ref = rest[0], rest[1]

        @pl.when(pl.program_id(0) % tiles_per_seq == tiles_per_seq - 1)
        def _():
            kwin_ref[0] = k[sub - WINDOW:].T
            vwin_ref[0] = v[sub - WINDOW:].T


def _inproj(x, attn_norm, w_in, qn2, kn2, sg_norm, cos_t, sin_t, act_dtype, seq, casts=()):
    rows, d_model = x.shape
    tm = min(ROW_TILE, rows)
    steps = rows // tm
    n_tab = cos_t.shape[0] // tm
    const = lambda i: (0, 0)
    row = lambda i: (i, 0)
    tab = lambda i: (i % n_tab, 0)
    slabs = [pl.BlockSpec((w.shape[0] // steps, w.shape[1]), row) for w in casts]
    out_specs = [
        pl.BlockSpec((tm, ATTN_WIDTH), row),
        pl.BlockSpec((tm, KV_WIDTH), row),
        pl.BlockSpec((tm, KV_WIDTH), row),
        pl.BlockSpec((tm, SG_WIDTH), row),
        pl.BlockSpec((tm, SG_WIDTH), row),
    ]
    out_shape = [
        jax.ShapeDtypeStruct((rows, ATTN_WIDTH), act_dtype),
        jax.ShapeDtypeStruct((rows, KV_WIDTH), act_dtype),
        jax.ShapeDtypeStruct((rows, KV_WIDTH), act_dtype),
        jax.ShapeDtypeStruct((rows, SG_WIDTH), BF16),
        jax.ShapeDtypeStruct((rows, SG_WIDTH), act_dtype),
    ]
    out_specs += slabs
    out_shape += [jax.ShapeDtypeStruct(w.shape, BF16) for w in casts]
    tiles_per_seq = 1
    if seq is not None:
        tiles_per_seq = seq // tm
        win = lambda i: (i // tiles_per_seq, 0, 0)
        out_specs += [pl.BlockSpec((1, KV_WIDTH, WINDOW), win)] * 2
        out_shape += [jax.ShapeDtypeStruct((rows // seq, KV_WIDTH, WINDOW), F32)] * 2
    return pl.pallas_call(
        functools.partial(_inproj_body, tiles_per_seq=tiles_per_seq, n_cast=len(casts)),
        grid=(steps,),
        in_specs=[
            pl.BlockSpec((tm, d_model), row),
            pl.BlockSpec((1, d_model), const),
            pl.BlockSpec(w_in.shape, const, pipeline_mode=pl.Buffered(1)),
            pl.BlockSpec((1, LANES), const),
            pl.BlockSpec((1, LANES), const),
            pl.BlockSpec((1, SG_WIDTH), const),
            pl.BlockSpec((tm, LANES), tab),
            pl.BlockSpec((tm, LANES), tab),
        ] + slabs,
        out_specs=out_specs,
        out_shape=out_shape,
        scratch_shapes=[pltpu.VMEM((INPROJ_SUB_BLOCKS, tm // INPROJ_SUB_BLOCKS, w_in.shape[1]), F32)],
        compiler_params=pltpu.CompilerParams(
            dimension_semantics=("arbitrary",), vmem_limit_bytes=VMEM_LIMIT_BYTES),
        name="inproj",
    )(x, attn_norm, w_in, qn2, kn2, sg_norm, cos_t, sin_t, *casts)


def _softmax_sink(s, mask, sink):
    s = jnp.where(mask, s, -jnp.inf)
    m = jnp.maximum(jnp.max(s, axis=-1, keepdims=True), sink)
    p = jnp.exp(s - m)
    denom = jnp.sum(p, axis=-1, keepdims=True) + jnp.exp(sink - m)
    return p, 1.0 / denom


def _spatial_gate(u, g, w_ref, bias, rows_per_seq):
    r = lax.broadcasted_iota(jnp.int32, (CHUNK, CHUNK), 0)
    c = lax.broadcasted_iota(jnp.int32, (CHUNK, CHUNK), 1)
    causal = (r // rows_per_seq == c // rows_per_seq) & (c % rows_per_seq <= r % rows_per_seq)
    gb = g.astype(BF16)
    periodic = ((r < rows_per_seq) & (c % rows_per_seq == r)).astype(BF16)
    outs = []
    for h in range(N_SG_HEADS):
        if rows_per_seq == CHUNK:
            w = w_ref[h]
        else:
            w = jnp.dot(w_ref[h, 0:rows_per_seq, :].astype(BF16), periodic, preferred_element_type=F32)
            w = jnp.concatenate([w] * (CHUNK // rows_per_seq), axis=0)
        w = jnp.where(causal, w, 0.0).astype(BF16)
        cols = slice(SG_HEAD_DIM * h, SG_HEAD_DIM * (h + 1))
        parts = []
        for ch in range(u.shape[0] // CHUNK):
            rws = slice(CHUNK * ch, CHUNK * (ch + 1))
            mixed = jnp.dot(w, gb[rws, cols], preferred_element_type=F32) + bias[:, cols]
            parts.append(u[rws, cols].astype(F32) * mixed)
        outs.append(jnp.concatenate(parts, axis=0))
    return jnp.concatenate(outs, axis=1)


def _normed_mix(a_swapped, sgo, aon, son):
    a = _swap_heads(a_swapped)
    return jnp.concatenate([_rms(a, aon), _rms(sgo, son)], axis=1).astype(BF16)


def _prompt_mix_body(sinks_ref, q_ref, k_ref, v_ref, u_ref, g_ref, sgw_ref, bias_ref,
                     aon_ref, son_ref, wd32, wo32, mix_ref, wd16, wo16, a_scr):
    wd16[...] = wd32[...].astype(BF16)
    wo16[...] = wo32[...].astype(BF16)

    j = pl.program_id(1)
    blocks = q_ref.shape[0] // WINDOW
    quarter = lax.broadcasted_iota(jnp.int32, (WINDOW, KV_WIDTH), 1) // HEAD_DIM
    qi = lax.broadcasted_iota(jnp.int32, (Q_PER_KV * WINDOW, WINDOW), 0) % WINDOW
    kj = lax.broadcasted_iota(jnp.int32, (Q_PER_KV * WINDOW, WINDOW), 1)
    from_prev = kj > qi
    row_g = lax.broadcasted_iota(jnp.int32, (Q_PER_KV * WINDOW, 1), 0) // WINDOW
    sink_cols = []
    for h in range(N_KV_HEADS):
        col = jnp.zeros((Q_PER_KV * WINDOW, 1), F32)
        for g in range(Q_PER_KV):
            col = jnp.where(row_g == g, sinks_ref[Q_PER_KV * h + g], col)
        sink_cols.append(col)

    def block(r, carry):
        n = j * blocks + r
        prev = pl.multiple_of(jnp.maximum(n - 1, 0) * WINDOW, WINDOW)
        cur = pl.multiple_of(n * WINDOW, WINDOW)
        no_prev = jnp.where(n > 0, 0.0, -jnp.inf)
        q = q_ref[pl.ds(pl.multiple_of(r * WINDOW, WINDOW), WINDOW), :]
        kb = jnp.concatenate([k_ref[pl.ds(prev, WINDOW), :], k_ref[pl.ds(cur, WINDOW), :]], axis=0)
        vb = jnp.concatenate([v_ref[pl.ds(prev, WINDOW), :], v_ref[pl.ds(cur, WINDOW), :]], axis=0)
        groups = None
        for h in range(N_KV_HEADS):
            sel = quarter == h
            lhs = jnp.concatenate(
                [jnp.where(sel, q[:, KV_WIDTH * g:KV_WIDTH * (g + 1)], 0) for g in range(Q_PER_KV)],
                axis=0)
            s = lax.dot_general(lhs, kb, (((1,), (1,)), ((), ())), preferred_element_type=F32)
            s = jnp.where(from_prev, s[:, :WINDOW] + no_prev, s[:, WINDOW:])
            sink = sink_cols[h]
            m = jnp.maximum(jnp.max(s, axis=-1, keepdims=True), sink)
            p = jnp.exp(s - m)
            inv = 1.0 / (jnp.sum(p, axis=-1, keepdims=True) + jnp.exp(sink - m))
            p_band = jnp.concatenate(
                [jnp.where(from_prev, p, 0.0), jnp.where(from_prev, 0.0, p)], axis=1).astype(BF16)
            o = jnp.dot(p_band, vb, preferred_element_type=F32) * inv
            parts = [o[WINDOW * g:WINDOW * (g + 1)] for g in range(Q_PER_KV)]
            groups = parts if groups is None else [
                jnp.where(sel, part, acc) for part, acc in zip(parts, groups)]
        a_scr[pl.ds(pl.multiple_of(r * WINDOW, WINDOW), WINDOW), :] = jnp.concatenate(groups, axis=1)
        return carry

    lax.fori_loop(0, blocks, block, 0, unroll=2)

    sgo = _spatial_gate(u_ref[...], g_ref[...], sgw_ref, bias_ref[...], CHUNK)
    mix_ref[...] = _normed_mix(a_scr[...], sgo, aon_ref[...], son_ref[...])


def _prompt_mix(sinks, q, k, v, u, g, sg_w, bias, aon, son, wd, wo, batch, seq):
    tm = ROW_TILE
    tiles = seq // tm
    steps = batch * tiles
    d_model = wo.shape[1]
    row = lambda b, j: (b * tiles + j, 0)
    per_seq = lambda b, j: (b, 0)
    const2 = lambda b, j: (0, 0)
    const3 = lambda b, j: (0, 0, 0)
    slab = lambda w: pl.BlockSpec((w.shape[0] // steps, w.shape[1]), row)
    bf16_like = lambda w: jax.ShapeDtypeStruct(w.shape, BF16)
    return pl.pallas_call(
        _prompt_mix_body,
        grid=(batch, tiles),
        in_specs=[
            pl.BlockSpec(memory_space=pltpu.SMEM),
            pl.BlockSpec((tm, ATTN_WIDTH), row),
            pl.BlockSpec((seq, KV_WIDTH), per_seq),
            pl.BlockSpec((seq, KV_WIDTH), per_seq),
            pl.BlockSpec((tm, SG_WIDTH), row),
            pl.BlockSpec((tm, SG_WIDTH), row),
            pl.BlockSpec(sg_w.shape, const3),
            pl.BlockSpec(bias.shape, const2),
            pl.BlockSpec((1, ATTN_WIDTH), const2),
            pl.BlockSpec((1, SG_WIDTH), const2),
            slab(wd), slab(wo),
        ],
        out_specs=[pl.BlockSpec((tm, d_model), row), slab(wd), slab(wo)],
        out_shape=[jax.ShapeDtypeStruct((batch * seq, d_model), BF16), bf16_like(wd), bf16_like(wo)],
        scratch_shapes=[pltpu.VMEM((tm, ATTN_WIDTH), F32)],
        compiler_params=pltpu.CompilerParams(
            dimension_semantics=("arbitrary", "arbitrary"), vmem_limit_bytes=VMEM_LIMIT_BYTES),
        name="prompt_mix",
    )(sinks, q, k, v, u, g, sg_w, bias, aon, son, wd, wo)


def _sample_mix_body(q_ref, k_ref, v_ref, ck_ref, cv_ref, u_ref, g_ref, sink_ref, sgw_ref,
                     bias_ref, aon_ref, son_ref, mix_ref, kw_ref, vw_ref, a_scr, *, dec_seq):
    rows = N_Q_HEADS * dec_seq
    quarter = lax.broadcasted_iota(jnp.int32, (dec_seq, KV_WIDTH), 1) // HEAD_DIM
    t = lax.broadcasted_iota(jnp.int32, (rows, 2 * WINDOW), 0) % dec_seq
    kj = lax.broadcasted_iota(jnp.int32, (rows, 2 * WINDOW), 1)
    pos_lane = lax.broadcasted_iota(jnp.int32, (KV_WIDTH, WINDOW), 1)
    sink = sink_ref[...]
    k_new_t = k_ref[...].T
    v_new_t = v_ref[...].T

    def one_seq(b, carry):
        r0 = pl.multiple_of(b * dec_seq, dec_seq)
        q = q_ref[pl.ds(r0, dec_seq), :]
        ck = ck_ref[b]
        cv = cv_ref[b]
        own = kj - WINDOW - b * dec_seq
        mask = ((kj < WINDOW) & (kj > t)) | ((own >= 0) & (own <= t))
        lhs = jnp.concatenate(
            [jnp.where(quarter == h, q[:, KV_WIDTH * g:KV_WIDTH * (g + 1)], 0.0)
             for h in range(N_KV_HEADS) for g in range(Q_PER_KV)], axis=0).astype(BF16)
        k_all = jnp.concatenate([ck, k_new_t], axis=1).astype(BF16)
        v_all = jnp.concatenate([cv, v_new_t], axis=1).astype(BF16)
        s = jnp.dot(lhs, k_all, preferred_element_type=F32)
        p, inv = _softmax_sink(s, mask, sink)
        o = lax.dot_general(p.astype(BF16), v_all, (((1,), (1,)), ((), ())),
                            preferred_element_type=F32) * inv
        groups = []
        for g in range(Q_PER_KV):
            piece = lambda h: o[(Q_PER_KV * h + g) * dec_seq:(Q_PER_KV * h + g + 1) * dec_seq]
            acc = piece(N_KV_HEADS - 1)
            for h in range(N_KV_HEADS - 2, -1, -1):
                acc = jnp.where(quarter == h, piece(h), acc)
            groups.append(acc)
        a_scr[pl.ds(r0, dec_seq), :] = jnp.concatenate(groups, axis=1)
        return carry

    lax.fori_loop(0, ck_ref.shape[0], one_seq, 0, unroll=4)

    keep = pos_lane < WINDOW - dec_seq
    for b in range(ck_ref.shape[0]):
        shift = WINDOW - dec_seq - b * dec_seq
        kw_ref[b] = jnp.where(keep, pltpu.roll(ck_ref[b], WINDOW - dec_seq, 1),
                              pltpu.roll(k_new_t, shift, 1))
        vw_ref[b] = jnp.where(keep, pltpu.roll(cv_ref[b], WINDOW - dec_seq, 1),
                              pltpu.roll(v_new_t, shift, 1))

    sgo = _spatial_gate(u_ref[...], g_ref[...], sgw_ref, bias_ref[...], dec_seq)
    mix_ref[...] = _normed_mix(a_scr[...], sgo, aon_ref[...], son_ref[...])


def _sample_mix(q, k, v, ck, cv, u, g, sink_col, sg_w, bias, aon, son, dec_seq):
    nseq = ck.shape[0]
    sb = SEQS_PER_STEP
    tm = sb * dec_seq
    assert tm == WINDOW == CHUNK and dec_seq % 8 == 0
    row = lambda i: (i, 0)
    seq3 = lambda i: (i, 0, 0)
    const2 = lambda i: (0, 0)
    const3 = lambda i: (0, 0, 0)
    return pl.pallas_call(
        functools.partial(_sample_mix_body, dec_seq=dec_seq),
        grid=(nseq // sb,),
        in_specs=[
            pl.BlockSpec((tm, ATTN_WIDTH), row),
            pl.BlockSpec((tm, KV_WIDTH), row),
            pl.BlockSpec((tm, KV_WIDTH), row),
            pl.BlockSpec((sb, KV_WIDTH, WINDOW), seq3),
            pl.BlockSpec((sb, KV_WIDTH, WINDOW), seq3),
            pl.BlockSpec((tm, SG_WIDTH), row),
            pl.BlockSpec((tm, SG_WIDTH), row),
            pl.BlockSpec(sink_col.shape, const2),
            pl.BlockSpec(sg_w.shape, const3),
            pl.BlockSpec(bias.shape, const2),
            pl.BlockSpec((1, ATTN_WIDTH), const2),
            pl.BlockSpec((1, SG_WIDTH), const2),
        ],
        out_specs=[
            pl.BlockSpec((tm, ATTN_WIDTH + SG_WIDTH), row),
            pl.BlockSpec((sb, KV_WIDTH, WINDOW), seq3),
            pl.BlockSpec((sb, KV_WIDTH, WINDOW), seq3),
        ],
        out_shape=[
            jax.ShapeDtypeStruct((nseq * dec_seq, ATTN_WIDTH + SG_WIDTH), BF16),
            jax.ShapeDtypeStruct(ck.shape, F32),
            jax.ShapeDtypeStruct(cv.shape, F32),
        ],
        scratch_shapes=[pltpu.VMEM((tm, ATTN_WIDTH), F32)],
        compiler_params=pltpu.CompilerParams(
            dimension_semantics=("arbitrary",), vmem_limit_bytes=VMEM_LIMIT_BYTES),
        name="sample_mix",
    )(q, k, v, ck, cv, u, g, sink_col, sg_w, bias, aon, son)


def _tail_body(x_ref, mix_ref, wo_ref, fn_ref, wg_ref, wu_ref, wd_ref, y_ref, hn_scr):
    def ffn_chunk():
        hn = hn_scr[...]
        gate = jnp.dot(hn, wg_ref[...], preferred_element_type=F32)
        up = jnp.dot(hn, wu_ref[...], preferred_element_type=F32)
        act = (jax.nn.silu(gate) * up).astype(BF16)
        y_ref[...] += jnp.dot(act, wd_ref[...], preferred_element_type=F32)

    @pl.when(pl.program_id(1) == 0)
    def _():
        half = x_ref.shape[0] // 2
        for rows in (slice(0, half), slice(half, 2 * half)):
            y_ref[rows, :] = x_ref[rows, :] + jnp.dot(
                mix_ref[rows, :], wo_ref[...], preferred_element_type=F32)
        for rows in (slice(0, half), slice(half, 2 * half)):
            hn_scr[rows, :] = _rms(y_ref[rows, :], fn_ref[...]).astype(BF16)
        ffn_chunk()

    @pl.when(pl.program_id(1) > 0)
    def _():
        ffn_chunk()


def _tail(x, mix, wo, ffn_norm, wg, wu, wd):
    rows, d_model = x.shape
    d_ff = wg.shape[1]
    tm = min(ROW_TILE, rows)
    tf = FF_TILE
    return pl.pallas_call(
        _tail_body,
        grid=(rows // tm, d_ff // tf),
        in_specs=[
            pl.BlockSpec((tm, d_model), lambda i, j: (i, 0)),
            pl.BlockSpec((tm, mix.shape[1]), lambda i, j: (i, 0)),
            pl.BlockSpec(wo.shape, lambda i, j: (0, 0), pipeline_mode=pl.Buffered(1)),
            pl.BlockSpec((1, d_model), lambda i, j: (0, 0)),
            pl.BlockSpec((d_model, tf), lambda i, j: (0, j)),
            pl.BlockSpec((d_model, tf), lambda i, j: (0, j)),
            pl.BlockSpec((tf, d_model), lambda i, j: (j, 0)),
        ],
        out_specs=pl.BlockSpec((tm, d_model), lambda i, j: (i, 0)),
        out_shape=jax.ShapeDtypeStruct(x.shape, F32),
        scratch_shapes=[pltpu.VMEM((tm, d_model), BF16)],
        compiler_params=pltpu.CompilerParams(
            dimension_semantics=("arbitrary", "arbitrary"), vmem_limit_bytes=VMEM_LIMIT_BYTES),
        name="tail",
    )(x, mix, wo, ffn_norm, wg, wu, wd)


def _rope_tables(pos):
    half = HEAD_DIM // 2
    inv = ROPE_THETA ** (-jnp.arange(half, dtype=F32) / half)
    ang = pos.astype(F32)[:, None] * inv[None, :]
    cos, sin = jnp.cos(ang), jnp.sin(ang)
    reps = LANES // half
    cos_t = jnp.tile(cos, (1, reps))
    sin_t = jnp.tile(jnp.concatenate([-sin, sin], axis=1), (1, reps // 2))
    return cos_t, sin_t


def _windows_to_native(w):
    return jnp.transpose(w, (0, 2, 3, 1)).reshape(w.shape[0], KV_WIDTH, WINDOW)


def _windows_from_native(w):
    return jnp.transpose(w.reshape(w.shape[0], N_KV_HEADS, HEAD_DIM, WINDOW), (0, 3, 1, 2))[None]


def kernel(x_prompt, x_sample, cache_k_win, cache_v_win, attn_norm, w_in, q_norm, k_norm, sinks,
           sg_norm, sg_w, sg_b, attn_out_norm, sg_out_norm, w_o, ffn_norm, w_gate, w_up, w_down):
    assert w_in.shape[0] == 1, "single-layer step only"
    batch, seq, d_model = x_prompt.shape
    dec_batch, dec_seq, _ = x_sample.shape

    w_in_b = w_in[0].astype(BF16)
    qn2 = jnp.tile(q_norm, (1, LANES // HEAD_DIM))
    kn2 = jnp.tile(k_norm, (1, LANES // HEAD_DIM))
    sink_vec = sinks[0]

    cos_p, sin_p = _rope_tables(jnp.arange(seq, dtype=jnp.int32))
    cos_s, sin_s = _rope_tables(PAST_LEN + jnp.arange(dec_seq, dtype=jnp.int32))
    tm_s = min(ROW_TILE, dec_batch * dec_seq)
    cos_s = jnp.tile(cos_s, (tm_s // dec_seq, 1))
    sin_s = jnp.tile(sin_s, (tm_s // dec_seq, 1))

    bias_p = jnp.repeat(sg_b[0][:, :CHUNK].T, SG_HEAD_DIM, axis=1)
    seqs = CHUNK // dec_seq
    bias_s = jnp.repeat(jnp.tile(sg_b[0][:, :dec_seq].T, (seqs, 1)), SG_HEAD_DIM, axis=1)
    sgw = sg_w[0][:, :CHUNK, :CHUNK]
    sink_col = jnp.repeat(sink_vec, dec_seq)[:, None]

    xp = x_prompt.reshape(batch * seq, d_model)
    q, k, v, u, g, wg, wu, kwin, vwin = _inproj(xp, attn_norm, w_in_b, qn2, kn2, sg_norm, cos_p,
                                                sin_p, BF16, seq, casts=(w_gate[0], w_up[0]))
    mix_p, wd, wo = _prompt_mix(sink_vec, q, k, v, u, g, sgw, bias_p, attn_out_norm, sg_out_norm,
                                w_down[0], w_o[0], batch, seq)
    y_prompt = _tail(xp, mix_p, wo, ffn_norm, wg, wu, wd).reshape(x_prompt.shape)
    k_win_prompt = _windows_from_native(kwin)
    v_win_prompt = _windows_from_native(vwin)

    xs = x_sample.reshape(dec_batch * dec_seq, d_model)
    q, k, v, u, g = _inproj(xs, attn_norm, w_in_b, qn2, kn2, sg_norm, cos_s, sin_s, F32, None)
    mix_s, kw, vw = _sample_mix(q, k, v, _windows_to_native(cache_k_win[0]),
                                _windows_to_native(cache_v_win[0]), u, g, sink_col, sgw, bias_s,
                                attn_out_norm, sg_out_norm, dec_seq)
    y_sample = _tail(xs, mix_s, wo, ffn_norm, wg, wu, wd).reshape(x_sample.shape)
    k_win_sample = _windows_from_native(kw)
    v_win_sample = _windows_from_native(vw)
    sg_v_sample = g.reshape(1, dec_batch, dec_seq, SG_WIDTH)

    return (y_prompt, y_sample, k_win_prompt, v_win_prompt, k_win_sample, v_win_sample, sg_v_sample)
```

```python
import functools

import jax
import jax.numpy as jnp
from jax import lax
from jax.experimental import pallas as pl
from jax.experimental.pallas import tpu as pltpu

HEAD_DIM = 64
N_KV_HEADS = 4
Q_PER_KV = 4
N_Q_HEADS = N_KV_HEADS * Q_PER_KV
KV_WIDTH = N_KV_HEADS * HEAD_DIM
ATTN_WIDTH = N_Q_HEADS * HEAD_DIM
WINDOW = 128
N_SG_HEADS = 8
SG_HEAD_DIM = 128
SG_WIDTH = N_SG_HEADS * SG_HEAD_DIM
CHUNK = 128
PAST_LEN = 16384
ROPE_THETA = 10000.0
EPS = 1e-6

LANES = 128
VMEM_LIMIT_BYTES = 56 * 1024 * 1024

ROW_TILE = 512
INPROJ_SUB_BLOCKS = 4
FF_TILE = 512
SEQS_PER_STEP = 16

F32 = jnp.float32
BF16 = jnp.bfloat16


def _rms(x, gain_row):
    r = lax.rsqrt(jnp.mean(x * x, axis=-1, keepdims=True) + EPS)
    return x * r * gain_row


def _swap_heads(t):
    lane = lax.broadcasted_iota(jnp.int32, (t.shape[0], LANES), 1)
    lo = lane < HEAD_DIM
    outs = []
    for j in range(ATTN_WIDTH // LANES):
        halves = []
        for half in range(2):
            c = 2 * j + half
            a = Q_PER_KV * (c % N_KV_HEADS) + c // N_KV_HEADS
            src = t[:, LANES * (a // 2):LANES * (a // 2 + 1)]
            halves.append(src if a % 2 == half else pltpu.roll(src, HEAD_DIM, 1))
        outs.append(jnp.where(lo, halves[0], halves[1]))
    return jnp.concatenate(outs, axis=1)


def _head_norm_rope(t, gain_row, cos, sin_signed):
    lane = lax.broadcasted_iota(jnp.int32, (t.shape[0], LANES), 1)
    lo_head = lane < HEAD_DIM
    first_half = (lane % HEAD_DIM) < (HEAD_DIM // 2)
    outs = []
    for j in range(t.shape[1] // LANES):
        blk = t[:, LANES * j:LANES * (j + 1)]
        sq = blk * blk
        s_all = jnp.sum(sq, axis=-1, keepdims=True)
        s_lo = jnp.sum(jnp.where(lo_head, sq, 0.0), axis=-1, keepdims=True)
        s_hi = s_all - s_lo
        r = jnp.where(lo_head,
                      lax.rsqrt(s_lo * (1.0 / HEAD_DIM) + EPS),
                      lax.rsqrt(s_hi * (1.0 / HEAD_DIM) + EPS))
        y = blk * r * gain_row
        rot = jnp.where(first_half,
                        pltpu.roll(y, LANES - HEAD_DIM // 2, 1),
                        pltpu.roll(y, HEAD_DIM // 2, 1))
        outs.append(y * cos + rot * sin_signed)
    return jnp.concatenate(outs, axis=1)


def _inproj_body(x_ref, an_ref, w_ref, qn_ref, kn_ref, sgn_ref, cos_ref, sin_ref, *rest,
                 tiles_per_seq, n_cast):
    casts_in, rest = rest[:n_cast], rest[n_cast:]
    q_ref, k_ref, v_ref, u_ref, g_ref = rest[:5]
    casts_out, rest = rest[5:5 + n_cast], rest[5 + n_cast:]
    proj_scr = rest[-1]
    n_sub, sub = proj_scr.shape[0], proj_scr.shape[1]
    q_gain = jnp.concatenate([qn_ref[...]] * (LANES // HEAD_DIM), axis=1)
    k_gain = jnp.concatenate([kn_ref[...]] * (LANES // HEAD_DIM), axis=1)
    c0, c1, c2, c3 = ATTN_WIDTH, ATTN_WIDTH + KV_WIDTH, ATTN_WIDTH + 2 * KV_WIDTH, \
        ATTN_WIDTH + 2 * KV_WIDTH + SG_WIDTH

    def project(r):
        xn = _rms(x_ref[sub * r:sub * (r + 1), :], an_ref[...]).astype(BF16)
        proj_scr[r] = jnp.dot(xn, w_ref[...], preferred_element_type=F32)

    def finish(r):
        rows = slice(sub * r, sub * (r + 1))
        cos = cos_ref[rows, :]
        sin = sin_ref[rows, :]
        q = _head_norm_rope(proj_scr[r, :, 0:c0], q_gain, cos, sin) * (HEAD_DIM ** -0.5)
        q_ref[rows, :] = _swap_heads(q).astype(q_ref.dtype)
        k = _head_norm_rope(proj_scr[r, :, c0:c1], k_gain, cos, sin)
        k_ref[rows, :] = k.astype(k_ref.dtype)
        v = proj_scr[r, :, c1:c2]
        v_ref[rows, :] = v.astype(v_ref.dtype)
        u_ref[rows, :] = jax.nn.gelu(proj_scr[r, :, c2:c3]).astype(u_ref.dtype)
        g = jax.nn.gelu(proj_scr[r, :, c3:])
        g_ref[rows, :] = _rms(g, sgn_ref[...]).astype(g_ref.dtype)
        return k, v

    project(0)
    for src, dst in zip(casts_in, casts_out):
        dst[...] = src[...].astype(BF16)
    for r in range(1, n_sub):
        project(r)
        finish(r - 1)
    k, v = finish(n_sub - 1)

    if len(rest) > 1:
        kwin_ref, vwin_ref = rest[0], rest[1]

        @pl.when(pl.program_id(0) % tiles_per_seq == tiles_per_seq - 1)
        def _():
            kwin_ref[0] = k[sub - WINDOW:].T
            vwin_ref[0] = v[sub - WINDOW:].T


def _inproj(x, attn_norm, w_in, qn2, kn2, sg_norm, cos_t, sin_t, act_dtype, seq, casts=()):
    rows, d_model = x.shape
    tm = min(ROW_TILE, rows)
    steps = rows // tm
    n_tab = cos_t.shape[0] // tm
    const = lambda i: (0, 0)
    row = lambda i: (i, 0)
    tab = lambda i: (i % n_tab, 0)
    slabs = [pl.BlockSpec((w.shape[0] // steps, w.shape[1]), row) for w in casts]
    out_specs = [
        pl.BlockSpec((tm, ATTN_WIDTH), row),
        pl.BlockSpec((tm, KV_WIDTH), row),
        pl.BlockSpec((tm, KV_WIDTH), row),
        pl.BlockSpec((tm, SG_WIDTH), row),
        pl.BlockSpec((tm, SG_WIDTH), row),
    ]
    out_shape = [
        jax.ShapeDtypeStruct((rows, ATTN_WIDTH), act_dtype),
        jax.ShapeDtypeStruct((rows, KV_WIDTH), act_dtype),
        jax.ShapeDtypeStruct((rows, KV_WIDTH), act_dtype),
        jax.ShapeDtypeStruct((rows, SG_WIDTH), BF16),
        jax.ShapeDtypeStruct((rows, SG_WIDTH), act_dtype),
    ]
    out_specs += slabs
    out_shape += [jax.ShapeDtypeStruct(w.shape, BF16) for w in casts]
    tiles_per_seq = 1
    if seq is not None:
        tiles_per_seq = seq // tm
        win = lambda i: (i // tiles_per_seq, 0, 0)
        out_specs += [pl.BlockSpec((1, KV_WIDTH, WINDOW), win)] * 2
        out_shape += [jax.ShapeDtypeStruct((rows // seq, KV_WIDTH, WINDOW), F32)] * 2
    return pl.pallas_call(
        functools.partial(_inproj_body, tiles_per_seq=tiles_per_seq, n_cast=len(casts)),
        grid=(steps,),
        in_specs=[
            pl.BlockSpec((tm, d_model), row),
            pl.BlockSpec((1, d_model), const),
            pl.BlockSpec(w_in.shape, const, pipeline_mode=pl.Buffered(1)),
            pl.BlockSpec((1, HEAD_DIM), const),
            pl.BlockSpec((1, HEAD_DIM), const),
            pl.BlockSpec((1, SG_WIDTH), const),
            pl.BlockSpec((tm, LANES), tab),
            pl.BlockSpec((tm, LANES), tab),
        ] + slabs,
        out_specs=out_specs,
        out_shape=out_shape,
        scratch_shapes=[pltpu.VMEM((INPROJ_SUB_BLOCKS, tm // INPROJ_SUB_BLOCKS, w_in.shape[1]), F32)],
        compiler_params=pltpu.CompilerParams(
            dimension_semantics=("arbitrary",), vmem_limit_bytes=VMEM_LIMIT_BYTES),
        name="inproj",
    )(x, attn_norm, w_in, qn2, kn2, sg_norm, cos_t, sin_t, *casts)


def _softmax_sink(s, mask, sink):
    s = jnp.where(mask, s, -jnp.inf)
    m = jnp.maximum(jnp.max(s, axis=-1, keepdims=True), sink)
    p = jnp.exp(s - m)
    denom = jnp.sum(p, axis=-1, keepdims=True) + jnp.exp(sink - m)
    return p, 1.0 / denom


def _spatial_gate(u, g, w_ref, b_ref, rows_per_seq):
    r = lax.broadcasted_iota(jnp.int32, (CHUNK, CHUNK), 0)
    c = lax.broadcasted_iota(jnp.int32, (CHUNK, CHUNK), 1)
    causal = (r // rows_per_seq == c // rows_per_seq) & (c % rows_per_seq <= r % rows_per_seq)
    gb = g.astype(BF16)
    b_t = b_ref[...].T[0:rows_per_seq, :]
    b_t = jnp.concatenate([b_t] * (CHUNK // rows_per_seq), axis=0)
    periodic = ((r < rows_per_seq) & (c % rows_per_seq == r)).astype(BF16)
    outs = []
    for h in range(N_SG_HEADS):
        if rows_per_seq == CHUNK:
            w = w_ref[h]
        else:
            w = jnp.dot(w_ref[h, 0:rows_per_seq, :].astype(BF16), periodic, preferred_element_type=F32)
            w = jnp.concatenate([w] * (CHUNK // rows_per_seq), axis=0)
        w = jnp.where(causal, w, 0.0).astype(BF16)
        cols = slice(SG_HEAD_DIM * h, SG_HEAD_DIM * (h + 1))
        bias = jnp.broadcast_to(b_t[:, h:h + 1], (CHUNK, SG_HEAD_DIM))
        parts = []
        for ch in range(u.shape[0] // CHUNK):
            rws = slice(CHUNK * ch, CHUNK * (ch + 1))
            mixed = jnp.dot(w, gb[rws, cols], preferred_element_type=F32) + bias
            parts.append(u[rws, cols].astype(F32) * mixed)
        outs.append(jnp.concatenate(parts, axis=0))
    return jnp.concatenate(outs, axis=1)


def _normed_mix(a_swapped, sgo, aon, son):
    a = _swap_heads(a_swapped)
    return jnp.concatenate([_rms(a, aon), _rms(sgo, son)], axis=1).astype(BF16)


def _prompt_mix_body(sinks_ref, q_ref, k_ref, v_ref, u_ref, g_ref, sgw_ref, bias_ref,
                     aon_ref, son_ref, wd32, wo32, mix_ref, wd16, wo16, a_scr):
    wd16[...] = wd32[...].astype(BF16)
    wo16[...] = wo32[...].astype(BF16)

    j = pl.program_id(1)
    blocks = q_ref.shape[0] // WINDOW
    quarter = lax.broadcasted_iota(jnp.int32, (WINDOW, KV_WIDTH), 1) // HEAD_DIM
    qi = lax.broadcasted_iota(jnp.int32, (Q_PER_KV * WINDOW, WINDOW), 0) % WINDOW
    kj = lax.broadcasted_iota(jnp.int32, (Q_PER_KV * WINDOW, WINDOW), 1)
    from_prev = kj > qi
    row_g = lax.broadcasted_iota(jnp.int32, (Q_PER_KV * WINDOW, 1), 0) // WINDOW
    sink_cols = []
    for h in range(N_KV_HEADS):
        col = jnp.zeros((Q_PER_KV * WINDOW, 1), F32)
        for g in range(Q_PER_KV):
            col = jnp.where(row_g == g, sinks_ref[Q_PER_KV * h + g], col)
        sink_cols.append(col)

    def block(r, carry):
        n = j * blocks + r
        prev = pl.multiple_of(jnp.maximum(n - 1, 0) * WINDOW, WINDOW)
        cur = pl.multiple_of(n * WINDOW, WINDOW)
        no_prev = jnp.where(n > 0, 0.0, -jnp.inf)
        q = q_ref[pl.ds(pl.multiple_of(r * WINDOW, WINDOW), WINDOW), :]
        kb = jnp.concatenate([k_ref[pl.ds(prev, WINDOW), :], k_ref[pl.ds(cur, WINDOW), :]], axis=0)
        vb = jnp.concatenate([v_ref[pl.ds(prev, WINDOW), :], v_ref[pl.ds(cur, WINDOW), :]], axis=0)
        groups = None
        for h in range(N_KV_HEADS):
            sel = quarter == h
            lhs = jnp.concatenate(
                [jnp.where(sel, q[:, KV_WIDTH * g:KV_WIDTH * (g + 1)], 0) for g in range(Q_PER_KV)],
                axis=0)
            s = lax.dot_general(lhs, kb, (((1,), (1,)), ((), ())), preferred_element_type=F32)
            s = jnp.where(from_prev, s[:, :WINDOW] + no_prev, s[:, WINDOW:])
            sink = sink_cols[h]
            m = jnp.maximum(jnp.max(s, axis=-1, keepdims=True), sink)
            p = jnp.exp(s - m)
            inv = 1.0 / (jnp.sum(p, axis=-1, keepdims=True) + jnp.exp(sink - m))
            p_band = jnp.concatenate(
                [jnp.where(from_prev, p, 0.0), jnp.where(from_prev, 0.0, p)], axis=1).astype(BF16)
            o = jnp.dot(p_band, vb, preferred_element_type=F32) * inv
            parts = [o[WINDOW * g:WINDOW * (g + 1)] for g in range(Q_PER_KV)]
            groups = parts if groups is None else [
                jnp.where(sel, part, acc) for part, acc in zip(parts, groups)]
        a_scr[pl.ds(pl.multiple_of(r * WINDOW, WINDOW), WINDOW), :] = jnp.concatenate(groups, axis=1)
        return carry

    lax.fori_loop(0, blocks, block, 0, unroll=2)

    sgo = _spatial_gate(u_ref[...], g_ref[...], sgw_ref, bias_ref, CHUNK)
    mix_ref[...] = _normed_mix(a_scr[...], sgo, aon_ref[...], son_ref[...])


def _prompt_mix(sinks, q, k, v, u, g, sg_w, bias, aon, son, wd, wo, batch, seq):
    tm = ROW_TILE
    tiles = seq // tm
    steps = batch * tiles
    d_model = wo.shape[1]
    row = lambda b, j: (b * tiles + j, 0)
    per_seq = lambda b, j: (b, 0)
    const2 = lambda b, j: (0, 0)
    const3 = lambda b, j: (0, 0, 0)
    slab = lambda w: pl.BlockSpec((w.shape[0] // steps, w.shape[1]), row)
    bf16_like = lambda w: jax.ShapeDtypeStruct(w.shape, BF16)
    return pl.pallas_call(
        _prompt_mix_body,
        grid=(batch, tiles),
        in_specs=[
            pl.BlockSpec(memory_space=pltpu.SMEM),
            pl.BlockSpec((tm, ATTN_WIDTH), row),
            pl.BlockSpec((seq, KV_WIDTH), per_seq),
            pl.BlockSpec((seq, KV_WIDTH), per_seq),
            pl.BlockSpec((tm, SG_WIDTH), row),
            pl.BlockSpec((tm, SG_WIDTH), row),
            pl.BlockSpec(sg_w.shape, const3),
            pl.BlockSpec(bias.shape, const2),
            pl.BlockSpec((1, ATTN_WIDTH), const2),
            pl.BlockSpec((1, SG_WIDTH), const2),
            slab(wd), slab(wo),
        ],
        out_specs=[pl.BlockSpec((tm, d_model), row), slab(wd), slab(wo)],
        out_shape=[jax.ShapeDtypeStruct((batch * seq, d_model), BF16), bf16_like(wd), bf16_like(wo)],
        scratch_shapes=[pltpu.VMEM((tm, ATTN_WIDTH), F32)],
        compiler_params=pltpu.CompilerParams(
            dimension_semantics=("arbitrary", "arbitrary"), vmem_limit_bytes=VMEM_LIMIT_BYTES),
        name="prompt_mix",
    )(sinks, q, k, v, u, g, sg_w, bias, aon, son, wd, wo)


def _sample_mix_body(sinks_ref, q_ref, k_ref, v_ref, ck_ref, cv_ref, u_ref, g_ref, sgw_ref,
                     bias_ref, aon_ref, son_ref, mix_ref, kw_ref, vw_ref, a_scr, *, dec_seq):
    rows = N_Q_HEADS * dec_seq
    quarter = lax.broadcasted_iota(jnp.int32, (dec_seq, KV_WIDTH), 1) // HEAD_DIM
    t = lax.broadcasted_iota(jnp.int32, (rows, 2 * WINDOW), 0) % dec_seq
    kj = lax.broadcasted_iota(jnp.int32, (rows, 2 * WINDOW), 1)
    pos_lane = lax.broadcasted_iota(jnp.int32, (KV_WIDTH, WINDOW), 1)
    row_head = lax.broadcasted_iota(jnp.int32, (rows, 1), 0) // dec_seq
    sink = jnp.zeros((rows, 1), F32)
    for a in range(N_Q_HEADS):
        sink = jnp.where(row_head == a, sinks_ref[a], sink)
    k_new_t = k_ref[...].T
    v_new_t = v_ref[...].T

    def one_seq(b, carry):
        r0 = pl.multiple_of(b * dec_seq, dec_seq)
        q = q_ref[pl.ds(r0, dec_seq), :]
        ck = ck_ref[b]
        cv = cv_ref[b]
        own = kj - WINDOW - b * dec_seq
        mask = ((kj < WINDOW) & (kj > t)) | ((own >= 0) & (own <= t))
        lhs = jnp.concatenate(
            [jnp.where(quarter == h, q[:, KV_WIDTH * g:KV_WIDTH * (g + 1)], 0.0)
             for h in range(N_KV_HEADS) for g in range(Q_PER_KV)], axis=0).astype(BF16)
        k_all = jnp.concatenate([ck, k_new_t], axis=1).astype(BF16)
        v_all = jnp.concatenate([cv, v_new_t], axis=1).astype(BF16)
        s = jnp.dot(lhs, k_all, preferred_element_type=F32)
        p, inv = _softmax_sink(s, mask, sink)
        o = lax.dot_general(p.astype(BF16), v_all, (((1,), (1,)), ((), ())),
                            preferred_element_type=F32) * inv
        groups = []
        for g in range(Q_PER_KV):
            piece = lambda h: o[(Q_PER_KV * h + g) * dec_seq:(Q_PER_KV * h + g + 1) * dec_seq]
            acc = piece(N_KV_HEADS - 1)
            for h in range(N_KV_HEADS - 2, -1, -1):
                acc = jnp.where(quarter == h, piece(h), acc)
            groups.append(acc)
        a_scr[pl.ds(r0, dec_seq), :] = jnp.concatenate(groups, axis=1)
        return carry

    lax.fori_loop(0, ck_ref.shape[0], one_seq, 0, unroll=4)

    keep = pos_lane < WINDOW - dec_seq
    for b in range(ck_ref.shape[0]):
        shift = WINDOW - dec_seq - b * dec_seq
        kw_ref[b] = jnp.where(keep, pltpu.roll(ck_ref[b], WINDOW - dec_seq, 1),
                              pltpu.roll(k_new_t, shift, 1))
        vw_ref[b] = jnp.where(keep, pltpu.roll(cv_ref[b], WINDOW - dec_seq, 1),
                              pltpu.roll(v_new_t, shift, 1))

    sgo = _spatial_gate(u_ref[...], g_ref[...], sgw_ref, bias_ref, dec_seq)
    mix_ref[...] = _normed_mix(a_scr[...], sgo, aon_ref[...], son_ref[...])


def _sample_mix(sinks, q, k, v, ck, cv, u, g, sg_w, bias, aon, son, dec_seq):
    nseq = ck.shape[0]
    sb = SEQS_PER_STEP
    tm = sb * dec_seq
    assert tm == WINDOW == CHUNK and dec_seq % 8 == 0
    row = lambda i: (i, 0)
    seq3 = lambda i: (i, 0, 0)
    const2 = lambda i: (0, 0)
    const3 = lambda i: (0, 0, 0)
    return pl.pallas_call(
        functools.partial(_sample_mix_body, dec_seq=dec_seq),
        grid=(nseq // sb,),
        in_specs=[
            pl.BlockSpec(memory_space=pltpu.SMEM),
            pl.BlockSpec((tm, ATTN_WIDTH), row),
            pl.BlockSpec((tm, KV_WIDTH), row),
            pl.BlockSpec((tm, KV_WIDTH), row),
            pl.BlockSpec((sb, KV_WIDTH, WINDOW), seq3),
            pl.BlockSpec((sb, KV_WIDTH, WINDOW), seq3),
            pl.BlockSpec((tm, SG_WIDTH), row),
            pl.BlockSpec((tm, SG_WIDTH), row),
            pl.BlockSpec(sg_w.shape, const3),
            pl.BlockSpec(bias.shape, const2),
            pl.BlockSpec((1, ATTN_WIDTH), const2),
            pl.BlockSpec((1, SG_WIDTH), const2),
        ],
        out_specs=[
            pl.BlockSpec((tm, ATTN_WIDTH + SG_WIDTH), row),
            pl.BlockSpec((sb, KV_WIDTH, WINDOW), seq3),
            pl.BlockSpec((sb, KV_WIDTH, WINDOW), seq3),
        ],
        out_shape=[
            jax.ShapeDtypeStruct((nseq * dec_seq, ATTN_WIDTH + SG_WIDTH), BF16),
            jax.ShapeDtypeStruct(ck.shape, F32),
            jax.ShapeDtypeStruct(cv.shape, F32),
        ],
        scratch_shapes=[pltpu.VMEM((tm, ATTN_WIDTH), F32)],
        compiler_params=pltpu.CompilerParams(
            dimension_semantics=("arbitrary",), vmem_limit_bytes=VMEM_LIMIT_BYTES),
        name="sample_mix",
    )(sinks, q, k, v, ck, cv, u, g, sg_w, bias, aon, son)


def _tail_body(x_ref, mix_ref, wo_ref, fn_ref, wg_ref, wu_ref, wd_ref, y_ref, hn_scr):
    def ffn_chunk():
        hn = hn_scr[...]
        gate = jnp.dot(hn, wg_ref[...], preferred_element_type=F32)
        up = jnp.dot(hn, wu_ref[...], preferred_element_type=F32)
        act = (jax.nn.silu(gate) * up).astype(BF16)
        y_ref[...] += jnp.dot(act, wd_ref[...], preferred_element_type=F32)

    @pl.when(pl.program_id(1) == 0)
    def _():
        half = x_ref.shape[0] // 2
        for rows in (slice(0, half), slice(half, 2 * half)):
            y_ref[rows, :] = x_ref[rows, :] + jnp.dot(
                mix_ref[rows, :], wo_ref[...], preferred_element_type=F32)
        for rows in (slice(0, half), slice(half, 2 * half)):
            hn_scr[rows, :] = _rms(y_ref[rows, :], fn_ref[...]).astype(BF16)
        ffn_chunk()

    @pl.when(pl.program_id(1) > 0)
    def _():
        ffn_chunk()


def _tail(x, mix, wo, ffn_norm, wg, wu, wd):
    rows, d_model = x.shape
    d_ff = wg.shape[1]
    tm = min(ROW_TILE, rows)
    tf = FF_TILE
    return pl.pallas_call(
        _tail_body,
        grid=(rows // tm, d_ff // tf),
        in_specs=[
            pl.BlockSpec((tm, d_model), lambda i, j: (i, 0)),
            pl.BlockSpec((tm, mix.shape[1]), lambda i, j: (i, 0)),
            pl.BlockSpec(wo.shape, lambda i, j: (0, 0), pipeline_mode=pl.Buffered(1)),
            pl.BlockSpec((1, d_model), lambda i, j: (0, 0)),
            pl.BlockSpec((d_model, tf), lambda i, j: (0, j)),
            pl.BlockSpec((d_model, tf), lambda i, j: (0, j)),
            pl.BlockSpec((tf, d_model), lambda i, j: (j, 0)),
        ],
        out_specs=pl.BlockSpec((tm, d_model), lambda i, j: (i, 0)),
        out_shape=jax.ShapeDtypeStruct(x.shape, F32),
        scratch_shapes=[pltpu.VMEM((tm, d_model), BF16)],
        compiler_params=pltpu.CompilerParams(
            dimension_semantics=("arbitrary", "arbitrary"), vmem_limit_bytes=VMEM_LIMIT_BYTES),
        name="tail",
    )(x, mix, wo, ffn_norm, wg, wu, wd)


def _rope_tables(pos):
    half = HEAD_DIM // 2
    lane = jnp.arange(LANES)
    inv = ROPE_THETA ** (-(lane % half).astype(F32) / half)
    ang = pos.astype(F32)[:, None] * inv[None, :]
    sign = jnp.where(lane % HEAD_DIM < half, -1.0, 1.0).astype(F32)
    return jnp.cos(ang), jnp.sin(ang) * sign[None, :]


def _windows_to_native(w):
    return jnp.transpose(w, (0, 2, 3, 1)).reshape(w.shape[0], KV_WIDTH, WINDOW)


def _windows_from_native(w):
    return jnp.transpose(w.reshape(w.shape[0], N_KV_HEADS, HEAD_DIM, WINDOW), (0, 3, 1, 2))[None]


def kernel(x_prompt, x_sample, cache_k_win, cache_v_win, attn_norm, w_in, q_norm, k_norm, sinks,
           sg_norm, sg_w, sg_b, attn_out_norm, sg_out_norm, w_o, ffn_norm, w_gate, w_up, w_down):
    assert w_in.shape[0] == 1, "single-layer step only"
    batch, seq, d_model = x_prompt.shape
    dec_batch, dec_seq, _ = x_sample.shape

    w_in_b = w_in[0].astype(BF16)
    sink_vec = sinks[0]
    sgw = sg_w[0][:, :CHUNK, :CHUNK]
    sgb = sg_b[0][:, :CHUNK]

    cos_p, sin_p = _rope_tables(jnp.arange(seq, dtype=jnp.int32))
    tm_s = min(ROW_TILE, dec_batch * dec_seq)
    cos_s, sin_s = _rope_tables(PAST_LEN + jnp.arange(tm_s, dtype=jnp.int32) % dec_seq)

    xp = x_prompt.reshape(batch * seq, d_model)
    q, k, v, u, g, wg, wu, kwin, vwin = _inproj(xp, attn_norm, w_in_b, q_norm, k_norm, sg_norm,
                                                cos_p, sin_p, BF16, seq,
                                                casts=(w_gate[0], w_up[0]))
    mix_p, wd, wo = _prompt_mix(sink_vec, q, k, v, u, g, sgw, sgb, attn_out_norm, sg_out_norm,
                                w_down[0], w_o[0], batch, seq)
    y_prompt = _tail(xp, mix_p, wo, ffn_norm, wg, wu, wd).reshape(x_prompt.shape)
    k_win_prompt = _windows_from_native(kwin)
    v_win_prompt = _windows_from_native(vwin)

    xs = x_sample.reshape(dec_batch * dec_seq, d_model)
    q, k, v, u, g = _inproj(xs, attn_norm, w_in_b, q_norm, k_norm, sg_norm, cos_s, sin_s, F32, None)
    mix_s, kw, vw = _sample_mix(sink_vec, q, k, v, _windows_to_native(cache_k_win[0]),
                                _windows_to_native(cache_v_win[0]), u, g, sgw, sgb,
                                attn_out_norm, sg_out_norm, dec_seq)
    y_sample = _tail(xs, mix_s, wo, ffn_norm, wg, wu, wd).reshape(x_sample.shape)
    k_win_sample = _windows_from_native(kw)
    v_win_sample = _windows_from_native(vw)
    sg_v_sample = g.reshape(1, dec_batch, dec_seq, SG_WIDTH)

    return (y_prompt, y_sample, k_win_prompt, v_win_prompt, k_win_sample, v_win_sample, sg_v_sample)
```

```python
import functools

import jax
import jax.numpy as jnp
from jax import lax
from jax.experimental import pallas as pl
from jax.experimental.pallas import tpu as pltpu

HEAD_DIM = 64
N_KV_HEADS = 4
Q_PER_KV = 4
N_Q_HEADS = N_KV_HEADS * Q_PER_KV
KV_WIDTH = N_KV_HEADS * HEAD_DIM
ATTN_WIDTH = N_Q_HEADS * HEAD_DIM
WINDOW = 128
N_SG_HEADS = 8
SG_HEAD_DIM = 128
SG_WIDTH = N_SG_HEADS * SG_HEAD_DIM
CHUNK = 128
PAST_LEN = 16384
ROPE_THETA = 10000.0
EPS = 1e-6

LANES = 128
VMEM_LIMIT_BYTES = 56 * 1024 * 1024

ROW_TILE = 512
INPROJ_SUB_BLOCKS = 4
W_IN_CHUNK = 512
FF_TILE = 512
SEQS_PER_STEP = 16

F32 = jnp.float32
BF16 = jnp.bfloat16


def _rms(x, gain_row):
    r = lax.rsqrt(jnp.mean(x * x, axis=-1, keepdims=True) + EPS)
    return x * r * gain_row


def _swap_heads(t):
    lane = lax.broadcasted_iota(jnp.int32, (t.shape[0], LANES), 1)
    lo = lane < HEAD_DIM
    outs = []
    for j in range(ATTN_WIDTH // LANES):
        halves = []
        for half in range(2):
            c = 2 * j + half
            a = Q_PER_KV * (c % N_KV_HEADS) + c // N_KV_HEADS
            src = t[:, LANES * (a // 2):LANES * (a // 2 + 1)]
            halves.append(src if a % 2 == half else pltpu.roll(src, HEAD_DIM, 1))
        outs.append(jnp.where(lo, halves[0], halves[1]))
    return jnp.concatenate(outs, axis=1)


def _head_norm_rope(t, gain_row, cos, sin_signed):
    lane = lax.broadcasted_iota(jnp.int32, (t.shape[0], LANES), 1)
    lo_head = lane < HEAD_DIM
    first_half = (lane % HEAD_DIM) < (HEAD_DIM // 2)
    outs = []
    for j in range(t.shape[1] // LANES):
        blk = t[:, LANES * j:LANES * (j + 1)]
        sq = blk * blk
        s_all = jnp.sum(sq, axis=-1, keepdims=True)
        s_lo = jnp.sum(jnp.where(lo_head, sq, 0.0), axis=-1, keepdims=True)
        s_hi = s_all - s_lo
        r = jnp.where(lo_head,
                      lax.rsqrt(s_lo * (1.0 / HEAD_DIM) + EPS),
                      lax.rsqrt(s_hi * (1.0 / HEAD_DIM) + EPS))
        y = blk * r * gain_row
        rot = jnp.where(first_half,
                        pltpu.roll(y, LANES - HEAD_DIM // 2, 1),
                        pltpu.roll(y, HEAD_DIM // 2, 1))
        outs.append(y * cos + rot * sin_signed)
    return jnp.concatenate(outs, axis=1)


_C0, _C1, _C2, _C3 = (ATTN_WIDTH, ATTN_WIDTH + KV_WIDTH, ATTN_WIDTH + 2 * KV_WIDTH,
                      ATTN_WIDTH + 2 * KV_WIDTH + SG_WIDTH)
IN_WIDTH = _C3 + SG_WIDTH


def _finish_attn(proj, rows, qn_ref, kn_ref, cos_ref, sin_ref, q_ref, k_ref, v_ref):
    q_gain = jnp.concatenate([qn_ref[...]] * (LANES // HEAD_DIM), axis=1)
    k_gain = jnp.concatenate([kn_ref[...]] * (LANES // HEAD_DIM), axis=1)
    cos = cos_ref[rows, :]
    sin = sin_ref[rows, :]
    q = _head_norm_rope(proj(slice(0, _C0)), q_gain, cos, sin) * (HEAD_DIM ** -0.5)
    q_ref[rows, :] = _swap_heads(q).astype(q_ref.dtype)
    k = _head_norm_rope(proj(slice(_C0, _C1)), k_gain, cos, sin)
    k_ref[rows, :] = k.astype(k_ref.dtype)
    v = proj(slice(_C1, _C2))
    v_ref[rows, :] = v.astype(v_ref.dtype)
    return k, v


def _finish_u(proj, rows, u_ref):
    u_ref[rows, :] = jax.nn.gelu(proj(slice(_C2, _C3))).astype(u_ref.dtype)


def _finish_g(proj, rows, sgn_ref, g_ref):
    g = jax.nn.gelu(proj(slice(_C3, IN_WIDTH)))
    g_ref[rows, :] = _rms(g, sgn_ref[...]).astype(g_ref.dtype)


def _inproj_body(x_ref, an_ref, w_ref, qn_ref, kn_ref, sgn_ref, cos_ref, sin_ref, *rest,
                 tiles_per_seq, n_cast):
    casts_in, rest = rest[:n_cast], rest[n_cast:]
    q_ref, k_ref, v_ref, u_ref, g_ref = rest[:5]
    casts_out, rest = rest[5:5 + n_cast], rest[5 + n_cast:]
    proj_scr = rest[-1]
    n_sub, sub = proj_scr.shape[0], proj_scr.shape[1]

    def project(r):
        xn = _rms(x_ref[sub * r:sub * (r + 1), :], an_ref[...]).astype(BF16)
        proj_scr[r] = jnp.dot(xn, w_ref[...], preferred_element_type=F32)

    def finish(r):
        proj = lambda cols: proj_scr[r, :, cols]
        rows = slice(sub * r, sub * (r + 1))
        kv = _finish_attn(proj, rows, qn_ref, kn_ref, cos_ref, sin_ref, q_ref, k_ref, v_ref)
        _finish_u(proj, rows, u_ref)
        _finish_g(proj, rows, sgn_ref, g_ref)
        return kv

    project(0)
    for src, dst in zip(casts_in, casts_out):
        dst[...] = src[...].astype(BF16)
    for r in range(1, n_sub):
        project(r)
        finish(r - 1)
    k, v = finish(n_sub - 1)

    if len(rest) > 1:
        kwin_ref, vwin_ref = rest[0], rest[1]

        @pl.when(pl.program_id(0) % tiles_per_seq == tiles_per_seq - 1)
        def _():
            kwin_ref[0] = k[sub - WINDOW:].T
            vwin_ref[0] = v[sub - WINDOW:].T


def _inproj(x, attn_norm, w_in, qn2, kn2, sg_norm, cos_t, sin_t, act_dtype, seq, casts=()):
    rows, d_model = x.shape
    tm = min(ROW_TILE, rows)
    steps = rows // tm
    n_tab = cos_t.shape[0] // tm
    const = lambda i: (0, 0)
    row = lambda i: (i, 0)
    tab = lambda i: (i % n_tab, 0)
    slabs = [pl.BlockSpec((w.shape[0] // steps, w.shape[1]), row) for w in casts]
    out_specs = [
        pl.BlockSpec((tm, ATTN_WIDTH), row),
        pl.BlockSpec((tm, KV_WIDTH), row),
        pl.BlockSpec((tm, KV_WIDTH), row),
        pl.BlockSpec((tm, SG_WIDTH), row),
        pl.BlockSpec((tm, SG_WIDTH), row),
    ]
    out_shape = [
        jax.ShapeDtypeStruct((rows, ATTN_WIDTH), act_dtype),
        jax.ShapeDtypeStruct((rows, KV_WIDTH), act_dtype),
        jax.ShapeDtypeStruct((rows, KV_WIDTH), act_dtype),
        jax.ShapeDtypeStruct((rows, SG_WIDTH), BF16),
        jax.ShapeDtypeStruct((rows, SG_WIDTH), act_dtype),
    ]
    out_specs += slabs
    out_shape += [jax.ShapeDtypeStruct(w.shape, BF16) for w in casts]
    tiles_per_seq = 1
    if seq is not None:
        tiles_per_seq = seq // tm
        win = lambda i: (i // tiles_per_seq, 0, 0)
        out_specs += [pl.BlockSpec((1, KV_WIDTH, WINDOW), win)] * 2
        out_shape += [jax.ShapeDtypeStruct((rows // seq, KV_WIDTH, WINDOW), F32)] * 2
    return pl.pallas_call(
        functools.partial(_inproj_body, tiles_per_seq=tiles_per_seq, n_cast=len(casts)),
        grid=(steps,),
        in_specs=[
            pl.BlockSpec((tm, d_model), row),
            pl.BlockSpec((1, d_model), const),
            pl.BlockSpec((d_model, IN_WIDTH), const, pipeline_mode=pl.Buffered(1)),
            pl.BlockSpec((1, HEAD_DIM), const),
            pl.BlockSpec((1, HEAD_DIM), const),
            pl.BlockSpec((1, SG_WIDTH), const),
            pl.BlockSpec((tm, LANES), tab),
            pl.BlockSpec((tm, LANES), tab),
        ] + slabs,
        out_specs=out_specs,
        out_shape=out_shape,
        scratch_shapes=[pltpu.VMEM((INPROJ_SUB_BLOCKS, tm // INPROJ_SUB_BLOCKS, IN_WIDTH), F32)],
        compiler_params=pltpu.CompilerParams(
            dimension_semantics=("arbitrary",), vmem_limit_bytes=VMEM_LIMIT_BYTES),
        name="inproj",
    )(x, attn_norm, w_in, qn2, kn2, sg_norm, cos_t, sin_t, *casts)


def _inproj_stream_body(x_ref, an_ref, w32_ref, qn_ref, kn_ref, sgn_ref, cos_ref, sin_ref,
                        q_ref, k_ref, v_ref, u_ref, g_ref, w16_ref, xn_scr, proj_scr):
    j = pl.program_id(1)
    chunk = w32_ref.shape[1]
    assert _C2 % chunk == 0 and _C3 % chunk == 0 and IN_WIDTH % chunk == 0
    sub = proj_scr.shape[0] // INPROJ_SUB_BLOCKS
    row_blocks = [slice(sub * r, sub * (r + 1)) for r in range(INPROJ_SUB_BLOCKS)]

    @pl.when(j == 0)
    def _():
        xn_scr[...] = _rms(x_ref[...], an_ref[...]).astype(BF16)

    def matmul_chunk():
        w16 = w32_ref[...].astype(BF16)
        w16_ref[...] = w16
        cols = pl.ds(pl.multiple_of(j * chunk, chunk), chunk)
        proj_scr[:, cols] = jnp.dot(xn_scr[...], w16, preferred_element_type=F32)

    def finish_attn():
        for rows in row_blocks:
            _finish_attn(lambda c: proj_scr[rows, c], rows, qn_ref, kn_ref, cos_ref, sin_ref,
                         q_ref, k_ref, v_ref)

    def finish_u():
        for rows in row_blocks:
            _finish_u(lambda c: proj_scr[rows, c], rows, u_ref)

    def finish_g():
        for rows in row_blocks:
            _finish_g(lambda c: proj_scr[rows, c], rows, sgn_ref, g_ref)

    after_attn, after_u, last = _C2 // chunk, _C3 // chunk, IN_WIDTH // chunk - 1

    @pl.when(j == after_attn)
    def _():
        matmul_chunk()
        finish_attn()

    @pl.when(j == after_u)
    def _():
        matmul_chunk()
        finish_u()

    @pl.when(j == last)
    def _():
        matmul_chunk()
        finish_g()

    @pl.when((j != after_attn) & (j != after_u) & (j != last))
    def _():
        matmul_chunk()


def _inproj_stream(x, attn_norm, w_in, q_norm, k_norm, sg_norm, cos_t, sin_t, act_dtype):
    rows, d_model = x.shape
    width = w_in.shape[1]
    tm = min(ROW_TILE, rows)
    chunk = W_IN_CHUNK
    n_chunks = width // chunk
    n_tab = cos_t.shape[0] // tm
    const = lambda i, j: (0, 0)
    row = lambda i, j: (i, 0)
    tab = lambda i, j: (i % n_tab, 0)
    col = lambda i, j: (0, j)
    col_once = lambda i, j: (0, jnp.where(i == 0, j, n_chunks))
    return pl.pallas_call(
        _inproj_stream_body,
        grid=(rows // tm, width // chunk),
        in_specs=[
            pl.BlockSpec((tm, d_model), row),
            pl.BlockSpec((1, d_model), const),
            pl.BlockSpec((d_model, chunk), col),
            pl.BlockSpec((1, HEAD_DIM), const),
            pl.BlockSpec((1, HEAD_DIM), const),
            pl.BlockSpec((1, SG_WIDTH), const),
            pl.BlockSpec((tm, LANES), tab),
            pl.BlockSpec((tm, LANES), tab),
        ],
        out_specs=[
            pl.BlockSpec((tm, ATTN_WIDTH), row),
            pl.BlockSpec((tm, KV_WIDTH), row),
            pl.BlockSpec((tm, KV_WIDTH), row),
            pl.BlockSpec((tm, SG_WIDTH), row),
            pl.BlockSpec((tm, SG_WIDTH), row),
            pl.BlockSpec((d_model, chunk), col_once),
        ],
        out_shape=[
            jax.ShapeDtypeStruct((rows, ATTN_WIDTH), act_dtype),
            jax.ShapeDtypeStruct((rows, KV_WIDTH), act_dtype),
            jax.ShapeDtypeStruct((rows, KV_WIDTH), act_dtype),
            jax.ShapeDtypeStruct((rows, SG_WIDTH), BF16),
            jax.ShapeDtypeStruct((rows, SG_WIDTH), act_dtype),
            jax.ShapeDtypeStruct((d_model, width + chunk), BF16),
        ],
        scratch_shapes=[pltpu.VMEM((tm, d_model), BF16), pltpu.VMEM((tm, width), F32)],
        compiler_params=pltpu.CompilerParams(
            dimension_semantics=("arbitrary", "arbitrary"), vmem_limit_bytes=VMEM_LIMIT_BYTES),
        name="inproj_stream",
    )(x, attn_norm, w_in, q_norm, k_norm, sg_norm, cos_t, sin_t)


def _softmax_sink(s, mask, sink):
    s = jnp.where(mask, s, -jnp.inf)
    m = jnp.maximum(jnp.max(s, axis=-1, keepdims=True), sink)
    p = jnp.exp(s - m)
    denom = jnp.sum(p, axis=-1, keepdims=True) + jnp.exp(sink - m)
    return p, 1.0 / denom


def _spatial_gate(u, g, w_ref, b_ref, rows_per_seq):
    r = lax.broadcasted_iota(jnp.int32, (CHUNK, CHUNK), 0)
    c = lax.broadcasted_iota(jnp.int32, (CHUNK, CHUNK), 1)
    causal = (r // rows_per_seq == c // rows_per_seq) & (c % rows_per_seq <= r % rows_per_seq)
    gb = g.astype(BF16)
    b_t = b_ref[...].T[0:rows_per_seq, :]
    b_t = jnp.concatenate([b_t] * (CHUNK // rows_per_seq), axis=0)
    periodic = ((r < rows_per_seq) & (c % rows_per_seq == r)).astype(BF16)
    outs = []
    for h in range(N_SG_HEADS):
        if rows_per_seq == CHUNK:
            w = w_ref[h]
        else:
            w = jnp.dot(w_ref[h, 0:rows_per_seq, :].astype(BF16), periodic, preferred_element_type=F32)
            w = jnp.concatenate([w] * (CHUNK // rows_per_seq), axis=0)
        w = jnp.where(causal, w, 0.0).astype(BF16)
        cols = slice(SG_HEAD_DIM * h, SG_HEAD_DIM * (h + 1))
        bias = jnp.broadcast_to(b_t[:, h:h + 1], (CHUNK, SG_HEAD_DIM))
        parts = []
        for ch in range(u.shape[0] // CHUNK):
            rws = slice(CHUNK * ch, CHUNK * (ch + 1))
            mixed = jnp.dot(w, gb[rws, cols], preferred_element_type=F32) + bias
            parts.append(u[rws, cols].astype(F32) * mixed)
        outs.append(jnp.concatenate(parts, axis=0))
    return jnp.concatenate(outs, axis=1)


def _normed_mix(a_swapped, sgo, aon, son):
    a = _swap_heads(a_swapped)
    return jnp.concatenate([_rms(a, aon), _rms(sgo, son)], axis=1).astype(BF16)


def _prompt_mix_body(sinks_ref, q_ref, k_ref, v_ref, u_ref, g_ref, sgw_ref, bias_ref,
                     aon_ref, son_ref, wd32, wo32, mix_ref, wd16, wo16, a_scr):
    wd16[...] = wd32[...].astype(BF16)
    wo16[...] = wo32[...].astype(BF16)

    j = pl.program_id(1)
    blocks = q_ref.shape[0] // WINDOW
    quarter = lax.broadcasted_iota(jnp.int32, (WINDOW, KV_WIDTH), 1) // HEAD_DIM
    qi = lax.broadcasted_iota(jnp.int32, (Q_PER_KV * WINDOW, WINDOW), 0) % WINDOW
    kj = lax.broadcasted_iota(jnp.int32, (Q_PER_KV * WINDOW, WINDOW), 1)
    from_prev = kj > qi
    row_g = lax.broadcasted_iota(jnp.int32, (Q_PER_KV * WINDOW, 1), 0) // WINDOW
    sink_cols = []
    for h in range(N_KV_HEADS):
        col = jnp.zeros((Q_PER_KV * WINDOW, 1), F32)
        for g in range(Q_PER_KV):
            col = jnp.where(row_g == g, sinks_ref[Q_PER_KV * h + g], col)
        sink_cols.append(col)

    def block(r, carry):
        n = j * blocks + r
        prev = pl.multiple_of(jnp.maximum(n - 1, 0) * WINDOW, WINDOW)
        cur = pl.multiple_of(n * WINDOW, WINDOW)
        no_prev = jnp.where(n > 0, 0.0, -jnp.inf)
        q = q_ref[pl.ds(pl.multiple_of(r * WINDOW, WINDOW), WINDOW), :]
        kb = jnp.concatenate([k_ref[pl.ds(prev, WINDOW), :], k_ref[pl.ds(cur, WINDOW), :]], axis=0)
        vb = jnp.concatenate([v_ref[pl.ds(prev, WINDOW), :], v_ref[pl.ds(cur, WINDOW), :]], axis=0)
        groups = None
        for h in range(N_KV_HEADS):
            sel = quarter == h
            lhs = jnp.concatenate(
                [jnp.where(sel, q[:, KV_WIDTH * g:KV_WIDTH * (g + 1)], 0) for g in range(Q_PER_KV)],
                axis=0)
            s = lax.dot_general(lhs, kb, (((1,), (1,)), ((), ())), preferred_element_type=F32)
            s = jnp.where(from_prev, s[:, :WINDOW] + no_prev, s[:, WINDOW:])
            sink = sink_cols[h]
            m = jnp.maximum(jnp.max(s, axis=-1, keepdims=True), sink)
            p = jnp.exp(s - m)
            inv = 1.0 / (jnp.sum(p, axis=-1, keepdims=True) + jnp.exp(sink - m))
            p_band = jnp.concatenate(
                [jnp.where(from_prev, p, 0.0), jnp.where(from_prev, 0.0, p)], axis=1).astype(BF16)
            o = jnp.dot(p_band, vb, preferred_element_type=F32) * inv
            parts = [o[WINDOW * g:WINDOW * (g + 1)] for g in range(Q_PER_KV)]
            groups = parts if groups is None else [
                jnp.where(sel, part, acc) for part, acc in zip(parts, groups)]
        a_scr[pl.ds(pl.multiple_of(r * WINDOW, WINDOW), WINDOW), :] = jnp.concatenate(groups, axis=1)
        return carry

    lax.fori_loop(0, blocks, block, 0, unroll=2)

    sgo = _spatial_gate(u_ref[...], g_ref[...], sgw_ref, bias_ref, CHUNK)
    mix_ref[...] = _normed_mix(a_scr[...], sgo, aon_ref[...], son_ref[...])


def _prompt_mix(sinks, q, k, v, u, g, sg_w, bias, aon, son, wd, wo, batch, seq):
    tm = ROW_TILE
    tiles = seq // tm
    steps = batch * tiles
    d_model = wo.shape[1]
    row = lambda b, j: (b * tiles + j, 0)
    per_seq = lambda b, j: (b, 0)
    const2 = lambda b, j: (0, 0)
    const3 = lambda b, j: (0, 0, 0)
    slab = lambda w: pl.BlockSpec((w.shape[0] // steps, w.shape[1]), row)
    bf16_like = lambda w: jax.ShapeDtypeStruct(w.shape, BF16)
    return pl.pallas_call(
        _prompt_mix_body,
        grid=(batch, tiles),
        in_specs=[
            pl.BlockSpec(memory_space=pltpu.SMEM),
            pl.BlockSpec((tm, ATTN_WIDTH), row),
            pl.BlockSpec((seq, KV_WIDTH), per_seq),
            pl.BlockSpec((seq, KV_WIDTH), per_seq),
            pl.BlockSpec((tm, SG_WIDTH), row),
            pl.BlockSpec((tm, SG_WIDTH), row),
            pl.BlockSpec(sg_w.shape, const3),
            pl.BlockSpec(bias.shape, const2),
            pl.BlockSpec((1, ATTN_WIDTH), const2),
            pl.BlockSpec((1, SG_WIDTH), const2),
            slab(wd), slab(wo),
        ],
        out_specs=[pl.BlockSpec((tm, d_model), row), slab(wd), slab(wo)],
        out_shape=[jax.ShapeDtypeStruct((batch * seq, d_model), BF16), bf16_like(wd), bf16_like(wo)],
        scratch_shapes=[pltpu.VMEM((tm, ATTN_WIDTH), F32)],
        compiler_params=pltpu.CompilerParams(
            dimension_semantics=("arbitrary", "arbitrary"), vmem_limit_bytes=VMEM_LIMIT_BYTES),
        name="prompt_mix",
    )(sinks, q, k, v, u, g, sg_w, bias, aon, son, wd, wo)


def _sample_mix_body(sinks_ref, q_ref, k_ref, v_ref, ck_ref, cv_ref, u_ref, g_ref, sgw_ref,
                     bias_ref, aon_ref, son_ref, mix_ref, kw_ref, vw_ref, a_scr, *, dec_seq):
    rows = N_Q_HEADS * dec_seq
    quarter = lax.broadcasted_iota(jnp.int32, (dec_seq, KV_WIDTH), 1) // HEAD_DIM
    t = lax.broadcasted_iota(jnp.int32, (rows, 2 * WINDOW), 0) % dec_seq
    kj = lax.broadcasted_iota(jnp.int32, (rows, 2 * WINDOW), 1)
    pos_lane = lax.broadcasted_iota(jnp.int32, (KV_WIDTH, WINDOW), 1)
    row_head = lax.broadcasted_iota(jnp.int32, (rows, 1), 0) // dec_seq
    sink = jnp.zeros((rows, 1), F32)
    for a in range(N_Q_HEADS):
        sink = jnp.where(row_head == a, sinks_ref[a], sink)
    k_new_t = k_ref[...].T
    v_new_t = v_ref[...].T

    def one_seq(b, carry):
        r0 = pl.multiple_of(b * dec_seq, dec_seq)
        q = q_ref[pl.ds(r0, dec_seq), :]
        ck = ck_ref[b]
        cv = cv_ref[b]
        own = kj - WINDOW - b * dec_seq
        mask = ((kj < WINDOW) & (kj > t)) | ((own >= 0) & (own <= t))
        lhs = jnp.concatenate(
            [jnp.where(quarter == h, q[:, KV_WIDTH * g:KV_WIDTH * (g + 1)], 0.0)
             for h in range(N_KV_HEADS) for g in range(Q_PER_KV)], axis=0).astype(BF16)
        k_all = jnp.concatenate([ck, k_new_t], axis=1).astype(BF16)
        v_all = jnp.concatenate([cv, v_new_t], axis=1).astype(BF16)
        s = jnp.dot(lhs, k_all, preferred_element_type=F32)
        p, inv = _softmax_sink(s, mask, sink)
        o = lax.dot_general(p.astype(BF16), v_all, (((1,), (1,)), ((), ())),
                            preferred_element_type=F32) * inv
        groups = []
        for g in range(Q_PER_KV):
            piece = lambda h: o[(Q_PER_KV * h + g) * dec_seq:(Q_PER_KV * h + g + 1) * dec_seq]
            acc = piece(N_KV_HEADS - 1)
            for h in range(N_KV_HEADS - 2, -1, -1):
                acc = jnp.where(quarter == h, piece(h), acc)
            groups.append(acc)
        a_scr[pl.ds(r0, dec_seq), :] = jnp.concatenate(groups, axis=1)
        return carry

    lax.fori_loop(0, ck_ref.shape[0], one_seq, 0, unroll=4)

    keep = pos_lane < WINDOW - dec_seq
    for b in range(ck_ref.shape[0]):
        shift = WINDOW - dec_seq - b * dec_seq
        kw_ref[b] = jnp.where(keep, pltpu.roll(ck_ref[b], WINDOW - dec_seq, 1),
                              pltpu.roll(k_new_t, shift, 1))
        vw_ref[b] = jnp.where(keep, pltpu.roll(cv_ref[b], WINDOW - dec_seq, 1),
                              pltpu.roll(v_new_t, shift, 1))

    sgo = _spatial_gate(u_ref[...], g_ref[...], sgw_ref, bias_ref, dec_seq)
    mix_ref[...] = _normed_mix(a_scr[...], sgo, aon_ref[...], son_ref[...])


def _sample_mix(sinks, q, k, v, ck, cv, u, g, sg_w, bias, aon, son, dec_seq):
    nseq = ck.shape[0]
    sb = SEQS_PER_STEP
    tm = sb * dec_seq
    assert tm == WINDOW == CHUNK and dec_seq % 8 == 0
    row = lambda i: (i, 0)
    seq3 = lambda i: (i, 0, 0)
    const2 = lambda i: (0, 0)
    const3 = lambda i: (0, 0, 0)
    return pl.pallas_call(
        functools.partial(_sample_mix_body, dec_seq=dec_seq),
        grid=(nseq // sb,),
        in_specs=[
            pl.BlockSpec(memory_space=pltpu.SMEM),
            pl.BlockSpec((tm, ATTN_WIDTH), row),
            pl.BlockSpec((tm, KV_WIDTH), row),
            pl.BlockSpec((tm, KV_WIDTH), row),
            pl.BlockSpec((sb, KV_WIDTH, WINDOW), seq3),
            pl.BlockSpec((sb, KV_WIDTH, WINDOW), seq3),
            pl.BlockSpec((tm, SG_WIDTH), row),
            pl.BlockSpec((tm, SG_WIDTH), row),
            pl.BlockSpec(sg_w.shape, const3),
            pl.BlockSpec(bias.shape, const2),
            pl.BlockSpec((1, ATTN_WIDTH), const2),
            pl.BlockSpec((1, SG_WIDTH), const2),
        ],
        out_specs=[
            pl.BlockSpec((tm, ATTN_WIDTH + SG_WIDTH), row),
            pl.BlockSpec((sb, KV_WIDTH, WINDOW), seq3),
            pl.BlockSpec((sb, KV_WIDTH, WINDOW), seq3),
        ],
        out_shape=[
            jax.ShapeDtypeStruct((nseq * dec_seq, ATTN_WIDTH + SG_WIDTH), BF16),
            jax.ShapeDtypeStruct(ck.shape, F32),
            jax.ShapeDtypeStruct(cv.shape, F32),
        ],
        scratch_shapes=[pltpu.VMEM((tm, ATTN_WIDTH), F32)],
        compiler_params=pltpu.CompilerParams(
            dimension_semantics=("arbitrary",), vmem_limit_bytes=VMEM_LIMIT_BYTES),
        name="sample_mix",
    )(sinks, q, k, v, ck, cv, u, g, sg_w, bias, aon, son)


def _tail_body(x_ref, mix_ref, wo_ref, fn_ref, wg_ref, wu_ref, wd_ref, y_ref, hn_scr):
    def ffn_chunk():
        hn = hn_scr[...]
        gate = jnp.dot(hn, wg_ref[...], preferred_element_type=F32)
        up = jnp.dot(hn, wu_ref[...], preferred_element_type=F32)
        act = (jax.nn.silu(gate) * up).astype(BF16)
        y_ref[...] += jnp.dot(act, wd_ref[...], preferred_element_type=F32)

    @pl.when(pl.program_id(1) == 0)
    def _():
        half = x_ref.shape[0] // 2
        for rows in (slice(0, half), slice(half, 2 * half)):
            y_ref[rows, :] = x_ref[rows, :] + jnp.dot(
                mix_ref[rows, :], wo_ref[...], preferred_element_type=F32)
        for rows in (slice(0, half), slice(half, 2 * half)):
            hn_scr[rows, :] = _rms(y_ref[rows, :], fn_ref[...]).astype(BF16)
        ffn_chunk()

    @pl.when(pl.program_id(1) > 0)
    def _():
        ffn_chunk()


def _tail(x, mix, wo, ffn_norm, wg, wu, wd):
    rows, d_model = x.shape
    d_ff = wg.shape[1]
    tm = min(ROW_TILE, rows)
    tf = FF_TILE
    return pl.pallas_call(
        _tail_body,
        grid=(rows // tm, d_ff // tf),
        in_specs=[
            pl.BlockSpec((tm, d_model), lambda i, j: (i, 0)),
            pl.BlockSpec((tm, mix.shape[1]), lambda i, j: (i, 0)),
            pl.BlockSpec(wo.shape, lambda i, j: (0, 0), pipeline_mode=pl.Buffered(1)),
            pl.BlockSpec((1, d_model), lambda i, j: (0, 0)),
            pl.BlockSpec((d_model, tf), lambda i, j: (0, j)),
            pl.BlockSpec((d_model, tf), lambda i, j: (0, j)),
            pl.BlockSpec((tf, d_model), lambda i, j: (j, 0)),
        ],
        out_specs=pl.BlockSpec((tm, d_model), lambda i, j: (i, 0)),
        out_shape=jax.ShapeDtypeStruct(x.shape, F32),
        scratch_shapes=[pltpu.VMEM((tm, d_model), BF16)],
        compiler_params=pltpu.CompilerParams(
            dimension_semantics=("arbitrary", "arbitrary"), vmem_limit_bytes=VMEM_LIMIT_BYTES),
        name="tail",
    )(x, mix, wo, ffn_norm, wg, wu, wd)


def _rope_tables(pos):
    half = HEAD_DIM // 2
    lane = jnp.arange(LANES)
    inv = ROPE_THETA ** (-(lane % half).astype(F32) / half)
    ang = pos.astype(F32)[:, None] * inv[None, :]
    sign = jnp.where(lane % HEAD_DIM < half, -1.0, 1.0).astype(F32)
    return jnp.cos(ang), jnp.sin(ang) * sign[None, :]


def _windows_to_native(w):
    return jnp.transpose(w, (0, 2, 3, 1)).reshape(w.shape[0], KV_WIDTH, WINDOW)


def _windows_from_native(w):
    return jnp.transpose(w.reshape(w.shape[0], N_KV_HEADS, HEAD_DIM, WINDOW), (0, 3, 1, 2))[None]


def kernel(x_prompt, x_sample, cache_k_win, cache_v_win, attn_norm, w_in, q_norm, k_norm, sinks,
           sg_norm, sg_w, sg_b, attn_out_norm, sg_out_norm, w_o, ffn_norm, w_gate, w_up, w_down):
    assert w_in.shape[0] == 1, "single-layer step only"
    batch, seq, d_model = x_prompt.shape
    dec_batch, dec_seq, _ = x_sample.shape

    sink_vec = sinks[0]
    sgw = sg_w[0][:, :CHUNK, :CHUNK]
    sgb = sg_b[0][:, :CHUNK]

    cos_p, sin_p = _rope_tables(jnp.arange(seq, dtype=jnp.int32))
    tm_s = min(ROW_TILE, dec_batch * dec_seq)
    cos_s, sin_s = _rope_tables(PAST_LEN + jnp.arange(tm_s, dtype=jnp.int32) % dec_seq)

    xs = x_sample.reshape(dec_batch * dec_seq, d_model)
    q_s, k_s, v_s, u_s, g_s, w_in_b = _inproj_stream(xs, attn_norm, w_in[0], q_norm, k_norm,
                                                     sg_norm, cos_s, sin_s, F32)

    xp = x_prompt.reshape(batch * seq, d_model)
    q, k, v, u, g, wg, wu, kwin, vwin = _inproj(xp, attn_norm, w_in_b, q_norm, k_norm, sg_norm,
                                                cos_p, sin_p, BF16, seq,
                                                casts=(w_gate[0], w_up[0]))
    mix_p, wd, wo = _prompt_mix(sink_vec, q, k, v, u, g, sgw, sgb, attn_out_norm, sg_out_norm,
                                w_down[0], w_o[0], batch, seq)
    y_prompt = _tail(xp, mix_p, wo, ffn_norm, wg, wu, wd).reshape(x_prompt.shape)
    k_win_prompt = _windows_from_native(kwin)
    v_win_prompt = _windows_from_native(vwin)

    mix_s, kw, vw = _sample_mix(sink_vec, q_s, k_s, v_s, _windows_to_native(cache_k_win[0]),
                                _windows_to_native(cache_v_win[0]), u_s, g_s, sgw, sgb,
                                attn_out_norm, sg_out_norm, dec_seq)
    y_sample = _tail(xs, mix_s, wo, ffn_norm, wg, wu, wd).reshape(x_sample.shape)
    k_win_sample = _windows_from_native(kw)
    v_win_sample = _windows_from_native(vw)
    sg_v_sample = g_s.reshape(1, dec_batch, dec_seq, SG_WIDTH)

    return (y_prompt, y_sample, k_win_prompt, v_win_prompt, k_win_sample, v_win_sample, sg_v_sample)
```

```python
import functools

import jax
import jax.numpy as jnp
from jax import lax
from jax.experimental import pallas as pl
from jax.experimental.pallas import tpu as pltpu

HEAD_DIM = 64
N_KV_HEADS = 4
Q_PER_KV = 4
N_Q_HEADS = N_KV_HEADS * Q_PER_KV
KV_WIDTH = N_KV_HEADS * HEAD_DIM
ATTN_WIDTH = N_Q_HEADS * HEAD_DIM
WINDOW = 128
N_SG_HEADS = 8
SG_HEAD_DIM = 128
SG_WIDTH = N_SG_HEADS * SG_HEAD_DIM
CHUNK = 128
PAST_LEN = 16384
ROPE_THETA = 10000.0
EPS = 1e-6

LANES = 128
VMEM_LIMIT_BYTES = 56 * 1024 * 1024

ROW_TILE = 512
INPROJ_SUB_BLOCKS = 2
FF_TILE = 512
SEQS_PER_STEP = 16

F32 = jnp.float32
BF16 = jnp.bfloat16


def _rms(x, gain_row):
    r = lax.rsqrt(jnp.mean(x * x, axis=-1, keepdims=True) + EPS)
    return x * r * gain_row


def _swap_heads(t):
    lane = lax.broadcasted_iota(jnp.int32, (t.shape[0], LANES), 1)
    lo = lane < HEAD_DIM
    outs = []
    for j in range(ATTN_WIDTH // LANES):
        halves = []
        for half in range(2):
            c = 2 * j + half
            a = Q_PER_KV * (c % N_KV_HEADS) + c // N_KV_HEADS
            src = t[:, LANES * (a // 2):LANES * (a // 2 + 1)]
            halves.append(src if a % 2 == half else pltpu.roll(src, HEAD_DIM, 1))
        outs.append(jnp.where(lo, halves[0], halves[1]))
    return jnp.concatenate(outs, axis=1)


def _head_norm_rope(t, gain_row, cos, sin_signed):
    lane = lax.broadcasted_iota(jnp.int32, (t.shape[0], LANES), 1)
    lo_head = lane < HEAD_DIM
    first_half = (lane % HEAD_DIM) < (HEAD_DIM // 2)
    outs = []
    for j in range(t.shape[1] // LANES):
        blk = t[:, LANES * j:LANES * (j + 1)]
        sq = blk * blk
        s_all = jnp.sum(sq, axis=-1, keepdims=True)
        s_lo = jnp.sum(jnp.where(lo_head, sq, 0.0), axis=-1, keepdims=True)
        s_hi = s_all - s_lo
        r = jnp.where(lo_head,
                      lax.rsqrt(s_lo * (1.0 / HEAD_DIM) + EPS),
                      lax.rsqrt(s_hi * (1.0 / HEAD_DIM) + EPS))
        y = blk * r * gain_row
        rot = jnp.where(first_half,
                        pltpu.roll(y, LANES - HEAD_DIM // 2, 1),
                        pltpu.roll(y, HEAD_DIM // 2, 1))
        outs.append(y * cos + rot * sin_signed)
    return jnp.concatenate(outs, axis=1)


def _inproj_body(x_ref, an_ref, w_ref, qn_ref, kn_ref, sgn_ref, cos_ref, sin_ref, *rest,
                 tiles_per_seq, n_cast):
    casts_in, rest = rest[:n_cast], rest[n_cast:]
    q_ref, k_ref, v_ref, u_ref, g_ref = rest[:5]
    casts_out, rest = rest[5:5 + n_cast], rest[5 + n_cast:]
    proj_scr = rest[-1]
    n_sub, sub = proj_scr.shape[0], proj_scr.shape[1]
    q_gain = jnp.concatenate([qn_ref[...]] * (LANES // HEAD_DIM), axis=1)
    k_gain = jnp.concatenate([kn_ref[...]] * (LANES // HEAD_DIM), axis=1)
    c0, c1, c2, c3 = ATTN_WIDTH, ATTN_WIDTH + KV_WIDTH, ATTN_WIDTH + 2 * KV_WIDTH, \
        ATTN_WIDTH + 2 * KV_WIDTH + SG_WIDTH

    def project(r):
        xn = _rms(x_ref[sub * r:sub * (r + 1), :], an_ref[...]).astype(BF16)
        proj_scr[r] = jnp.dot(xn, w_ref[...], preferred_element_type=F32)

    def finish(r):
        rows = slice(sub * r, sub * (r + 1))
        cos = cos_ref[rows, :]
        sin = sin_ref[rows, :]
        q = _head_norm_rope(proj_scr[r, :, 0:c0], q_gain, cos, sin) * (HEAD_DIM ** -0.5)
        q_ref[rows, :] = _swap_heads(q).astype(q_ref.dtype)
        k = _head_norm_rope(proj_scr[r, :, c0:c1], k_gain, cos, sin)
        k_ref[rows, :] = k.astype(k_ref.dtype)
        v = proj_scr[r, :, c1:c2]
        v_ref[rows, :] = v.astype(v_ref.dtype)
        u_ref[rows, :] = jax.nn.gelu(proj_scr[r, :, c2:c3]).astype(u_ref.dtype)
        g = jax.nn.gelu(proj_scr[r, :, c3:])
        g_ref[rows, :] = _rms(g, sgn_ref[...]).astype(g_ref.dtype)
        return k, v

    project(0)
    for src, dst in zip(casts_in, casts_out):
        dst[...] = src[...].astype(BF16)
    for r in range(1, n_sub):
        project(r)
        finish(r - 1)
    k, v = finish(n_sub - 1)

    if len(rest) > 1:
        kwin_ref, vwin_ref = rest[0], rest[1]

        @pl.when(pl.program_id(0) % tiles_per_seq == tiles_per_seq - 1)
        def _():
            kwin_ref[0] = k[sub - WINDOW:].T
            vwin_ref[0] = v[sub - WINDOW:].T


def _inproj(x, attn_norm, w_in, q_norm, k_norm, sg_norm, cos_t, sin_t, act_dtype, seq, casts=()):
    rows, d_model = x.shape
    tm = min(ROW_TILE, rows)
    steps = rows // tm
    n_tab = cos_t.shape[0] // tm
    const = lambda i: (0, 0)
    row = lambda i: (i, 0)
    tab = lambda i: (i % n_tab, 0)
    slabs = [pl.BlockSpec((w.shape[0] // steps, w.shape[1]), row) for w in casts]
    out_specs = [
        pl.BlockSpec((tm, ATTN_WIDTH), row),
        pl.BlockSpec((tm, KV_WIDTH), row),
        pl.BlockSpec((tm, KV_WIDTH), row),
        pl.BlockSpec((tm, SG_WIDTH), row),
        pl.BlockSpec((tm, SG_WIDTH), row),
    ]
    out_shape = [
        jax.ShapeDtypeStruct((rows, ATTN_WIDTH), act_dtype),
        jax.ShapeDtypeStruct((rows, KV_WIDTH), act_dtype),
        jax.ShapeDtypeStruct((rows, KV_WIDTH), act_dtype),
        jax.ShapeDtypeStruct((rows, SG_WIDTH), BF16),
        jax.ShapeDtypeStruct((rows, SG_WIDTH), act_dtype),
    ]
    out_specs += slabs
    out_shape += [jax.ShapeDtypeStruct(w.shape, BF16) for w in casts]
    tiles_per_seq = 1
    if seq is not None:
        tiles_per_seq = seq // tm
        win = lambda i: (i // tiles_per_seq, 0, 0)
        out_specs += [pl.BlockSpec((1, KV_WIDTH, WINDOW), win)] * 2
        out_shape += [jax.ShapeDtypeStruct((rows // seq, KV_WIDTH, WINDOW), F32)] * 2
    return pl.pallas_call(
        functools.partial(_inproj_body, tiles_per_seq=tiles_per_seq, n_cast=len(casts)),
        grid=(steps,),
        in_specs=[
            pl.BlockSpec((tm, d_model), row),
            pl.BlockSpec((1, d_model), const),
            pl.BlockSpec(w_in.shape, const, pipeline_mode=pl.Buffered(1)),
            pl.BlockSpec((1, HEAD_DIM), const),
            pl.BlockSpec((1, HEAD_DIM), const),
            pl.BlockSpec((1, SG_WIDTH), const),
            pl.BlockSpec((tm, LANES), tab),
            pl.BlockSpec((tm, LANES), tab),
        ] + slabs,
        out_specs=out_specs,
        out_shape=out_shape,
        scratch_shapes=[pltpu.VMEM((INPROJ_SUB_BLOCKS, tm // INPROJ_SUB_BLOCKS, w_in.shape[1]), F32)],
        compiler_params=pltpu.CompilerParams(
            dimension_semantics=("arbitrary",), vmem_limit_bytes=VMEM_LIMIT_BYTES),
        name="inproj",
    )(x, attn_norm, w_in, q_norm, k_norm, sg_norm, cos_t, sin_t, *casts)


def _softmax_sink(s, mask, sink):
    s = jnp.where(mask, s, -jnp.inf)
    m = jnp.maximum(jnp.max(s, axis=-1, keepdims=True), sink)
    p = jnp.exp(s - m)
    denom = jnp.sum(p, axis=-1, keepdims=True) + jnp.exp(sink - m)
    return p, 1.0 / denom


def _spatial_gate(u, g, w_ref, b_ref, rows_per_seq):
    r = lax.broadcasted_iota(jnp.int32, (CHUNK, CHUNK), 0)
    c = lax.broadcasted_iota(jnp.int32, (CHUNK, CHUNK), 1)
    causal = (r // rows_per_seq == c // rows_per_seq) & (c % rows_per_seq <= r % rows_per_seq)
    gb = g.astype(BF16)
    b_t = b_ref[...].T[0:rows_per_seq, :]
    b_t = jnp.concatenate([b_t] * (CHUNK // rows_per_seq), axis=0)
    periodic = ((r < rows_per_seq) & (c % rows_per_seq == r)).astype(BF16)
    outs = []
    for h in range(N_SG_HEADS):
        if rows_per_seq == CHUNK:
            w = w_ref[h]
        else:
            w = jnp.dot(w_ref[h, 0:rows_per_seq, :].astype(BF16), periodic, preferred_element_type=F32)
            w = jnp.concatenate([w] * (CHUNK // rows_per_seq), axis=0)
        w = jnp.where(causal, w, 0.0).astype(BF16)
        cols = slice(SG_HEAD_DIM * h, SG_HEAD_DIM * (h + 1))
        bias = jnp.broadcast_to(b_t[:, h:h + 1], (CHUNK, SG_HEAD_DIM))
        parts = []
        for ch in range(u.shape[0] // CHUNK):
            rws = slice(CHUNK * ch, CHUNK * (ch + 1))
            mixed = jnp.dot(w, gb[rws, cols], preferred_element_type=F32) + bias
            parts.append(u[rws, cols].astype(F32) * mixed)
        outs.append(jnp.concatenate(parts, axis=0))
    return jnp.concatenate(outs, axis=1)


def _normed_mix(a_swapped, sgo, aon, son):
    a = _swap_heads(a_swapped)
    return jnp.concatenate([_rms(a, aon), _rms(sgo, son)], axis=1).astype(BF16)


def _prompt_mix_body(sinks_ref, q_ref, k_ref, v_ref, u_ref, g_ref, sgw_ref, bias_ref,
                     aon_ref, son_ref, wd32, wo32, mix_ref, wd16, wo16, a_scr):
    wd16[...] = wd32[...].astype(BF16)
    wo16[...] = wo32[...].astype(BF16)

    j = pl.program_id(1)
    blocks = q_ref.shape[0] // WINDOW
    quarter = lax.broadcasted_iota(jnp.int32, (WINDOW, KV_WIDTH), 1) // HEAD_DIM
    qi = lax.broadcasted_iota(jnp.int32, (Q_PER_KV * WINDOW, WINDOW), 0) % WINDOW
    kj = lax.broadcasted_iota(jnp.int32, (Q_PER_KV * WINDOW, WINDOW), 1)
    from_prev = kj > qi
    row_g = lax.broadcasted_iota(jnp.int32, (Q_PER_KV * WINDOW, 1), 0) // WINDOW
    sink_cols = []
    for h in range(N_KV_HEADS):
        col = jnp.zeros((Q_PER_KV * WINDOW, 1), F32)
        for g in range(Q_PER_KV):
            col = jnp.where(row_g == g, sinks_ref[Q_PER_KV * h + g], col)
        sink_cols.append(col)

    def block(r, carry):
        n = j * blocks + r
        prev = pl.multiple_of(jnp.maximum(n - 1, 0) * WINDOW, WINDOW)
        cur = pl.multiple_of(n * WINDOW, WINDOW)
        no_prev = jnp.where(n > 0, 0.0, -jnp.inf)
        q = q_ref[pl.ds(pl.multiple_of(r * WINDOW, WINDOW), WINDOW), :]
        kb = jnp.concatenate([k_ref[pl.ds(prev, WINDOW), :], k_ref[pl.ds(cur, WINDOW), :]], axis=0)
        vb = jnp.concatenate([v_ref[pl.ds(prev, WINDOW), :], v_ref[pl.ds(cur, WINDOW), :]], axis=0)
        groups = None
        for h in range(N_KV_HEADS):
            sel = quarter == h
            lhs = jnp.concatenate(
                [jnp.where(sel, q[:, KV_WIDTH * g:KV_WIDTH * (g + 1)], 0) for g in range(Q_PER_KV)],
                axis=0)
            s = lax.dot_general(lhs, kb, (((1,), (1,)), ((), ())), preferred_element_type=F32)
            s = jnp.where(from_prev, s[:, :WINDOW] + no_prev, s[:, WINDOW:])
            sink = sink_cols[h]
            m = jnp.maximum(jnp.max(s, axis=-1, keepdims=True), sink)
            p = jnp.exp(s - m)
            inv = 1.0 / (jnp.sum(p, axis=-1, keepdims=True) + jnp.exp(sink - m))
            p_band = jnp.concatenate(
                [jnp.where(from_prev, p, 0.0), jnp.where(from_prev, 0.0, p)], axis=1).astype(BF16)
            o = jnp.dot(p_band, vb, preferred_element_type=F32) * inv
            parts = [o[WINDOW * g:WINDOW * (g + 1)] for g in range(Q_PER_KV)]
            groups = parts if groups is None else [
                jnp.where(sel, part, acc) for part, acc in zip(parts, groups)]
        a_scr[pl.ds(pl.multiple_of(r * WINDOW, WINDOW), WINDOW), :] = jnp.concatenate(groups, axis=1)
        return carry

    lax.fori_loop(0, blocks, block, 0, unroll=2)

    sgo = _spatial_gate(u_ref[...], g_ref[...], sgw_ref, bias_ref, CHUNK)
    mix_ref[...] = _normed_mix(a_scr[...], sgo, aon_ref[...], son_ref[...])


def _prompt_mix(sinks, q, k, v, u, g, sg_w, bias, aon, son, wd, wo, batch, seq):
    tm = ROW_TILE
    tiles = seq // tm
    steps = batch * tiles
    d_model = wo.shape[1]
    row = lambda b, j: (b * tiles + j, 0)
    per_seq = lambda b, j: (b, 0)
    const2 = lambda b, j: (0, 0)
    const3 = lambda b, j: (0, 0, 0)
    slab = lambda w: pl.BlockSpec((w.shape[0] // steps, w.shape[1]), row)
    bf16_like = lambda w: jax.ShapeDtypeStruct(w.shape, BF16)
    return pl.pallas_call(
        _prompt_mix_body,
        grid=(batch, tiles),
        in_specs=[
            pl.BlockSpec(memory_space=pltpu.SMEM),
            pl.BlockSpec((tm, ATTN_WIDTH), row),
            pl.BlockSpec((seq, KV_WIDTH), per_seq),
            pl.BlockSpec((seq, KV_WIDTH), per_seq),
            pl.BlockSpec((tm, SG_WIDTH), row),
            pl.BlockSpec((tm, SG_WIDTH), row),
            pl.BlockSpec(sg_w.shape, const3),
            pl.BlockSpec(bias.shape, const2),
            pl.BlockSpec((1, ATTN_WIDTH), const2),
            pl.BlockSpec((1, SG_WIDTH), const2),
            slab(wd), slab(wo),
        ],
        out_specs=[pl.BlockSpec((tm, d_model), row), slab(wd), slab(wo)],
        out_shape=[jax.ShapeDtypeStruct((batch * seq, d_model), BF16), bf16_like(wd), bf16_like(wo)],
        scratch_shapes=[pltpu.VMEM((tm, ATTN_WIDTH), F32)],
        compiler_params=pltpu.CompilerParams(
            dimension_semantics=("arbitrary", "arbitrary"), vmem_limit_bytes=VMEM_LIMIT_BYTES),
        name="prompt_mix",
    )(sinks, q, k, v, u, g, sg_w, bias, aon, son, wd, wo)


def _sample_mix_body(sinks_ref, q_ref, k_ref, v_ref, ck_ref, cv_ref, u_ref, g_ref, sgw_ref,
                     bias_ref, aon_ref, son_ref, mix_ref, kw_ref, vw_ref, a_scr, *, dec_seq):
    rows = N_Q_HEADS * dec_seq
    quarter = lax.broadcasted_iota(jnp.int32, (dec_seq, KV_WIDTH), 1) // HEAD_DIM
    t = lax.broadcasted_iota(jnp.int32, (rows, 2 * WINDOW), 0) % dec_seq
    kj = lax.broadcasted_iota(jnp.int32, (rows, 2 * WINDOW), 1)
    pos_lane = lax.broadcasted_iota(jnp.int32, (KV_WIDTH, WINDOW), 1)
    row_head = lax.broadcasted_iota(jnp.int32, (rows, 1), 0) // dec_seq
    sink = jnp.zeros((rows, 1), F32)
    for a in range(N_Q_HEADS):
        sink = jnp.where(row_head == a, sinks_ref[a], sink)
    k_new_t = k_ref[...].T
    v_new_t = v_ref[...].T

    def one_seq(b, carry):
        r0 = pl.multiple_of(b * dec_seq, dec_seq)
        q = q_ref[pl.ds(r0, dec_seq), :]
        ck = ck_ref[b]
        cv = cv_ref[b]
        own = kj - WINDOW - b * dec_seq
        mask = ((kj < WINDOW) & (kj > t)) | ((own >= 0) & (own <= t))
        lhs = jnp.concatenate(
            [jnp.where(quarter == h, q[:, KV_WIDTH * g:KV_WIDTH * (g + 1)], 0.0)
             for h in range(N_KV_HEADS) for g in range(Q_PER_KV)], axis=0).astype(BF16)
        k_all = jnp.concatenate([ck, k_new_t], axis=1).astype(BF16)
        v_all = jnp.concatenate([cv, v_new_t], axis=1).astype(BF16)
        s = jnp.dot(lhs, k_all, preferred_element_type=F32)
        p, inv = _softmax_sink(s, mask, sink)
        o = lax.dot_general(p.astype(BF16), v_all, (((1,), (1,)), ((), ())),
                            preferred_element_type=F32) * inv
        groups = []
        for g in range(Q_PER_KV):
            piece = lambda h: o[(Q_PER_KV * h + g) * dec_seq:(Q_PER_KV * h + g + 1) * dec_seq]
            acc = piece(N_KV_HEADS - 1)
            for h in range(N_KV_HEADS - 2, -1, -1):
                acc = jnp.where(quarter == h, piece(h), acc)
            groups.append(acc)
        a_scr[pl.ds(r0, dec_seq), :] = jnp.concatenate(groups, axis=1)
        return carry

    lax.fori_loop(0, ck_ref.shape[0], one_seq, 0, unroll=4)

    keep = pos_lane < WINDOW - dec_seq
    for b in range(ck_ref.shape[0]):
        shift = WINDOW - dec_seq - b * dec_seq
        kw_ref[b] = jnp.where(keep, pltpu.roll(ck_ref[b], WINDOW - dec_seq, 1),
                              pltpu.roll(k_new_t, shift, 1))
        vw_ref[b] = jnp.where(keep, pltpu.roll(cv_ref[b], WINDOW - dec_seq, 1),
                              pltpu.roll(v_new_t, shift, 1))

    sgo = _spatial_gate(u_ref[...], g_ref[...], sgw_ref, bias_ref, dec_seq)
    mix_ref[...] = _normed_mix(a_scr[...], sgo, aon_ref[...], son_ref[...])


def _sample_mix(sinks, q, k, v, ck, cv, u, g, sg_w, bias, aon, son, dec_seq):
    nseq = ck.shape[0]
    sb = SEQS_PER_STEP
    tm = sb * dec_seq
    assert tm == WINDOW == CHUNK and dec_seq % 8 == 0
    row = lambda i: (i, 0)
    seq3 = lambda i: (i, 0, 0)
    const2 = lambda i: (0, 0)
    const3 = lambda i: (0, 0, 0)
    return pl.pallas_call(
        functools.partial(_sample_mix_body, dec_seq=dec_seq),
        grid=(nseq // sb,),
        in_specs=[
            pl.BlockSpec(memory_space=pltpu.SMEM),
            pl.BlockSpec((tm, ATTN_WIDTH), row),
            pl.BlockSpec((tm, KV_WIDTH), row),
            pl.BlockSpec((tm, KV_WIDTH), row),
            pl.BlockSpec((sb, KV_WIDTH, WINDOW), seq3),
            pl.BlockSpec((sb, KV_WIDTH, WINDOW), seq3),
            pl.BlockSpec((tm, SG_WIDTH), row),
            pl.BlockSpec((tm, SG_WIDTH), row),
            pl.BlockSpec(sg_w.shape, const3),
            pl.BlockSpec(bias.shape, const2),
            pl.BlockSpec((1, ATTN_WIDTH), const2),
            pl.BlockSpec((1, SG_WIDTH), const2),
        ],
        out_specs=[
            pl.BlockSpec((tm, ATTN_WIDTH + SG_WIDTH), row),
            pl.BlockSpec((sb, KV_WIDTH, WINDOW), seq3),
            pl.BlockSpec((sb, KV_WIDTH, WINDOW), seq3),
        ],
        out_shape=[
            jax.ShapeDtypeStruct((nseq * dec_seq, ATTN_WIDTH + SG_WIDTH), BF16),
            jax.ShapeDtypeStruct(ck.shape, F32),
            jax.ShapeDtypeStruct(cv.shape, F32),
        ],
        scratch_shapes=[pltpu.VMEM((tm, ATTN_WIDTH), F32)],
        compiler_params=pltpu.CompilerParams(
            dimension_semantics=("arbitrary",), vmem_limit_bytes=VMEM_LIMIT_BYTES),
        name="sample_mix",
    )(sinks, q, k, v, ck, cv, u, g, sg_w, bias, aon, son)


def _tail_body(x_ref, mix_ref, wo_ref, fn_ref, wg_ref, wu_ref, wd_ref, y_ref, hn_scr):
    def ffn_chunk():
        hn = hn_scr[...]
        gate = jnp.dot(hn, wg_ref[...], preferred_element_type=F32)
        up = jnp.dot(hn, wu_ref[...], preferred_element_type=F32)
        act = (jax.nn.silu(gate) * up).astype(BF16)
        y_ref[...] += jnp.dot(act, wd_ref[...], preferred_element_type=F32)

    @pl.when(pl.program_id(1) == 0)
    def _():
        half = x_ref.shape[0] // 2
        for rows in (slice(0, half), slice(half, 2 * half)):
            y_ref[rows, :] = x_ref[rows, :] + jnp.dot(
                mix_ref[rows, :], wo_ref[...], preferred_element_type=F32)
        for rows in (slice(0, half), slice(half, 2 * half)):
            hn_scr[rows, :] = _rms(y_ref[rows, :], fn_ref[...]).astype(BF16)
        ffn_chunk()

    @pl.when(pl.program_id(1) > 0)
    def _():
        ffn_chunk()


def _tail(x, mix, wo, ffn_norm, wg, wu, wd):
    rows, d_model = x.shape
    d_ff = wg.shape[1]
    tm = min(ROW_TILE, rows)
    tf = FF_TILE
    return pl.pallas_call(
        _tail_body,
        grid=(rows // tm, d_ff // tf),
        in_specs=[
            pl.BlockSpec((tm, d_model), lambda i, j: (i, 0)),
            pl.BlockSpec((tm, mix.shape[1]), lambda i, j: (i, 0)),
            pl.BlockSpec(wo.shape, lambda i, j: (0, 0), pipeline_mode=pl.Buffered(1)),
            pl.BlockSpec((1, d_model), lambda i, j: (0, 0)),
            pl.BlockSpec((d_model, tf), lambda i, j: (0, j)),
            pl.BlockSpec((d_model, tf), lambda i, j: (0, j)),
            pl.BlockSpec((tf, d_model), lambda i, j: (j, 0)),
        ],
        out_specs=pl.BlockSpec((tm, d_model), lambda i, j: (i, 0)),
        out_shape=jax.ShapeDtypeStruct(x.shape, F32),
        scratch_shapes=[pltpu.VMEM((tm, d_model), BF16)],
        compiler_params=pltpu.CompilerParams(
            dimension_semantics=("arbitrary", "arbitrary"), vmem_limit_bytes=VMEM_LIMIT_BYTES),
        name="tail",
    )(x, mix, wo, ffn_norm, wg, wu, wd)


def _rope_tables(pos):
    half = HEAD_DIM // 2
    lane = jnp.arange(LANES)
    inv = ROPE_THETA ** (-(lane % half).astype(F32) / half)
    ang = pos.astype(F32)[:, None] * inv[None, :]
    sign = jnp.where(lane % HEAD_DIM < half, -1.0, 1.0).astype(F32)
    return jnp.cos(ang), jnp.sin(ang) * sign[None, :]


def _windows_to_native(w):
    return jnp.transpose(w, (0, 2, 3, 1)).reshape(w.shape[0], KV_WIDTH, WINDOW)


def _windows_from_native(w):
    return jnp.transpose(w.reshape(w.shape[0], N_KV_HEADS, HEAD_DIM, WINDOW), (0, 3, 1, 2))[None]


def kernel(x_prompt, x_sample, cache_k_win, cache_v_win, attn_norm, w_in, q_norm, k_norm, sinks,
           sg_norm, sg_w, sg_b, attn_out_norm, sg_out_norm, w_o, ffn_norm, w_gate, w_up, w_down):
    assert w_in.shape[0] == 1, "single-layer step only"
    batch, seq, d_model = x_prompt.shape
    dec_batch, dec_seq, _ = x_sample.shape

    w_in_b = w_in[0].astype(BF16)
    sink_vec = sinks[0]
    sgw = sg_w[0][:, :CHUNK, :CHUNK]
    sgb = sg_b[0][:, :CHUNK]

    cos_p, sin_p = _rope_tables(jnp.arange(seq, dtype=jnp.int32))
    tm_s = min(ROW_TILE, dec_batch * dec_seq)
    cos_s, sin_s = _rope_tables(PAST_LEN + jnp.arange(tm_s, dtype=jnp.int32) % dec_seq)

    xp = x_prompt.reshape(batch * seq, d_model)
    q, k, v, u, g, wg, wu, kwin, vwin = _inproj(xp, attn_norm, w_in_b, q_norm, k_norm, sg_norm,
                                                cos_p, sin_p, BF16, seq,
                                                casts=(w_gate[0], w_up[0]))
    mix_p, wd, wo = _prompt_mix(sink_vec, q, k, v, u, g, sgw, sgb, attn_out_norm, sg_out_norm,
                                w_down[0], w_o[0], batch, seq)
    y_prompt = _tail(xp, mix_p, wo, ffn_norm, wg, wu, wd).reshape(x_prompt.shape)
    k_win_prompt = _windows_from_native(kwin)
    v_win_prompt = _windows_from_native(vwin)

    xs = x_sample.reshape(dec_batch * dec_seq, d_model)
    q, k, v, u, g = _inproj(xs, attn_norm, w_in_b, q_norm, k_norm, sg_norm, cos_s, sin_s, F32, None)
    mix_s, kw, vw = _sample_mix(sink_vec, q, k, v, _windows_to_native(cache_k_win[0]),
                                _windows_to_native(cache_v_win[0]), u, g, sgw, sgb,
                                attn_out_norm, sg_out_norm, dec_seq)
    y_sample = _tail(xs, mix_s, wo, ffn_norm, wg, wu, wd).reshape(x_sample.shape)
    k_win_sample = _windows_from_native(kw)
    v_win_sample = _windows_from_native(vw)
    sg_v_sample = g.reshape(1, dec_batch, dec_seq, SG_WIDTH)

    return (y_prompt, y_sample, k_win_prompt, v_win_prompt, k_win_sample, v_win_sample, sg_v_sample)
```

```python
import functools

import jax
import jax.numpy as jnp
from jax import lax
from jax.experimental import pallas as pl
from jax.experimental.pallas import tpu as pltpu

HEAD_DIM = 64
N_KV_HEADS = 4
Q_PER_KV = 4
N_Q_HEADS = N_KV_HEADS * Q_PER_KV
KV_WIDTH = N_KV_HEADS * HEAD_DIM
ATTN_WIDTH = N_Q_HEADS * HEAD_DIM
WINDOW = 128
N_SG_HEADS = 8
SG_HEAD_DIM = 128
SG_WIDTH = N_SG_HEADS * SG_HEAD_DIM
CHUNK = 128
PAST_LEN = 16384
ROPE_THETA = 10000.0
EPS = 1e-6

LANES = 128
VMEM_LIMIT_BYTES = 56 * 1024 * 1024

ROW_TILE = 512
INPROJ_SUB_BLOCKS = 4
FF_TILE = 512
SEQS_PER_STEP = 16

F32 = jnp.float32
BF16 = jnp.bfloat16


def _rms(x, gain_row):
    r = lax.rsqrt(jnp.mean(x * x, axis=-1, keepdims=True) + EPS)
    return x * r * gain_row


def _swapped_head(c):
    return Q_PER_KV * (c % N_KV_HEADS) + c // N_KV_HEADS


def _swap_heads(t):
    lane = lax.broadcasted_iota(jnp.int32, (t.shape[0], LANES), 1)
    lo = lane < HEAD_DIM
    outs = []
    for j in range(ATTN_WIDTH // LANES):
        halves = []
        for half in range(2):
            a = _swapped_head(2 * j + half)
            src = t[:, LANES * (a // 2):LANES * (a // 2 + 1)]
            halves.append(src if a % 2 == half else pltpu.roll(src, HEAD_DIM, 1))
        outs.append(jnp.where(lo, halves[0], halves[1]))
    return jnp.concatenate(outs, axis=1)


def _head_norm_rope(t, gain_row, cos, sin_signed):
    lane = lax.broadcasted_iota(jnp.int32, (t.shape[0], LANES), 1)
    lo_head = lane < HEAD_DIM
    first_half = (lane % HEAD_DIM) < (HEAD_DIM // 2)
    outs = []
    for j in range(t.shape[1] // LANES):
        blk = t[:, LANES * j:LANES * (j + 1)]
        sq = blk * blk
        s_all = jnp.sum(sq, axis=-1, keepdims=True)
        s_lo = jnp.sum(jnp.where(lo_head, sq, 0.0), axis=-1, keepdims=True)
        s_hi = s_all - s_lo
        r = jnp.where(lo_head,
                      lax.rsqrt(s_lo * (1.0 / HEAD_DIM) + EPS),
                      lax.rsqrt(s_hi * (1.0 / HEAD_DIM) + EPS))
        y = blk * r * gain_row
        rot = jnp.where(first_half,
                        pltpu.roll(y, LANES - HEAD_DIM // 2, 1),
                        pltpu.roll(y, HEAD_DIM // 2, 1))
        outs.append(y * cos + rot * sin_signed)
    return jnp.concatenate(outs, axis=1)


def _inproj_body(x_ref, an_ref, w_ref, qn_ref, kn_ref, sgn_ref, cos_ref, sin_ref, *rest,
                 tiles_per_seq, n_cast):
    casts_in, rest = rest[:n_cast], rest[n_cast:]
    q_ref, k_ref, v_ref, u_ref, g_ref = rest[:5]
    casts_out, rest = rest[5:5 + n_cast], rest[5 + n_cast:]
    proj_scr = rest[-1]
    n_sub, sub = proj_scr.shape[0], proj_scr.shape[1]
    q_gain = jnp.concatenate([qn_ref[...]] * (LANES // HEAD_DIM), axis=1)
    k_gain = jnp.concatenate([kn_ref[...]] * (LANES // HEAD_DIM), axis=1)
    c0, c1, c2, c3 = ATTN_WIDTH, ATTN_WIDTH + KV_WIDTH, ATTN_WIDTH + 2 * KV_WIDTH, \
        ATTN_WIDTH + 2 * KV_WIDTH + SG_WIDTH

    def project(r):
        xn = _rms(x_ref[sub * r:sub * (r + 1), :], an_ref[...]).astype(BF16)
        proj_scr[r] = jnp.dot(xn, w_ref[...], preferred_element_type=F32)

    def finish(r):
        rows = slice(sub * r, sub * (r + 1))
        cos = cos_ref[rows, :]
        sin = sin_ref[rows, :]
        q = _head_norm_rope(proj_scr[r, :, 0:c0], q_gain, cos, sin) * (HEAD_DIM ** -0.5)
        q_ref[rows, :] = _swap_heads(q).astype(q_ref.dtype)
        k = _head_norm_rope(proj_scr[r, :, c0:c1], k_gain, cos, sin)
        k_ref[rows, :] = k.astype(k_ref.dtype)
        v = proj_scr[r, :, c1:c2]
        v_ref[rows, :] = v.astype(v_ref.dtype)
        u_ref[rows, :] = jax.nn.gelu(proj_scr[r, :, c2:c3]).astype(u_ref.dtype)
        g = jax.nn.gelu(proj_scr[r, :, c3:])
        g_ref[rows, :] = _rms(g, sgn_ref[...]).astype(g_ref.dtype)
        return k, v

    project(0)
    for src, dst in zip(casts_in, casts_out):
        dst[...] = src[...].astype(BF16)
    for r in range(1, n_sub):
        project(r)
        finish(r - 1)
    k, v = finish(n_sub - 1)

    if len(rest) > 1:
        kwin_ref, vwin_ref = rest[0], rest[1]

        @pl.when(pl.program_id(0) % tiles_per_seq == tiles_per_seq - 1)
        def _():
            kwin_ref[0] = k[sub - WINDOW:].T
            vwin_ref[0] = v[sub - WINDOW:].T


def _inproj(x, attn_norm, w_in, q_norm, k_norm, sg_norm, cos_t, sin_t, act_dtype, seq, casts=()):
    rows, d_model = x.shape
    tm = min(ROW_TILE, rows)
    steps = rows // tm
    n_tab = cos_t.shape[0] // tm
    const = lambda i: (0, 0)
    row = lambda i: (i, 0)
    tab = lambda i: (i % n_tab, 0)
    slabs = [pl.BlockSpec((w.shape[0] // steps, w.shape[1]), row) for w in casts]
    out_specs = [
        pl.BlockSpec((tm, ATTN_WIDTH), row),
        pl.BlockSpec((tm, KV_WIDTH), row),
        pl.BlockSpec((tm, KV_WIDTH), row),
        pl.BlockSpec((tm, SG_WIDTH), row),
        pl.BlockSpec((tm, SG_WIDTH), row),
    ]
    out_shape = [
        jax.ShapeDtypeStruct((rows, ATTN_WIDTH), act_dtype),
        jax.ShapeDtypeStruct((rows, KV_WIDTH), act_dtype),
        jax.ShapeDtypeStruct((rows, KV_WIDTH), act_dtype),
        jax.ShapeDtypeStruct((rows, SG_WIDTH), BF16),
        jax.ShapeDtypeStruct((rows, SG_WIDTH), act_dtype),
    ]
    out_specs += slabs
    out_shape += [jax.ShapeDtypeStruct(w.shape, BF16) for w in casts]
    tiles_per_seq = 1
    if seq is not None:
        tiles_per_seq = seq // tm
        win = lambda i: (i // tiles_per_seq, 0, 0)
        out_specs += [pl.BlockSpec((1, KV_WIDTH, WINDOW), win)] * 2
        out_shape += [jax.ShapeDtypeStruct((rows // seq, KV_WIDTH, WINDOW), F32)] * 2
    return pl.pallas_call(
        functools.partial(_inproj_body, tiles_per_seq=tiles_per_seq, n_cast=len(casts)),
        grid=(steps,),
        in_specs=[
            pl.BlockSpec((tm, d_model), row),
            pl.BlockSpec((1, d_model), const),
            pl.BlockSpec(w_in.shape, const, pipeline_mode=pl.Buffered(1)),
            pl.BlockSpec((1, HEAD_DIM), const),
            pl.BlockSpec((1, HEAD_DIM), const),
            pl.BlockSpec((1, SG_WIDTH), const),
            pl.BlockSpec((tm, LANES), tab),
            pl.BlockSpec((tm, LANES), tab),
        ] + slabs,
        out_specs=out_specs,
        out_shape=out_shape,
        scratch_shapes=[pltpu.VMEM((INPROJ_SUB_BLOCKS, tm // INPROJ_SUB_BLOCKS, w_in.shape[1]), F32)],
        compiler_params=pltpu.CompilerParams(
            dimension_semantics=("arbitrary",), vmem_limit_bytes=VMEM_LIMIT_BYTES),
        name="inproj",
    )(x, attn_norm, w_in, q_norm, k_norm, sg_norm, cos_t, sin_t, *casts)


def _softmax_sink(s, mask, sink):
    s = jnp.where(mask, s, -jnp.inf)
    m = jnp.maximum(jnp.max(s, axis=-1, keepdims=True), sink)
    p = jnp.exp(s - m)
    denom = jnp.sum(p, axis=-1, keepdims=True) + jnp.exp(sink - m)
    return p, 1.0 / denom


def _spatial_gate(u, g, w_ref, b_ref, rows_per_seq):
    r = lax.broadcasted_iota(jnp.int32, (CHUNK, CHUNK), 0)
    c = lax.broadcasted_iota(jnp.int32, (CHUNK, CHUNK), 1)
    causal = (r // rows_per_seq == c // rows_per_seq) & (c % rows_per_seq <= r % rows_per_seq)
    gb = g.astype(BF16)
    b_t = b_ref[...].T[0:rows_per_seq, :]
    b_t = jnp.concatenate([b_t] * (CHUNK // rows_per_seq), axis=0)
    periodic = ((r < rows_per_seq) & (c % rows_per_seq == r)).astype(BF16)
    outs = []
    for h in range(N_SG_HEADS):
        if rows_per_seq == CHUNK:
            w = w_ref[h]
        else:
            w = jnp.dot(w_ref[h, 0:rows_per_seq, :].astype(BF16), periodic, preferred_element_type=F32)
            w = jnp.concatenate([w] * (CHUNK // rows_per_seq), axis=0)
        w = jnp.where(causal, w, 0.0).astype(BF16)
        cols = slice(SG_HEAD_DIM * h, SG_HEAD_DIM * (h + 1))
        bias = jnp.broadcast_to(b_t[:, h:h + 1], (CHUNK, SG_HEAD_DIM))
        parts = []
        for ch in range(u.shape[0] // CHUNK):
            rws = slice(CHUNK * ch, CHUNK * (ch + 1))
            mixed = jnp.dot(w, gb[rws, cols], preferred_element_type=F32) + bias
            parts.append(u[rws, cols].astype(F32) * mixed)
        outs.append(jnp.concatenate(parts, axis=0))
    return jnp.concatenate(outs, axis=1)


def _normed_mix(a_swapped, sgo, aon, son):
    return jnp.concatenate([_rms(a_swapped, _swap_heads(aon)), _rms(sgo, son)], axis=1).astype(BF16)


def _prompt_mix_body(sinks_ref, q_ref, k_ref, v_ref, u_ref, g_ref, sgw_ref, bias_ref,
                     aon_ref, son_ref, wd32, wo32_lo, wo32_hi, mix_ref, wd16, wo16, a_scr):
    wd16[...] = wd32[...].astype(BF16)
    wo16[0:HEAD_DIM, :] = wo32_lo[...].astype(BF16)
    wo16[HEAD_DIM:2 * HEAD_DIM, :] = wo32_hi[...].astype(BF16)

    j = pl.program_id(1)
    blocks = q_ref.shape[0] // WINDOW
    quarter = lax.broadcasted_iota(jnp.int32, (WINDOW, KV_WIDTH), 1) // HEAD_DIM
    qi = lax.broadcasted_iota(jnp.int32, (Q_PER_KV * WINDOW, WINDOW), 0) % WINDOW
    kj = lax.broadcasted_iota(jnp.int32, (Q_PER_KV * WINDOW, WINDOW), 1)
    from_prev = kj > qi
    row_g = lax.broadcasted_iota(jnp.int32, (Q_PER_KV * WINDOW, 1), 0) // WINDOW
    sink_cols = []
    for h in range(N_KV_HEADS):
        col = jnp.zeros((Q_PER_KV * WINDOW, 1), F32)
        for g in range(Q_PER_KV):
            col = jnp.where(row_g == g, sinks_ref[Q_PER_KV * h + g], col)
        sink_cols.append(col)

    def block(r, carry):
        n = j * blocks + r
        prev = pl.multiple_of(jnp.maximum(n - 1, 0) * WINDOW, WINDOW)
        cur = pl.multiple_of(n * WINDOW, WINDOW)
        no_prev = jnp.where(n > 0, 0.0, -jnp.inf)
        q = q_ref[pl.ds(pl.multiple_of(r * WINDOW, WINDOW), WINDOW), :]
        kb = jnp.concatenate([k_ref[pl.ds(prev, WINDOW), :], k_ref[pl.ds(cur, WINDOW), :]], axis=0)
        vb = jnp.concatenate([v_ref[pl.ds(prev, WINDOW), :], v_ref[pl.ds(cur, WINDOW), :]], axis=0)
        groups = None
        for h in range(N_KV_HEADS):
            sel = quarter == h
            lhs = jnp.concatenate(
                [jnp.where(sel, q[:, KV_WIDTH * g:KV_WIDTH * (g + 1)], 0) for g in range(Q_PER_KV)],
                axis=0)
            s = lax.dot_general(lhs, kb, (((1,), (1,)), ((), ())), preferred_element_type=F32)
            s = jnp.where(from_prev, s[:, :WINDOW] + no_prev, s[:, WINDOW:])
            sink = sink_cols[h]
            m = jnp.maximum(jnp.max(s, axis=-1, keepdims=True), sink)
            p = jnp.exp(s - m)
            inv = 1.0 / (jnp.sum(p, axis=-1, keepdims=True) + jnp.exp(sink - m))
            p_band = jnp.concatenate(
                [jnp.where(from_prev, p, 0.0), jnp.where(from_prev, 0.0, p)], axis=1).astype(BF16)
            o = jnp.dot(p_band, vb, preferred_element_type=F32) * inv
            parts = [o[WINDOW * g:WINDOW * (g + 1)] for g in range(Q_PER_KV)]
            groups = parts if groups is None else [
                jnp.where(sel, part, acc) for part, acc in zip(parts, groups)]
        a_scr[pl.ds(pl.multiple_of(r * WINDOW, WINDOW), WINDOW), :] = jnp.concatenate(groups, axis=1)
        return carry

    lax.fori_loop(0, blocks, block, 0, unroll=2)

    sgo = _spatial_gate(u_ref[...], g_ref[...], sgw_ref, bias_ref, CHUNK)
    mix_ref[...] = _normed_mix(a_scr[...], sgo, aon_ref[...], son_ref[...])


def _prompt_mix(sinks, q, k, v, u, g, sg_w, bias, aon, son, wd, wo, batch, seq):
    tm = ROW_TILE
    tiles = seq // tm
    steps = batch * tiles
    d_model = wo.shape[1]
    assert wo.shape[0] == 2 * HEAD_DIM * steps
    row = lambda b, j: (b * tiles + j, 0)
    per_seq = lambda b, j: (b, 0)
    const2 = lambda b, j: (0, 0)
    const3 = lambda b, j: (0, 0, 0)
    slab = lambda w: pl.BlockSpec((w.shape[0] // steps, w.shape[1]), row)
    bf16_like = lambda w: jax.ShapeDtypeStruct(w.shape, BF16)

    def wo_block(half):
        def index(b, j):
            c = 2 * (b * tiles + j) + half
            return jnp.where(c < N_Q_HEADS, _swapped_head(c), c), 0
        return pl.BlockSpec((HEAD_DIM, d_model), index)

    return pl.pallas_call(
        _prompt_mix_body,
        grid=(batch, tiles),
        in_specs=[
            pl.BlockSpec(memory_space=pltpu.SMEM),
            pl.BlockSpec((tm, ATTN_WIDTH), row),
            pl.BlockSpec((seq, KV_WIDTH), per_seq),
            pl.BlockSpec((seq, KV_WIDTH), per_seq),
            pl.BlockSpec((tm, SG_WIDTH), row),
            pl.BlockSpec((tm, SG_WIDTH), row),
            pl.BlockSpec(sg_w.shape, const3),
            pl.BlockSpec(bias.shape, const2),
            pl.BlockSpec((1, ATTN_WIDTH), const2),
            pl.BlockSpec((1, SG_WIDTH), const2),
            slab(wd), wo_block(0), wo_block(1),
        ],
        out_specs=[pl.BlockSpec((tm, d_model), row), slab(wd), slab(wo)],
        out_shape=[jax.ShapeDtypeStruct((batch * seq, d_model), BF16), bf16_like(wd), bf16_like(wo)],
        scratch_shapes=[pltpu.VMEM((tm, ATTN_WIDTH), F32)],
        compiler_params=pltpu.CompilerParams(
            dimension_semantics=("arbitrary", "arbitrary"), vmem_limit_bytes=VMEM_LIMIT_BYTES),
        name="prompt_mix",
    )(sinks, q, k, v, u, g, sg_w, bias, aon, son, wd, wo, wo)


def _sample_mix_body(sinks_ref, q_ref, k_ref, v_ref, ck_ref, cv_ref, u_ref, g_ref, sgw_ref,
                     bias_ref, aon_ref, son_ref, mix_ref, kw_ref, vw_ref, a_scr, *, dec_seq):
    rows = N_Q_HEADS * dec_seq
    quarter = lax.broadcasted_iota(jnp.int32, (dec_seq, KV_WIDTH), 1) // HEAD_DIM
    t = lax.broadcasted_iota(jnp.int32, (rows, 2 * WINDOW), 0) % dec_seq
    kj = lax.broadcasted_iota(jnp.int32, (rows, 2 * WINDOW), 1)
    pos_lane = lax.broadcasted_iota(jnp.int32, (KV_WIDTH, WINDOW), 1)
    row_head = lax.broadcasted_iota(jnp.int32, (rows, 1), 0) // dec_seq
    sink = jnp.zeros((rows, 1), F32)
    for a in range(N_Q_HEADS):
        sink = jnp.where(row_head == a, sinks_ref[a], sink)
    k_new_t = k_ref[...].T
    v_new_t = v_ref[...].T

    def one_seq(b, carry):
        r0 = pl.multiple_of(b * dec_seq, dec_seq)
        q = q_ref[pl.ds(r0, dec_seq), :]
        ck = ck_ref[b]
        cv = cv_ref[b]
        own = kj - WINDOW - b * dec_seq
        mask = ((kj < WINDOW) & (kj > t)) | ((own >= 0) & (own <= t))
        lhs = jnp.concatenate(
            [jnp.where(quarter == h, q[:, KV_WIDTH * g:KV_WIDTH * (g + 1)], 0.0)
             for h in range(N_KV_HEADS) for g in range(Q_PER_KV)], axis=0).astype(BF16)
        k_all = jnp.concatenate([ck, k_new_t], axis=1).astype(BF16)
        v_all = jnp.concatenate([cv, v_new_t], axis=1).astype(BF16)
        s = jnp.dot(lhs, k_all, preferred_element_type=F32)
        p, inv = _softmax_sink(s, mask, sink)
        o = lax.dot_general(p.astype(BF16), v_all, (((1,), (1,)), ((), ())),
                            preferred_element_type=F32) * inv
        groups = []
        for g in range(Q_PER_KV):
            piece = lambda h: o[(Q_PER_KV * h + g) * dec_seq:(Q_PER_KV * h + g + 1) * dec_seq]
            acc = piece(N_KV_HEADS - 1)
            for h in range(N_KV_HEADS - 2, -1, -1):
                acc = jnp.where(quarter == h, piece(h), acc)
            groups.append(acc)
        a_scr[pl.ds(r0, dec_seq), :] = jnp.concatenate(groups, axis=1)
        return carry

    lax.fori_loop(0, ck_ref.shape[0], one_seq, 0, unroll=4)

    keep = pos_lane < WINDOW - dec_seq
    for b in range(ck_ref.shape[0]):
        shift = WINDOW - dec_seq - b * dec_seq
        kw_ref[b] = jnp.where(keep, pltpu.roll(ck_ref[b], WINDOW - dec_seq, 1),
                              pltpu.roll(k_new_t, shift, 1))
        vw_ref[b] = jnp.where(keep, pltpu.roll(cv_ref[b], WINDOW - dec_seq, 1),
                              pltpu.roll(v_new_t, shift, 1))

    sgo = _spatial_gate(u_ref[...], g_ref[...], sgw_ref, bias_ref, dec_seq)
    mix_ref[...] = _normed_mix(a_scr[...], sgo, aon_ref[...], son_ref[...])


def _sample_mix(sinks, q, k, v, ck, cv, u, g, sg_w, bias, aon, son, dec_seq):
    nseq = ck.shape[0]
    sb = SEQS_PER_STEP
    tm = sb * dec_seq
    assert tm == WINDOW == CHUNK and dec_seq % 8 == 0
    row = lambda i: (i, 0)
    seq3 = lambda i: (i, 0, 0)
    const2 = lambda i: (0, 0)
    const3 = lambda i: (0, 0, 0)
    return pl.pallas_call(
        functools.partial(_sample_mix_body, dec_seq=dec_seq),
        grid=(nseq // sb,),
        in_specs=[
            pl.BlockSpec(memory_space=pltpu.SMEM),
            pl.BlockSpec((tm, ATTN_WIDTH), row),
            pl.BlockSpec((tm, KV_WIDTH), row),
            pl.BlockSpec((tm, KV_WIDTH), row),
            pl.BlockSpec((sb, KV_WIDTH, WINDOW), seq3),
            pl.BlockSpec((sb, KV_WIDTH, WINDOW), seq3),
            pl.BlockSpec((tm, SG_WIDTH), row),
            pl.BlockSpec((tm, SG_WIDTH), row),
            pl.BlockSpec(sg_w.shape, const3),
            pl.BlockSpec(bias.shape, const2),
            pl.BlockSpec((1, ATTN_WIDTH), const2),
            pl.BlockSpec((1, SG_WIDTH), const2),
        ],
        out_specs=[
            pl.BlockSpec((tm, ATTN_WIDTH + SG_WIDTH), row),
            pl.BlockSpec((sb, KV_WIDTH, WINDOW), seq3),
            pl.BlockSpec((sb, KV_WIDTH, WINDOW), seq3),
        ],
        out_shape=[
            jax.ShapeDtypeStruct((nseq * dec_seq, ATTN_WIDTH + SG_WIDTH), BF16),
            jax.ShapeDtypeStruct(ck.shape, F32),
            jax.ShapeDtypeStruct(cv.shape, F32),
        ],
        scratch_shapes=[pltpu.VMEM((tm, ATTN_WIDTH), F32)],
        compiler_params=pltpu.CompilerParams(
            dimension_semantics=("arbitrary",), vmem_limit_bytes=VMEM_LIMIT_BYTES),
        name="sample_mix",
    )(sinks, q, k, v, ck, cv, u, g, sg_w, bias, aon, son)


def _tail_body(x_ref, mix_ref, wo_ref, fn_ref, wg_ref, wu_ref, wd_ref, y_ref, hn_scr):
    def ffn_chunk():
        hn = hn_scr[...]
        gate = jnp.dot(hn, wg_ref[...], preferred_element_type=F32)
        up = jnp.dot(hn, wu_ref[...], preferred_element_type=F32)
        act = (jax.nn.silu(gate) * up).astype(BF16)
        y_ref[...] += jnp.dot(act, wd_ref[...], preferred_element_type=F32)

    @pl.when(pl.program_id(1) == 0)
    def _():
        half = x_ref.shape[0] // 2
        for rows in (slice(0, half), slice(half, 2 * half)):
            y_ref[rows, :] = x_ref[rows, :] + jnp.dot(
                mix_ref[rows, :], wo_ref[...], preferred_element_type=F32)
        for rows in (slice(0, half), slice(half, 2 * half)):
            hn_scr[rows, :] = _rms(y_ref[rows, :], fn_ref[...]).astype(BF16)
        ffn_chunk()

    @pl.when(pl.program_id(1) > 0)
    def _():
        ffn_chunk()


def _tail(x, mix, wo, ffn_norm, wg, wu, wd):
    rows, d_model = x.shape
    d_ff = wg.shape[1]
    tm = min(ROW_TILE, rows)
    tf = FF_TILE
    return pl.pallas_call(
        _tail_body,
        grid=(rows // tm, d_ff // tf),
        in_specs=[
            pl.BlockSpec((tm, d_model), lambda i, j: (i, 0)),
            pl.BlockSpec((tm, mix.shape[1]), lambda i, j: (i, 0)),
            pl.BlockSpec(wo.shape, lambda i, j: (0, 0), pipeline_mode=pl.Buffered(1)),
            pl.BlockSpec((1, d_model), lambda i, j: (0, 0)),
            pl.BlockSpec((d_model, tf), lambda i, j: (0, j)),
            pl.BlockSpec((d_model, tf), lambda i, j: (0, j)),
            pl.BlockSpec((tf, d_model), lambda i, j: (j, 0)),
        ],
        out_specs=pl.BlockSpec((tm, d_model), lambda i, j: (i, 0)),
        out_shape=jax.ShapeDtypeStruct(x.shape, F32),
        scratch_shapes=[pltpu.VMEM((tm, d_model), BF16)],
        compiler_params=pltpu.CompilerParams(
            dimension_semantics=("arbitrary", "arbitrary"), vmem_limit_bytes=VMEM_LIMIT_BYTES),
        name="tail",
    )(x, mix, wo, ffn_norm, wg, wu, wd)


def _rope_tables(pos):
    half = HEAD_DIM // 2
    lane = jnp.arange(LANES)
    inv = ROPE_THETA ** (-(lane % half).astype(F32) / half)
    ang = pos.astype(F32)[:, None] * inv[None, :]
    sign = jnp.where(lane % HEAD_DIM < half, -1.0, 1.0).astype(F32)
    return jnp.cos(ang), jnp.sin(ang) * sign[None, :]


def _windows_to_native(w):
    return jnp.transpose(w, (0, 2, 3, 1)).reshape(w.shape[0], KV_WIDTH, WINDOW)


def _windows_from_native(w):
    return jnp.transpose(w.reshape(w.shape[0], N_KV_HEADS, HEAD_DIM, WINDOW), (0, 3, 1, 2))[None]


def kernel(x_prompt, x_sample, cache_k_win, cache_v_win, attn_norm, w_in, q_norm, k_norm, sinks,
           sg_norm, sg_w, sg_b, attn_out_norm, sg_out_norm, w_o, ffn_norm, w_gate, w_up, w_down):
    assert w_in.shape[0] == 1, "single-layer step only"
    batch, seq, d_model = x_prompt.shape
    dec_batch, dec_seq, _ = x_sample.shape

    w_in_b = w_in[0].astype(BF16)
    sink_vec = sinks[0]
    sgw = sg_w[0][:, :CHUNK, :CHUNK]
    sgb = sg_b[0][:, :CHUNK]

    cos_p, sin_p = _rope_tables(jnp.arange(seq, dtype=jnp.int32))
    tm_s = min(ROW_TILE, dec_batch * dec_seq)
    cos_s, sin_s = _rope_tables(PAST_LEN + jnp.arange(tm_s, dtype=jnp.int32) % dec_seq)

    xp = x_prompt.reshape(batch * seq, d_model)
    q, k, v, u, g, wg, wu, kwin, vwin = _inproj(xp, attn_norm, w_in_b, q_norm, k_norm, sg_norm,
                                                cos_p, sin_p, BF16, seq,
                                                casts=(w_gate[0], w_up[0]))
    mix_p, wd, wo = _prompt_mix(sink_vec, q, k, v, u, g, sgw, sgb, attn_out_norm, sg_out_norm,
                                w_down[0], w_o[0], batch, seq)
    y_prompt = _tail(xp, mix_p, wo, ffn_norm, wg, wu, wd).reshape(x_prompt.shape)
    k_win_prompt = _windows_from_native(kwin)
    v_win_prompt = _windows_from_native(vwin)

    xs = x_sample.reshape(dec_batch * dec_seq, d_model)
    q, k, v, u, g = _inproj(xs, attn_norm, w_in_b, q_norm, k_norm, sg_norm, cos_s, sin_s, F32, None)
    mix_s, kw, vw = _sample_mix(sink_vec, q, k, v, _windows_to_native(cache_k_win[0]),
                                _windows_to_native(cache_v_win[0]), u, g, sgw, sgb,
                                attn_out_norm, sg_out_norm, dec_seq)
    y_sample = _tail(xs, mix_s, wo, ffn_norm, wg, wu, wd).reshape(x_sample.shape)
    k_win_sample = _windows_from_native(kw)
    v_win_sample = _windows_from_native(vw)
    sg_v_sample = g.reshape(1, dec_batch, dec_seq, SG_WIDTH)

    return (y_prompt, y_sample, k_win_prompt, v_win_prompt, k_win_sample, v_win_sample, sg_v_sample)
```

```python
import functools

import jax
import jax.numpy as jnp
from jax import lax
from jax.experimental import pallas as pl
from jax.experimental.pallas import tpu as pltpu

HEAD_DIM = 64
N_KV_HEADS = 4
Q_PER_KV = 4
N_Q_HEADS = N_KV_HEADS * Q_PER_KV
KV_WIDTH = N_KV_HEADS * HEAD_DIM
ATTN_WIDTH = N_Q_HEADS * HEAD_DIM
WINDOW = 128
N_SG_HEADS = 8
SG_HEAD_DIM = 128
SG_WIDTH = N_SG_HEADS * SG_HEAD_DIM
CHUNK = 128
PAST_LEN = 16384
ROPE_THETA = 10000.0
EPS = 1e-6

LANES = 128
VMEM_LIMIT_BYTES = 56 * 1024 * 1024

ROW_TILE = 512
INPROJ_SUB_BLOCKS = 4
FF_TILE = 512
SEQS_PER_STEP = 16

F32 = jnp.float32
BF16 = jnp.bfloat16


def _rms(x, gain_row):
    r = lax.rsqrt(jnp.mean(x * x, axis=-1, keepdims=True) + EPS)
    return x * r * gain_row


def _swapped_head(c):
    return Q_PER_KV * (c % N_KV_HEADS) + c // N_KV_HEADS


def _swap_heads(t):
    lane = lax.broadcasted_iota(jnp.int32, (t.shape[0], LANES), 1)
    lo = lane < HEAD_DIM
    outs = []
    for j in range(ATTN_WIDTH // LANES):
        halves = []
        for half in range(2):
            a = _swapped_head(2 * j + half)
            src = t[:, LANES * (a // 2):LANES * (a // 2 + 1)]
            halves.append(src if a % 2 == half else pltpu.roll(src, HEAD_DIM, 1))
        outs.append(jnp.where(lo, halves[0], halves[1]))
    return jnp.concatenate(outs, axis=1)


def _head_norm_rope(t, gain_row, cos, sin_signed):
    lane = lax.broadcasted_iota(jnp.int32, (t.shape[0], LANES), 1)
    lo_head = lane < HEAD_DIM
    first_half = (lane % HEAD_DIM) < (HEAD_DIM // 2)
    outs = []
    for j in range(t.shape[1] // LANES):
        blk = t[:, LANES * j:LANES * (j + 1)]
        sq = blk * blk
        s_all = jnp.sum(sq, axis=-1, keepdims=True)
        s_lo = jnp.sum(jnp.where(lo_head, sq, 0.0), axis=-1, keepdims=True)
        s_hi = s_all - s_lo
        r = jnp.where(lo_head,
                      lax.rsqrt(s_lo * (1.0 / HEAD_DIM) + EPS),
                      lax.rsqrt(s_hi * (1.0 / HEAD_DIM) + EPS))
        y = blk * r * gain_row
        rot = jnp.where(first_half,
                        pltpu.roll(y, LANES - HEAD_DIM // 2, 1),
                        pltpu.roll(y, HEAD_DIM // 2, 1))
        outs.append(y * cos + rot * sin_signed)
    return jnp.concatenate(outs, axis=1)


def _inproj_body(x_ref, an_ref, w_ref, qn_ref, kn_ref, sgn_ref, cos_ref, sin_ref, *rest,
                 tiles_per_seq, n_cast):
    casts_in, rest = rest[:n_cast], rest[n_cast:]
    q_ref, k_ref, v_ref, u_ref, g_ref = rest[:5]
    casts_out, rest = rest[5:5 + n_cast], rest[5 + n_cast:]
    proj_scr = rest[-1]
    n_sub, sub = proj_scr.shape[0], proj_scr.shape[1]
    q_gain = jnp.concatenate([qn_ref[...]] * (LANES // HEAD_DIM), axis=1)
    k_gain = jnp.concatenate([kn_ref[...]] * (LANES // HEAD_DIM), axis=1)
    c0, c1, c2, c3 = ATTN_WIDTH, ATTN_WIDTH + KV_WIDTH, ATTN_WIDTH + 2 * KV_WIDTH, \
        ATTN_WIDTH + 2 * KV_WIDTH + SG_WIDTH

    def project(r):
        xn = _rms(x_ref[sub * r:sub * (r + 1), :], an_ref[...]).astype(BF16)
        proj_scr[r] = jnp.dot(xn, w_ref[...], preferred_element_type=F32)

    def finish(r):
        rows = slice(sub * r, sub * (r + 1))
        cos = cos_ref[rows, :]
        sin = sin_ref[rows, :]
        q = _head_norm_rope(proj_scr[r, :, 0:c0], q_gain, cos, sin) * (HEAD_DIM ** -0.5)
        q_ref[rows, :] = _swap_heads(q).astype(q_ref.dtype)
        k = _head_norm_rope(proj_scr[r, :, c0:c1], k_gain, cos, sin)
        k_ref[rows, :] = k.astype(k_ref.dtype)
        v = proj_scr[r, :, c1:c2]
        v_ref[rows, :] = v.astype(v_ref.dtype)
        u_ref[rows, :] = jax.nn.gelu(proj_scr[r, :, c2:c3]).astype(u_ref.dtype)
        g = jax.nn.gelu(proj_scr[r, :, c3:])
        g_ref[rows, :] = _rms(g, sgn_ref[...]).astype(g_ref.dtype)
        return k, v

    project(0)
    for src, dst in zip(casts_in, casts_out):
        dst[...] = src[...].astype(BF16)
    for r in range(1, n_sub):
        project(r)
        finish(r - 1)
    k, v = finish(n_sub - 1)

    if len(rest) > 1:
        kwin_ref, vwin_ref = rest[0], rest[1]

        @pl.when(pl.program_id(0) % tiles_per_seq == tiles_per_seq - 1)
        def _():
            kwin_ref[0] = k[sub - WINDOW:].T
            vwin_ref[0] = v[sub - WINDOW:].T


def _inproj(x, attn_norm, w_in, q_norm, k_norm, sg_norm, cos_t, sin_t, act_dtype, seq, casts=()):
    rows, d_model = x.shape
    tm = min(ROW_TILE, rows)
    steps = rows // tm
    n_tab = cos_t.shape[0] // tm
    const = lambda i: (0, 0)
    row = lambda i: (i, 0)
    tab = lambda i: (i % n_tab, 0)
    slabs = [pl.BlockSpec((w.shape[0] // steps, w.shape[1]), row) for w in casts]
    out_specs = [
        pl.BlockSpec((tm, ATTN_WIDTH), row),
        pl.BlockSpec((tm, KV_WIDTH), row),
        pl.BlockSpec((tm, KV_WIDTH), row),
        pl.BlockSpec((tm, SG_WIDTH), row),
        pl.BlockSpec((tm, SG_WIDTH), row),
    ]
    out_shape = [
        jax.ShapeDtypeStruct((rows, ATTN_WIDTH), act_dtype),
        jax.ShapeDtypeStruct((rows, KV_WIDTH), act_dtype),
        jax.ShapeDtypeStruct((rows, KV_WIDTH), act_dtype),
        jax.ShapeDtypeStruct((rows, SG_WIDTH), BF16),
        jax.ShapeDtypeStruct((rows, SG_WIDTH), act_dtype),
    ]
    out_specs += slabs
    out_shape += [jax.ShapeDtypeStruct(w.shape, BF16) for w in casts]
    tiles_per_seq = 1
    if seq is not None:
        tiles_per_seq = seq // tm
        win = lambda i: (i // tiles_per_seq, 0, 0)
        out_specs += [pl.BlockSpec((1, KV_WIDTH, WINDOW), win)] * 2
        out_shape += [jax.ShapeDtypeStruct((rows // seq, KV_WIDTH, WINDOW), F32)] * 2
    return pl.pallas_call(
        functools.partial(_inproj_body, tiles_per_seq=tiles_per_seq, n_cast=len(casts)),
        grid=(steps,),
        in_specs=[
            pl.BlockSpec((tm, d_model), row),
            pl.BlockSpec((1, d_model), const),
            pl.BlockSpec(w_in.shape, const, pipeline_mode=pl.Buffered(1)),
            pl.BlockSpec((1, HEAD_DIM), const),
            pl.BlockSpec((1, HEAD_DIM), const),
            pl.BlockSpec((1, SG_WIDTH), const),
            pl.BlockSpec((tm, LANES), tab),
            pl.BlockSpec((tm, LANES), tab),
        ] + slabs,
        out_specs=out_specs,
        out_shape=out_shape,
        scratch_shapes=[pltpu.VMEM((INPROJ_SUB_BLOCKS, tm // INPROJ_SUB_BLOCKS, w_in.shape[1]), F32)],
        compiler_params=pltpu.CompilerParams(
            dimension_semantics=("arbitrary",), vmem_limit_bytes=VMEM_LIMIT_BYTES),
        name="inproj",
    )(x, attn_norm, w_in, q_norm, k_norm, sg_norm, cos_t, sin_t, *casts)


def _softmax_sink(s, mask, sink):
    s = jnp.where(mask, s, -jnp.inf)
    m = jnp.maximum(jnp.max(s, axis=-1, keepdims=True), sink)
    p = jnp.exp(s - m)
    denom = jnp.sum(p, axis=-1, keepdims=True) + jnp.exp(sink - m)
    return p, 1.0 / denom


def _spatial_gate(u, g, w_ref, b_ref, rows_per_seq):
    r = lax.broadcasted_iota(jnp.int32, (CHUNK, CHUNK), 0)
    c = lax.broadcasted_iota(jnp.int32, (CHUNK, CHUNK), 1)
    causal = (r // rows_per_seq == c // rows_per_seq) & (c % rows_per_seq <= r % rows_per_seq)
    gb = g.astype(BF16)
    b_t = b_ref[...].T[0:rows_per_seq, :]
    b_t = jnp.concatenate([b_t] * (CHUNK // rows_per_seq), axis=0)
    periodic = ((r < rows_per_seq) & (c % rows_per_seq == r)).astype(BF16)
    outs = []
    for h in range(N_SG_HEADS):
        if rows_per_seq == CHUNK:
            w = w_ref[h]
        else:
            w = jnp.dot(w_ref[h, 0:rows_per_seq, :].astype(BF16), periodic, preferred_element_type=F32)
            w = jnp.concatenate([w] * (CHUNK // rows_per_seq), axis=0)
        w = jnp.where(causal, w, 0.0).astype(BF16)
        cols = slice(SG_HEAD_DIM * h, SG_HEAD_DIM * (h + 1))
        bias = jnp.broadcast_to(b_t[:, h:h + 1], (CHUNK, SG_HEAD_DIM))
        parts = []
        for ch in range(u.shape[0] // CHUNK):
            rws = slice(CHUNK * ch, CHUNK * (ch + 1))
            mixed = jnp.dot(w, gb[rws, cols], preferred_element_type=F32) + bias
            parts.append(u[rws, cols].astype(F32) * mixed)
        outs.append(jnp.concatenate(parts, axis=0))
    return jnp.concatenate(outs, axis=1)


def _normed_mix(a_swapped, sgo, aon, son):
    return jnp.concatenate([_rms(a_swapped, _swap_heads(aon)), _rms(sgo, son)], axis=1).astype(BF16)


def _prompt_mix_body(sinks_ref, q_ref, k_ref, v_ref, u_ref, g_ref, sgw_ref, bias_ref,
                     aon_ref, son_ref, wd32, wo32_lo, wo32_hi, mix_ref, wd16, wo16, a_scr):
    wd16[...] = wd32[...].astype(BF16)
    wo16[0:HEAD_DIM, :] = wo32_lo[...].astype(BF16)
    wo16[HEAD_DIM:2 * HEAD_DIM, :] = wo32_hi[...].astype(BF16)

    j = pl.program_id(1)
    blocks = q_ref.shape[0] // WINDOW
    quarter = lax.broadcasted_iota(jnp.int32, (WINDOW, KV_WIDTH), 1) // HEAD_DIM
    qi = lax.broadcasted_iota(jnp.int32, (Q_PER_KV * WINDOW, WINDOW), 0) % WINDOW
    kj = lax.broadcasted_iota(jnp.int32, (Q_PER_KV * WINDOW, WINDOW), 1)
    from_prev = kj > qi
    row_g = lax.broadcasted_iota(jnp.int32, (Q_PER_KV * WINDOW, 1), 0) // WINDOW
    sink_cols = []
    for h in range(N_KV_HEADS):
        col = jnp.zeros((Q_PER_KV * WINDOW, 1), F32)
        for g in range(Q_PER_KV):
            col = jnp.where(row_g == g, sinks_ref[Q_PER_KV * h + g], col)
        sink_cols.append(col)

    def block(r, carry):
        n = j * blocks + r
        prev = pl.multiple_of(jnp.maximum(n - 1, 0) * WINDOW, WINDOW)
        cur = pl.multiple_of(n * WINDOW, WINDOW)
        no_prev = jnp.where(n > 0, 0.0, -jnp.inf)
        q = q_ref[pl.ds(pl.multiple_of(r * WINDOW, WINDOW), WINDOW), :]
        kb = jnp.concatenate([k_ref[pl.ds(prev, WINDOW), :], k_ref[pl.ds(cur, WINDOW), :]], axis=0)
        vb = jnp.concatenate([v_ref[pl.ds(prev, WINDOW), :], v_ref[pl.ds(cur, WINDOW), :]], axis=0)
        groups = None
        for h in range(N_KV_HEADS):
            sel = quarter == h
            lhs = jnp.concatenate(
                [jnp.where(sel, q[:, KV_WIDTH * g:KV_WIDTH * (g + 1)], 0) for g in range(Q_PER_KV)],
                axis=0)
            s = lax.dot_general(lhs, kb, (((1,), (1,)), ((), ())), preferred_element_type=F32)
            s = jnp.where(from_prev, s[:, :WINDOW] + no_prev, s[:, WINDOW:])
            sink = sink_cols[h]
            m = jnp.maximum(jnp.max(s, axis=-1, keepdims=True), sink)
            p = jnp.exp(s - m)
            inv = 1.0 / (jnp.sum(p, axis=-1, keepdims=True) + jnp.exp(sink - m))
            p_band = jnp.concatenate(
                [jnp.where(from_prev, p, 0.0), jnp.where(from_prev, 0.0, p)], axis=1).astype(BF16)
            o = jnp.dot(p_band, vb, preferred_element_type=F32) * inv
            parts = [o[WINDOW * g:WINDOW * (g + 1)] for g in range(Q_PER_KV)]
            groups = parts if groups is None else [
                jnp.where(sel, part, acc) for part, acc in zip(parts, groups)]
        a_scr[pl.ds(pl.multiple_of(r * WINDOW, WINDOW), WINDOW), :] = jnp.concatenate(groups, axis=1)
        return carry

    lax.fori_loop(0, blocks, block, 0, unroll=2)

    sgo = _spatial_gate(u_ref[...], g_ref[...], sgw_ref, bias_ref, CHUNK)
    mix_ref[...] = _normed_mix(a_scr[...], sgo, aon_ref[...], son_ref[...])


def _prompt_mix(sinks, q, k, v, u, g, sg_w, bias, aon, son, wd, wo, batch, seq):
    tm = ROW_TILE
    tiles = seq // tm
    steps = batch * tiles
    d_model = wo.shape[1]
    assert wo.shape[0] == 2 * HEAD_DIM * steps
    row = lambda b, j: (b * tiles + j, 0)
    per_seq = lambda b, j: (b, 0)
    const2 = lambda b, j: (0, 0)
    const3 = lambda b, j: (0, 0, 0)
    slab = lambda w: pl.BlockSpec((w.shape[0] // steps, w.shape[1]), row)
    bf16_like = lambda w: jax.ShapeDtypeStruct(w.shape, BF16)

    def wo_block(half):
        def index(b, j):
            c = 2 * (b * tiles + j) + half
            return jnp.where(c < N_Q_HEADS, _swapped_head(c), c), 0
        return pl.BlockSpec((HEAD_DIM, d_model), index)

    return pl.pallas_call(
        _prompt_mix_body,
        grid=(batch, tiles),
        in_specs=[
            pl.BlockSpec(memory_space=pltpu.SMEM),
            pl.BlockSpec((tm, ATTN_WIDTH), row),
            pl.BlockSpec((seq, KV_WIDTH), per_seq),
            pl.BlockSpec((seq, KV_WIDTH), per_seq),
            pl.BlockSpec((tm, SG_WIDTH), row),
            pl.BlockSpec((tm, SG_WIDTH), row),
            pl.BlockSpec(sg_w.shape, const3),
            pl.BlockSpec(bias.shape, const2),
            pl.BlockSpec((1, ATTN_WIDTH), const2),
            pl.BlockSpec((1, SG_WIDTH), const2),
            slab(wd), wo_block(0), wo_block(1),
        ],
        out_specs=[pl.BlockSpec((tm, d_model), row), slab(wd), slab(wo)],
        out_shape=[jax.ShapeDtypeStruct((batch * seq, d_model), BF16), bf16_like(wd), bf16_like(wo)],
        scratch_shapes=[pltpu.VMEM((tm, ATTN_WIDTH), F32)],
        compiler_params=pltpu.CompilerParams(
            dimension_semantics=("arbitrary", "arbitrary"), vmem_limit_bytes=VMEM_LIMIT_BYTES),
        name="prompt_mix",
    )(sinks, q, k, v, u, g, sg_w, bias, aon, son, wd, wo, wo)


def _sample_mix_body(sinks_ref, q_ref, k_ref, v_ref, ck_ref, cv_ref, u_ref, g_ref, sgw_ref,
                     bias_ref, aon_ref, son_ref, mix_ref, knt_ref, vnt_ref, a_scr, *, dec_seq):
    rows = N_Q_HEADS * dec_seq
    quarter = lax.broadcasted_iota(jnp.int32, (dec_seq, KV_WIDTH), 1) // HEAD_DIM
    t = lax.broadcasted_iota(jnp.int32, (rows, 2 * WINDOW), 0) % dec_seq
    kj = lax.broadcasted_iota(jnp.int32, (rows, 2 * WINDOW), 1)
    row_head = lax.broadcasted_iota(jnp.int32, (rows, 1), 0) // dec_seq
    sink = jnp.zeros((rows, 1), F32)
    for a in range(N_Q_HEADS):
        sink = jnp.where(row_head == a, sinks_ref[a], sink)
    k_new_t = k_ref[...].T
    v_new_t = v_ref[...].T
    knt_ref[0] = k_new_t
    vnt_ref[0] = v_new_t

    def one_seq(b, carry):
        r0 = pl.multiple_of(b * dec_seq, dec_seq)
        q = q_ref[pl.ds(r0, dec_seq), :]
        ck = ck_ref[b]
        cv = cv_ref[b]
        own = kj - WINDOW - b * dec_seq
        mask = ((kj < WINDOW) & (kj > t)) | ((own >= 0) & (own <= t))
        lhs = jnp.concatenate(
            [jnp.where(quarter == h, q[:, KV_WIDTH * g:KV_WIDTH * (g + 1)], 0.0)
             for h in range(N_KV_HEADS) for g in range(Q_PER_KV)], axis=0).astype(BF16)
        k_all = jnp.concatenate([ck, k_new_t], axis=1).astype(BF16)
        v_all = jnp.concatenate([cv, v_new_t], axis=1).astype(BF16)
        s = jnp.dot(lhs, k_all, preferred_element_type=F32)
        p, inv = _softmax_sink(s, mask, sink)
        o = lax.dot_general(p.astype(BF16), v_all, (((1,), (1,)), ((), ())),
                            preferred_element_type=F32) * inv
        groups = []
        for g in range(Q_PER_KV):
            piece = lambda h: o[(Q_PER_KV * h + g) * dec_seq:(Q_PER_KV * h + g + 1) * dec_seq]
            acc = piece(N_KV_HEADS - 1)
            for h in range(N_KV_HEADS - 2, -1, -1):
                acc = jnp.where(quarter == h, piece(h), acc)
            groups.append(acc)
        a_scr[pl.ds(r0, dec_seq), :] = jnp.concatenate(groups, axis=1)
        return carry

    lax.fori_loop(0, ck_ref.shape[0], one_seq, 0, unroll=4)

    sgo = _spatial_gate(u_ref[...], g_ref[...], sgw_ref, bias_ref, dec_seq)
    mix_ref[...] = _normed_mix(a_scr[...], sgo, aon_ref[...], son_ref[...])


def _sample_mix(sinks, q, k, v, ck, cv, u, g, sg_w, bias, aon, son, dec_seq):
    nseq = ck.shape[0]
    sb = SEQS_PER_STEP
    tm = sb * dec_seq
    assert tm == WINDOW == CHUNK and dec_seq % 8 == 0
    row = lambda i: (i, 0)
    seq3 = lambda i: (i, 0, 0)
    const2 = lambda i: (0, 0)
    const3 = lambda i: (0, 0, 0)
    return pl.pallas_call(
        functools.partial(_sample_mix_body, dec_seq=dec_seq),
        grid=(nseq // sb,),
        in_specs=[
            pl.BlockSpec(memory_space=pltpu.SMEM),
            pl.BlockSpec((tm, ATTN_WIDTH), row),
            pl.BlockSpec((tm, KV_WIDTH), row),
            pl.BlockSpec((tm, KV_WIDTH), row),
            pl.BlockSpec((sb, KV_WIDTH, WINDOW), seq3),
            pl.BlockSpec((sb, KV_WIDTH, WINDOW), seq3),
            pl.BlockSpec((tm, SG_WIDTH), row),
            pl.BlockSpec((tm, SG_WIDTH), row),
            pl.BlockSpec(sg_w.shape, const3),
            pl.BlockSpec(bias.shape, const2),
            pl.BlockSpec((1, ATTN_WIDTH), const2),
            pl.BlockSpec((1, SG_WIDTH), const2),
        ],
        out_specs=[
            pl.BlockSpec((tm, ATTN_WIDTH + SG_WIDTH), row),
            pl.BlockSpec((1, KV_WIDTH, tm), seq3),
            pl.BlockSpec((1, KV_WIDTH, tm), seq3),
        ],
        out_shape=[
            jax.ShapeDtypeStruct((nseq * dec_seq, ATTN_WIDTH + SG_WIDTH), BF16),
            jax.ShapeDtypeStruct((nseq // sb, KV_WIDTH, tm), F32),
            jax.ShapeDtypeStruct((nseq // sb, KV_WIDTH, tm), F32),
        ],
        scratch_shapes=[pltpu.VMEM((tm, ATTN_WIDTH), F32)],
        compiler_params=pltpu.CompilerParams(
            dimension_semantics=("arbitrary",), vmem_limit_bytes=VMEM_LIMIT_BYTES),
        name="sample_mix",
    )(sinks, q, k, v, ck, cv, u, g, sg_w, bias, aon, son)


def _tail_body(x_ref, mix_ref, wo_ref, fn_ref, wg_ref, wu_ref, wd_ref, *rest, window_seqs, dec_seq):
    if window_seqs:
        ck_ref, cv_ref, knt_ref, vnt_ref, y_ref, kw_ref, vw_ref, hn_scr = rest
    else:
        y_ref, hn_scr = rest

    def update_windows():
        step = pl.program_id(0) * pl.num_programs(1) + pl.program_id(1)
        in_group = jnp.minimum(step, window_seqs - 1) % SEQS_PER_STEP
        shift = WINDOW - dec_seq - in_group * dec_seq
        keep = lax.broadcasted_iota(jnp.int32, (KV_WIDTH, WINDOW), 1) < WINDOW - dec_seq
        kw_ref[0] = jnp.where(keep, pltpu.roll(ck_ref[0], WINDOW - dec_seq, 1),
                              pltpu.roll(knt_ref[0], shift, 1))
        vw_ref[0] = jnp.where(keep, pltpu.roll(cv_ref[0], WINDOW - dec_seq, 1),
                              pltpu.roll(vnt_ref[0], shift, 1))

    def ffn_chunk():
        if window_seqs:
            update_windows()
        hn = hn_scr[...]
        gate = jnp.dot(hn, wg_ref[...], preferred_element_type=F32)
        up = jnp.dot(hn, wu_ref[...], preferred_element_type=F32)
        act = (jax.nn.silu(gate) * up).astype(BF16)
        y_ref[...] += jnp.dot(act, wd_ref[...], preferred_element_type=F32)

    @pl.when(pl.program_id(1) == 0)
    def _():
        half = x_ref.shape[0] // 2
        for rows in (slice(0, half), slice(half, 2 * half)):
            y_ref[rows, :] = x_ref[rows, :] + jnp.dot(
                mix_ref[rows, :], wo_ref[...], preferred_element_type=F32)
        for rows in (slice(0, half), slice(half, 2 * half)):
            hn_scr[rows, :] = _rms(y_ref[rows, :], fn_ref[...]).astype(BF16)
        ffn_chunk()

    @pl.when(pl.program_id(1) > 0)
    def _():
        ffn_chunk()


def _tail(x, mix, wo, ffn_norm, wg, wu, wd, windows=None):
    rows, d_model = x.shape
    d_ff = wg.shape[1]
    tm = min(ROW_TILE, rows)
    tf = FF_TILE
    chunks = d_ff // tf
    in_specs = [
        pl.BlockSpec((tm, d_model), lambda i, j: (i, 0)),
        pl.BlockSpec((tm, mix.shape[1]), lambda i, j: (i, 0)),
        pl.BlockSpec(wo.shape, lambda i, j: (0, 0), pipeline_mode=pl.Buffered(1)),
        pl.BlockSpec((1, d_model), lambda i, j: (0, 0)),
        pl.BlockSpec((d_model, tf), lambda i, j: (0, j)),
        pl.BlockSpec((d_model, tf), lambda i, j: (0, j)),
        pl.BlockSpec((tf, d_model), lambda i, j: (j, 0)),
    ]
    out_specs = [pl.BlockSpec((tm, d_model), lambda i, j: (i, 0))]
    out_shape = [jax.ShapeDtypeStruct(x.shape, F32)]
    operands = [x, mix, wo, ffn_norm, wg, wu, wd]
    window_seqs, dec_seq = 0, 0
    if windows is not None:
        ck, cv, knt, vnt, dec_seq = windows
        window_seqs = ck.shape[0]
        assert (rows // tm) * chunks >= window_seqs
        seq = lambda i, j: (jnp.minimum(i * chunks + j, window_seqs - 1), 0, 0)
        group = lambda i, j: (jnp.minimum(i * chunks + j, window_seqs - 1) // SEQS_PER_STEP, 0, 0)
        one_seq = pl.BlockSpec((1, KV_WIDTH, WINDOW), seq)
        in_specs += [one_seq, one_seq, pl.BlockSpec((1,) + knt.shape[1:], group),
                     pl.BlockSpec((1,) + vnt.shape[1:], group)]
        out_specs += [one_seq, one_seq]
        out_shape += [jax.ShapeDtypeStruct(ck.shape, F32), jax.ShapeDtypeStruct(cv.shape, F32)]
        operands += [ck, cv, knt, vnt]
    return pl.pallas_call(
        functools.partial(_tail_body, window_seqs=window_seqs, dec_seq=dec_seq),
        grid=(rows // tm, chunks),
        in_specs=in_specs,
        out_specs=out_specs,
        out_shape=out_shape,
        scratch_shapes=[pltpu.VMEM((tm, d_model), BF16)],
        compiler_params=pltpu.CompilerParams(
            dimension_semantics=("arbitrary", "arbitrary"), vmem_limit_bytes=VMEM_LIMIT_BYTES),
        name="tail",
    )(*operands)


def _rope_tables(pos):
    half = HEAD_DIM // 2
    lane = jnp.arange(LANES)
    inv = ROPE_THETA ** (-(lane % half).astype(F32) / half)
    ang = pos.astype(F32)[:, None] * inv[None, :]
    sign = jnp.where(lane % HEAD_DIM < half, -1.0, 1.0).astype(F32)
    return jnp.cos(ang), jnp.sin(ang) * sign[None, :]


def _windows_to_native(w):
    return jnp.transpose(w, (0, 2, 3, 1)).reshape(w.shape[0], KV_WIDTH, WINDOW)


def _windows_from_native(w):
    return jnp.transpose(w.reshape(w.shape[0], N_KV_HEADS, HEAD_DIM, WINDOW), (0, 3, 1, 2))[None]


def kernel(x_prompt, x_sample, cache_k_win, cache_v_win, attn_norm, w_in, q_norm, k_norm, sinks,
           sg_norm, sg_w, sg_b, attn_out_norm, sg_out_norm, w_o, ffn_norm, w_gate, w_up, w_down):
    assert w_in.shape[0] == 1, "single-layer step only"
    batch, seq, d_model = x_prompt.shape
    dec_batch, dec_seq, _ = x_sample.shape

    w_in_b = w_in[0].astype(BF16)
    sink_vec = sinks[0]
    sgw = sg_w[0][:, :CHUNK, :CHUNK]
    sgb = sg_b[0][:, :CHUNK]

    cos_p, sin_p = _rope_tables(jnp.arange(seq, dtype=jnp.int32))
    tm_s = min(ROW_TILE, dec_batch * dec_seq)
    cos_s, sin_s = _rope_tables(PAST_LEN + jnp.arange(tm_s, dtype=jnp.int32) % dec_seq)

    xp = x_prompt.reshape(batch * seq, d_model)
    q, k, v, u, g, wg, wu, kwin, vwin = _inproj(xp, attn_norm, w_in_b, q_norm, k_norm, sg_norm,
                                                cos_p, sin_p, BF16, seq,
                                                casts=(w_gate[0], w_up[0]))
    mix_p, wd, wo = _prompt_mix(sink_vec, q, k, v, u, g, sgw, sgb, attn_out_norm, sg_out_norm,
                                w_down[0], w_o[0], batch, seq)
    k_win_prompt = _windows_from_native(kwin)
    v_win_prompt = _windows_from_native(vwin)

    xs = x_sample.reshape(dec_batch * dec_seq, d_model)
    q, k, v, u, g = _inproj(xs, attn_norm, w_in_b, q_norm, k_norm, sg_norm, cos_s, sin_s, F32, None)
    ck = _windows_to_native(cache_k_win[0])
    cv = _windows_to_native(cache_v_win[0])
    mix_s, knt, vnt = _sample_mix(sink_vec, q, k, v, ck, cv, u, g, sgw, sgb,
                                  attn_out_norm, sg_out_norm, dec_seq)

    y_prompt, kw, vw = _tail(xp, mix_p, wo, ffn_norm, wg, wu, wd, windows=(ck, cv, knt, vnt, dec_seq))
    y_prompt = y_prompt.reshape(x_prompt.shape)
    (y_sample,) = _tail(xs, mix_s, wo, ffn_norm, wg, wu, wd)
    y_sample = y_sample.reshape(x_sample.shape)
    k_win_sample = _windows_from_native(kw)
    v_win_sample = _windows_from_native(vw)
    sg_v_sample = g.reshape(1, dec_batch, dec_seq, SG_WIDTH)

    return (y_prompt, y_sample, k_win_prompt, v_win_prompt, k_win_sample, v_win_sample, sg_v_sample)
```

```python
import functools

import jax
import jax.numpy as jnp
from jax import lax
from jax.experimental import pallas as pl
from jax.experimental.pallas import tpu as pltpu

HEAD_DIM = 64
N_KV_HEADS = 4
Q_PER_KV = 4
N_Q_HEADS = N_KV_HEADS * Q_PER_KV
KV_WIDTH = N_KV_HEADS * HEAD_DIM
ATTN_WIDTH = N_Q_HEADS * HEAD_DIM
WINDOW = 128
N_SG_HEADS = 8
SG_HEAD_DIM = 128
SG_WIDTH = N_SG_HEADS * SG_HEAD_DIM
CHUNK = 128
PAST_LEN = 16384
ROPE_THETA = 10000.0
EPS = 1e-6

LANES = 128
VMEM_LIMIT_BYTES = 56 * 1024 * 1024

ROW_TILE = 512
INPROJ_SUB_BLOCKS = 4
FF_TILE = 512
SEQS_PER_STEP = 16

F32 = jnp.float32
BF16 = jnp.bfloat16


def _rms(x, gain_row):
    r = lax.rsqrt(jnp.mean(x * x, axis=-1, keepdims=True) + EPS)
    return x * r * gain_row


def _swapped_head(c):
    return Q_PER_KV * (c % N_KV_HEADS) + c // N_KV_HEADS


def _swap_heads(t):
    lane = lax.broadcasted_iota(jnp.int32, (t.shape[0], LANES), 1)
    lo = lane < HEAD_DIM
    outs = []
    for j in range(ATTN_WIDTH // LANES):
        halves = []
        for half in range(2):
            a = _swapped_head(2 * j + half)
            src = t[:, LANES * (a // 2):LANES * (a // 2 + 1)]
            halves.append(src if a % 2 == half else pltpu.roll(src, HEAD_DIM, 1))
        outs.append(jnp.where(lo, halves[0], halves[1]))
    return jnp.concatenate(outs, axis=1)


def _head_norm_rope(t, gain_row, cos, sin_signed):
    lane = lax.broadcasted_iota(jnp.int32, (t.shape[0], LANES), 1)
    lo_head = lane < HEAD_DIM
    first_half = (lane % HEAD_DIM) < (HEAD_DIM // 2)
    outs = []
    for j in range(t.shape[1] // LANES):
        blk = t[:, LANES * j:LANES * (j + 1)]
        sq = blk * blk
        s_all = jnp.sum(sq, axis=-1, keepdims=True)
        s_lo = jnp.sum(jnp.where(lo_head, sq, 0.0), axis=-1, keepdims=True)
        s_hi = s_all - s_lo
        r = jnp.where(lo_head,
                      lax.rsqrt(s_lo * (1.0 / HEAD_DIM) + EPS),
                      lax.rsqrt(s_hi * (1.0 / HEAD_DIM) + EPS))
        y = blk * r * gain_row
        rot = jnp.where(first_half,
                        pltpu.roll(y, LANES - HEAD_DIM // 2, 1),
                        pltpu.roll(y, HEAD_DIM // 2, 1))
        outs.append(y * cos + rot * sin_signed)
    return jnp.concatenate(outs, axis=1)


def _inproj_body(x_ref, an_ref, w_ref, qn_ref, kn_ref, sgn_ref, cos_ref, sin_ref, *rest,
                 tiles_per_seq, n_cast):
    casts_in, rest = rest[:n_cast], rest[n_cast:]
    q_ref, k_ref, v_ref, u_ref, g_ref = rest[:5]
    casts_out, rest = rest[5:5 + n_cast], rest[5 + n_cast:]
    proj_scr = rest[-1]
    n_sub, sub = proj_scr.shape[0], proj_scr.shape[1]
    q_gain = jnp.concatenate([qn_ref[...]] * (LANES // HEAD_DIM), axis=1)
    k_gain = jnp.concatenate([kn_ref[...]] * (LANES // HEAD_DIM), axis=1)
    c0, c1, c2, c3 = ATTN_WIDTH, ATTN_WIDTH + KV_WIDTH, ATTN_WIDTH + 2 * KV_WIDTH, \
        ATTN_WIDTH + 2 * KV_WIDTH + SG_WIDTH

    def project(r):
        xn = _rms(x_ref[sub * r:sub * (r + 1), :], an_ref[...]).astype(BF16)
        proj_scr[r] = jnp.dot(xn, w_ref[...], preferred_element_type=F32)

    def finish(r):
        rows = slice(sub * r, sub * (r + 1))
        cos = cos_ref[rows, :]
        sin = sin_ref[rows, :]
        q = _head_norm_rope(proj_scr[r, :, 0:c0], q_gain, cos, sin) * (HEAD_DIM ** -0.5)
        q_ref[rows, :] = _swap_heads(q).astype(q_ref.dtype)
        k = _head_norm_rope(proj_scr[r, :, c0:c1], k_gain, cos, sin)
        k_ref[rows, :] = k.astype(k_ref.dtype)
        v = proj_scr[r, :, c1:c2]
        v_ref[rows, :] = v.astype(v_ref.dtype)
        u_ref[rows, :] = jax.nn.gelu(proj_scr[r, :, c2:c3]).astype(u_ref.dtype)
        g = jax.nn.gelu(proj_scr[r, :, c3:])
        g_ref[rows, :] = _rms(g, sgn_ref[...]).astype(g_ref.dtype)
        return k, v

    project(0)
    for src, dst in zip(casts_in, casts_out):
        dst[...] = src[...].astype(BF16)
    for r in range(1, n_sub):
        project(r)
        finish(r - 1)
    k, v = finish(n_sub - 1)

    if len(rest) > 1:
        kwin_ref, vwin_ref = rest[0], rest[1]

        @pl.when(pl.program_id(0) % tiles_per_seq == tiles_per_seq - 1)
        def _():
            kwin_ref[0] = k[sub - WINDOW:].T
            vwin_ref[0] = v[sub - WINDOW:].T


def _inproj(x, attn_norm, w_in, q_norm, k_norm, sg_norm, cos_t, sin_t, act_dtype, seq, casts=()):
    rows, d_model = x.shape
    tm = min(ROW_TILE, rows)
    steps = rows // tm
    n_tab = cos_t.shape[0] // tm
    const = lambda i: (0, 0)
    row = lambda i: (i, 0)
    tab = lambda i: (i % n_tab, 0)
    slabs = [pl.BlockSpec((w.shape[0] // steps, w.shape[1]), row) for w in casts]
    out_specs = [
        pl.BlockSpec((tm, ATTN_WIDTH), row),
        pl.BlockSpec((tm, KV_WIDTH), row),
        pl.BlockSpec((tm, KV_WIDTH), row),
        pl.BlockSpec((tm, SG_WIDTH), row),
        pl.BlockSpec((tm, SG_WIDTH), row),
    ]
    out_shape = [
        jax.ShapeDtypeStruct((rows, ATTN_WIDTH), act_dtype),
        jax.ShapeDtypeStruct((rows, KV_WIDTH), act_dtype),
        jax.ShapeDtypeStruct((rows, KV_WIDTH), act_dtype),
        jax.ShapeDtypeStruct((rows, SG_WIDTH), BF16),
        jax.ShapeDtypeStruct((rows, SG_WIDTH), act_dtype),
    ]
    out_specs += slabs
    out_shape += [jax.ShapeDtypeStruct(w.shape, BF16) for w in casts]
    tiles_per_seq = 1
    if seq is not None:
        tiles_per_seq = seq // tm
        win = lambda i: (i // tiles_per_seq, 0, 0)
        out_specs += [pl.BlockSpec((1, KV_WIDTH, WINDOW), win)] * 2
        out_shape += [jax.ShapeDtypeStruct((rows // seq, KV_WIDTH, WINDOW), F32)] * 2
    return pl.pallas_call(
        functools.partial(_inproj_body, tiles_per_seq=tiles_per_seq, n_cast=len(casts)),
        grid=(steps,),
        in_specs=[
            pl.BlockSpec((tm, d_model), row),
            pl.BlockSpec((1, d_model), const),
            pl.BlockSpec(w_in.shape, const, pipeline_mode=pl.Buffered(1)),
            pl.BlockSpec((1, HEAD_DIM), const),
            pl.BlockSpec((1, HEAD_DIM), const),
            pl.BlockSpec((1, SG_WIDTH), const),
            pl.BlockSpec((tm, LANES), tab),
            pl.BlockSpec((tm, LANES), tab),
        ] + slabs,
        out_specs=out_specs,
        out_shape=out_shape,
        scratch_shapes=[pltpu.VMEM((INPROJ_SUB_BLOCKS, tm // INPROJ_SUB_BLOCKS, w_in.shape[1]), F32)],
        compiler_params=pltpu.CompilerParams(
            dimension_semantics=("arbitrary",), vmem_limit_bytes=VMEM_LIMIT_BYTES),
        name="inproj",
    )(x, attn_norm, w_in, q_norm, k_norm, sg_norm, cos_t, sin_t, *casts)


def _softmax_sink(s, mask, sink):
    s = jnp.where(mask, s, -jnp.inf)
    m = jnp.maximum(jnp.max(s, axis=-1, keepdims=True), sink)
    p = jnp.exp(s - m)
    denom = jnp.sum(p, axis=-1, keepdims=True) + jnp.exp(sink - m)
    return p, 1.0 / denom


def _spatial_gate(u, g, w_ref, b_ref, rows_per_seq):
    r = lax.broadcasted_iota(jnp.int32, (CHUNK, CHUNK), 0)
    c = lax.broadcasted_iota(jnp.int32, (CHUNK, CHUNK), 1)
    causal = (r // rows_per_seq == c // rows_per_seq) & (c % rows_per_seq <= r % rows_per_seq)
    gb = g.astype(BF16)
    b_t = b_ref[...].T[0:rows_per_seq, :]
    b_t = jnp.concatenate([b_t] * (CHUNK // rows_per_seq), axis=0)
    periodic = ((r < rows_per_seq) & (c % rows_per_seq == r)).astype(BF16)
    outs = []
    for h in range(N_SG_HEADS):
        if rows_per_seq == CHUNK:
            w = w_ref[h]
        else:
            w = jnp.dot(w_ref[h, 0:rows_per_seq, :].astype(BF16), periodic, preferred_element_type=F32)
            w = jnp.concatenate([w] * (CHUNK // rows_per_seq), axis=0)
        w = jnp.where(causal, w, 0.0).astype(BF16)
        cols = slice(SG_HEAD_DIM * h, SG_HEAD_DIM * (h + 1))
        bias = jnp.broadcast_to(b_t[:, h:h + 1], (CHUNK, SG_HEAD_DIM))
        parts = []
        for ch in range(u.shape[0] // CHUNK):
            rws = slice(CHUNK * ch, CHUNK * (ch + 1))
            mixed = jnp.dot(w, gb[rws, cols], preferred_element_type=F32) + bias
            parts.append(u[rws, cols].astype(F32) * mixed)
        outs.append(jnp.concatenate(parts, axis=0))
    return jnp.concatenate(outs, axis=1)


def _normed_mix(a_swapped, sgo, aon, son):
    return jnp.concatenate([_rms(a_swapped, _swap_heads(aon)), _rms(sgo, son)], axis=1).astype(BF16)


def _prompt_mix_body(sinks_ref, q_ref, k_ref, v_ref, u_ref, g_ref, sgw_ref, bias_ref,
                     aon_ref, son_ref, wd32, wo32_lo, wo32_hi, mix_ref, wd16, wo16, a_scr):
    wd16[...] = wd32[...].astype(BF16)
    wo16[0:HEAD_DIM, :] = wo32_lo[...].astype(BF16)
    wo16[HEAD_DIM:2 * HEAD_DIM, :] = wo32_hi[...].astype(BF16)

    j = pl.program_id(1)
    blocks = q_ref.shape[0] // WINDOW
    quarter = lax.broadcasted_iota(jnp.int32, (WINDOW, KV_WIDTH), 1) // HEAD_DIM
    qi = lax.broadcasted_iota(jnp.int32, (Q_PER_KV * WINDOW, WINDOW), 0) % WINDOW
    kj = lax.broadcasted_iota(jnp.int32, (Q_PER_KV * WINDOW, WINDOW), 1)
    from_prev = kj > qi
    row_g = lax.broadcasted_iota(jnp.int32, (Q_PER_KV * WINDOW, 1), 0) // WINDOW
    sink_cols = []
    for h in range(N_KV_HEADS):
        col = jnp.zeros((Q_PER_KV * WINDOW, 1), F32)
        for g in range(Q_PER_KV):
            col = jnp.where(row_g == g, sinks_ref[Q_PER_KV * h + g], col)
        sink_cols.append(col)

    def block(r, carry):
        n = j * blocks + r
        prev = pl.multiple_of(jnp.maximum(n - 1, 0) * WINDOW, WINDOW)
        cur = pl.multiple_of(n * WINDOW, WINDOW)
        no_prev = jnp.where(n > 0, 0.0, -jnp.inf)
        q = q_ref[pl.ds(pl.multiple_of(r * WINDOW, WINDOW), WINDOW), :]
        kb = jnp.concatenate([k_ref[pl.ds(prev, WINDOW), :], k_ref[pl.ds(cur, WINDOW), :]], axis=0)
        vb = jnp.concatenate([v_ref[pl.ds(prev, WINDOW), :], v_ref[pl.ds(cur, WINDOW), :]], axis=0)
        groups = None
        for h in range(N_KV_HEADS):
            sel = quarter == h
            lhs = jnp.concatenate(
                [jnp.where(sel, q[:, KV_WIDTH * g:KV_WIDTH * (g + 1)], 0) for g in range(Q_PER_KV)],
                axis=0)
            s = lax.dot_general(lhs, kb, (((1,), (1,)), ((), ())), preferred_element_type=F32)
            s = jnp.where(from_prev, s[:, :WINDOW] + no_prev, s[:, WINDOW:])
            sink = sink_cols[h]
            m = jnp.maximum(jnp.max(s, axis=-1, keepdims=True), sink)
            p = jnp.exp(s - m)
            inv = 1.0 / (jnp.sum(p, axis=-1, keepdims=True) + jnp.exp(sink - m))
            p_band = jnp.concatenate(
                [jnp.where(from_prev, p, 0.0), jnp.where(from_prev, 0.0, p)], axis=1).astype(BF16)
            o = jnp.dot(p_band, vb, preferred_element_type=F32) * inv
            parts = [o[WINDOW * g:WINDOW * (g + 1)] for g in range(Q_PER_KV)]
            groups = parts if groups is None else [
                jnp.where(sel, part, acc) for part, acc in zip(parts, groups)]
        a_scr[pl.ds(pl.multiple_of(r * WINDOW, WINDOW), WINDOW), :] = jnp.concatenate(groups, axis=1)
        return carry

    lax.fori_loop(0, blocks, block, 0, unroll=2)

    sgo = _spatial_gate(u_ref[...], g_ref[...], sgw_ref, bias_ref, CHUNK)
    mix_ref[...] = _normed_mix(a_scr[...], sgo, aon_ref[...], son_ref[...])


def _prompt_mix(sinks, q, k, v, u, g, sg_w, bias, aon, son, wd, wo, batch, seq):
    tm = ROW_TILE
    tiles = seq // tm
    steps = batch * tiles
    d_model = wo.shape[1]
    assert wo.shape[0] == 2 * HEAD_DIM * steps
    row = lambda b, j: (b * tiles + j, 0)
    per_seq = lambda b, j: (b, 0)
    const2 = lambda b, j: (0, 0)
    const3 = lambda b, j: (0, 0, 0)
    slab = lambda w: pl.BlockSpec((w.shape[0] // steps, w.shape[1]), row)
    bf16_like = lambda w: jax.ShapeDtypeStruct(w.shape, BF16)

    def wo_block(half):
        def index(b, j):
            c = 2 * (b * tiles + j) + half
            return jnp.where(c < N_Q_HEADS, _swapped_head(c), c), 0
        return pl.BlockSpec((HEAD_DIM, d_model), index)

    return pl.pallas_call(
        _prompt_mix_body,
        grid=(batch, tiles),
        in_specs=[
            pl.BlockSpec(memory_space=pltpu.SMEM),
            pl.BlockSpec((tm, ATTN_WIDTH), row),
            pl.BlockSpec((seq, KV_WIDTH), per_seq),
            pl.BlockSpec((seq, KV_WIDTH), per_seq),
            pl.BlockSpec((tm, SG_WIDTH), row),
            pl.BlockSpec((tm, SG_WIDTH), row),
            pl.BlockSpec(sg_w.shape, const3),
            pl.BlockSpec(bias.shape, const2),
            pl.BlockSpec((1, ATTN_WIDTH), const2),
            pl.BlockSpec((1, SG_WIDTH), const2),
            slab(wd), wo_block(0), wo_block(1),
        ],
        out_specs=[pl.BlockSpec((tm, d_model), row), slab(wd), slab(wo)],
        out_shape=[jax.ShapeDtypeStruct((batch * seq, d_model), BF16), bf16_like(wd), bf16_like(wo)],
        scratch_shapes=[pltpu.VMEM((tm, ATTN_WIDTH), F32)],
        compiler_params=pltpu.CompilerParams(
            dimension_semantics=("arbitrary", "arbitrary"), vmem_limit_bytes=VMEM_LIMIT_BYTES),
        name="prompt_mix",
    )(sinks, q, k, v, u, g, sg_w, bias, aon, son, wd, wo, wo)


def _sample_mix_body(sinks_ref, q_ref, k_ref, v_ref, ck_ref, cv_ref, u_ref, g_ref, sgw_ref,
                     bias_ref, aon_ref, son_ref, mix_ref, knt_ref, vnt_ref, a_scr, *, dec_seq):
    rows = N_Q_HEADS * dec_seq
    quarter = lax.broadcasted_iota(jnp.int32, (dec_seq, KV_WIDTH), 1) // HEAD_DIM
    t = lax.broadcasted_iota(jnp.int32, (rows, 2 * WINDOW), 0) % dec_seq
    kj = lax.broadcasted_iota(jnp.int32, (rows, 2 * WINDOW), 1)
    row_head = lax.broadcasted_iota(jnp.int32, (rows, 1), 0) // dec_seq
    sink = jnp.zeros((rows, 1), F32)
    for a in range(N_Q_HEADS):
        sink = jnp.where(row_head == a, sinks_ref[a], sink)
    k_new_t = k_ref[...].T
    v_new_t = v_ref[...].T
    knt_ref[0] = k_new_t
    vnt_ref[0] = v_new_t

    def one_seq(b, carry):
        r0 = pl.multiple_of(b * dec_seq, dec_seq)
        q = q_ref[pl.ds(r0, dec_seq), :]
        ck = ck_ref[b]
        cv = cv_ref[b]
        own = kj - WINDOW - b * dec_seq
        mask = ((kj < WINDOW) & (kj > t)) | ((own >= 0) & (own <= t))
        lhs = jnp.concatenate(
            [jnp.where(quarter == h, q[:, KV_WIDTH * g:KV_WIDTH * (g + 1)], 0.0)
             for h in range(N_KV_HEADS) for g in range(Q_PER_KV)], axis=0).astype(BF16)
        k_all = jnp.concatenate([ck, k_new_t], axis=1).astype(BF16)
        v_all = jnp.concatenate([cv, v_new_t], axis=1).astype(BF16)
        s = jnp.dot(lhs, k_all, preferred_element_type=F32)
        p, inv = _softmax_sink(s, mask, sink)
        o = lax.dot_general(p.astype(BF16), v_all, (((1,), (1,)), ((), ())),
                            preferred_element_type=F32) * inv
        groups = []
        for g in range(Q_PER_KV):
            piece = lambda h: o[(Q_PER_KV * h + g) * dec_seq:(Q_PER_KV * h + g + 1) * dec_seq]
            acc = piece(N_KV_HEADS - 1)
            for h in range(N_KV_HEADS - 2, -1, -1):
                acc = jnp.where(quarter == h, piece(h), acc)
            groups.append(acc)
        a_scr[pl.ds(r0, dec_seq), :] = jnp.concatenate(groups, axis=1)
        return carry

    lax.fori_loop(0, ck_ref.shape[0], one_seq, 0, unroll=4)

    sgo = _spatial_gate(u_ref[...], g_ref[...], sgw_ref, bias_ref, dec_seq)
    mix_ref[...] = _normed_mix(a_scr[...], sgo, aon_ref[...], son_ref[...])


def _sample_mix(sinks, q, k, v, ck, cv, u, g, sg_w, bias, aon, son, dec_seq):
    nseq = ck.shape[0]
    sb = SEQS_PER_STEP
    tm = sb * dec_seq
    assert tm == WINDOW == CHUNK and dec_seq % 8 == 0
    row = lambda i: (i, 0)
    seq3 = lambda i: (i, 0, 0)
    const2 = lambda i: (0, 0)
    const3 = lambda i: (0, 0, 0)
    return pl.pallas_call(
        functools.partial(_sample_mix_body, dec_seq=dec_seq),
        grid=(nseq // sb,),
        in_specs=[
            pl.BlockSpec(memory_space=pltpu.SMEM),
            pl.BlockSpec((tm, ATTN_WIDTH), row),
            pl.BlockSpec((tm, KV_WIDTH), row),
            pl.BlockSpec((tm, KV_WIDTH), row),
            pl.BlockSpec((sb, KV_WIDTH, WINDOW), seq3),
            pl.BlockSpec((sb, KV_WIDTH, WINDOW), seq3),
            pl.BlockSpec((tm, SG_WIDTH), row),
            pl.BlockSpec((tm, SG_WIDTH), row),
            pl.BlockSpec(sg_w.shape, const3),
            pl.BlockSpec(bias.shape, const2),
            pl.BlockSpec((1, ATTN_WIDTH), const2),
            pl.BlockSpec((1, SG_WIDTH), const2),
        ],
        out_specs=[
            pl.BlockSpec((tm, ATTN_WIDTH + SG_WIDTH), row),
            pl.BlockSpec((1, KV_WIDTH, tm), seq3),
            pl.BlockSpec((1, KV_WIDTH, tm), seq3),
        ],
        out_shape=[
            jax.ShapeDtypeStruct((nseq * dec_seq, ATTN_WIDTH + SG_WIDTH), BF16),
            jax.ShapeDtypeStruct((nseq // sb, KV_WIDTH, tm), F32),
            jax.ShapeDtypeStruct((nseq // sb, KV_WIDTH, tm), F32),
        ],
        scratch_shapes=[pltpu.VMEM((tm, ATTN_WIDTH), F32)],
        compiler_params=pltpu.CompilerParams(
            dimension_semantics=("arbitrary",), vmem_limit_bytes=VMEM_LIMIT_BYTES),
        name="sample_mix",
    )(sinks, q, k, v, ck, cv, u, g, sg_w, bias, aon, son)


def _tail_body(x_ref, mix_ref, wo_ref, fn_ref, wg_ref, wu_ref, wd_ref, *rest, window_seqs, dec_seq):
    if window_seqs:
        ck_ref, cv_ref, knt_ref, vnt_ref, y_ref, kw_ref, vw_ref, hn_scr = rest
    else:
        y_ref, hn_scr = rest

    def update_windows():
        per_tile = ck_ref.shape[0]
        local = jnp.minimum(pl.program_id(1), per_tile - 1)
        in_group = (pl.program_id(0) * per_tile + local) % SEQS_PER_STEP
        shift = WINDOW - dec_seq - in_group * dec_seq
        keep = lax.broadcasted_iota(jnp.int32, (KV_WIDTH, WINDOW), 1) < WINDOW - dec_seq
        kw_ref[local] = jnp.where(keep, pltpu.roll(ck_ref[local], WINDOW - dec_seq, 1),
                                  pltpu.roll(knt_ref[0], shift, 1))
        vw_ref[local] = jnp.where(keep, pltpu.roll(cv_ref[local], WINDOW - dec_seq, 1),
                                  pltpu.roll(vnt_ref[0], shift, 1))

    def ffn_chunk():
        if window_seqs:
            update_windows()
        hn = hn_scr[...]
        gate = jnp.dot(hn, wg_ref[...], preferred_element_type=F32)
        up = jnp.dot(hn, wu_ref[...], preferred_element_type=F32)
        act = (jax.nn.silu(gate) * up).astype(BF16)
        y_ref[...] += jnp.dot(act, wd_ref[...], preferred_element_type=F32)

    @pl.when(pl.program_id(1) == 0)
    def _():
        half = x_ref.shape[0] // 2
        for rows in (slice(0, half), slice(half, 2 * half)):
            y_ref[rows, :] = x_ref[rows, :] + jnp.dot(
                mix_ref[rows, :], wo_ref[...], preferred_element_type=F32)
        for rows in (slice(0, half), slice(half, 2 * half)):
            hn_scr[rows, :] = _rms(y_ref[rows, :], fn_ref[...]).astype(BF16)
        ffn_chunk()

    @pl.when(pl.program_id(1) > 0)
    def _():
        ffn_chunk()


def _tail(x, mix, wo, ffn_norm, wg, wu, wd, windows=None):
    rows, d_model = x.shape
    d_ff = wg.shape[1]
    tm = min(ROW_TILE, rows)
    tf = FF_TILE
    chunks = d_ff // tf
    in_specs = [
        pl.BlockSpec((tm, d_model), lambda i, j: (i, 0)),
        pl.BlockSpec((tm, mix.shape[1]), lambda i, j: (i, 0)),
        pl.BlockSpec(wo.shape, lambda i, j: (0, 0), pipeline_mode=pl.Buffered(1)),
        pl.BlockSpec((1, d_model), lambda i, j: (0, 0)),
        pl.BlockSpec((d_model, tf), lambda i, j: (0, j)),
        pl.BlockSpec((d_model, tf), lambda i, j: (0, j)),
        pl.BlockSpec((tf, d_model), lambda i, j: (j, 0)),
    ]
    out_specs = [pl.BlockSpec((tm, d_model), lambda i, j: (i, 0))]
    out_shape = [jax.ShapeDtypeStruct(x.shape, F32)]
    operands = [x, mix, wo, ffn_norm, wg, wu, wd]
    window_seqs, dec_seq = 0, 0
    if windows is not None:
        ck, cv, knt, vnt, dec_seq = windows
        window_seqs = ck.shape[0]
        per_tile = window_seqs // (rows // tm)
        assert per_tile * (rows // tm) == window_seqs and per_tile <= chunks
        assert SEQS_PER_STEP % per_tile == 0
        tile_seqs = pl.BlockSpec((per_tile, KV_WIDTH, WINDOW), lambda i, j: (i, 0, 0))
        group = lambda i, j: (i * per_tile // SEQS_PER_STEP, 0, 0)
        in_specs += [tile_seqs, tile_seqs, pl.BlockSpec((1,) + knt.shape[1:], group),
                     pl.BlockSpec((1,) + vnt.shape[1:], group)]
        out_specs += [tile_seqs, tile_seqs]
        out_shape += [jax.ShapeDtypeStruct(ck.shape, F32), jax.ShapeDtypeStruct(cv.shape, F32)]
        operands += [ck, cv, knt, vnt]
    return pl.pallas_call(
        functools.partial(_tail_body, window_seqs=window_seqs, dec_seq=dec_seq),
        grid=(rows // tm, chunks),
        in_specs=in_specs,
        out_specs=out_specs,
        out_shape=out_shape,
        scratch_shapes=[pltpu.VMEM((tm, d_model), BF16)],
        compiler_params=pltpu.CompilerParams(
            dimension_semantics=("arbitrary", "arbitrary"), vmem_limit_bytes=VMEM_LIMIT_BYTES),
        name="tail",
    )(*operands)


def _rope_tables(pos):
    half = HEAD_DIM // 2
    lane = jnp.arange(LANES)
    inv = ROPE_THETA ** (-(lane % half).astype(F32) / half)
    ang = pos.astype(F32)[:, None] * inv[None, :]
    sign = jnp.where(lane % HEAD_DIM < half, -1.0, 1.0).astype(F32)
    return jnp.cos(ang), jnp.sin(ang) * sign[None, :]


def _windows_to_native(w):
    return jnp.transpose(w, (0, 2, 3, 1)).reshape(w.shape[0], KV_WIDTH, WINDOW)


def _windows_from_native(w):
    return jnp.transpose(w.reshape(w.shape[0], N_KV_HEADS, HEAD_DIM, WINDOW), (0, 3, 1, 2))[None]


def kernel(x_prompt, x_sample, cache_k_win, cache_v_win, attn_norm, w_in, q_norm, k_norm, sinks,
           sg_norm, sg_w, sg_b, attn_out_norm, sg_out_norm, w_o, ffn_norm, w_gate, w_up, w_down):
    assert w_in.shape[0] == 1, "single-layer step only"
    batch, seq, d_model = x_prompt.shape
    dec_batch, dec_seq, _ = x_sample.shape

    w_in_b = w_in[0].astype(BF16)
    sink_vec = sinks[0]
    sgw = sg_w[0][:, :CHUNK, :CHUNK]
    sgb = sg_b[0][:, :CHUNK]

    cos_p, sin_p = _rope_tables(jnp.arange(seq, dtype=jnp.int32))
    tm_s = min(ROW_TILE, dec_batch * dec_seq)
    cos_s, sin_s = _rope_tables(PAST_LEN + jnp.arange(tm_s, dtype=jnp.int32) % dec_seq)

    xp = x_prompt.reshape(batch * seq, d_model)
    q, k, v, u, g, wg, wu, kwin, vwin = _inproj(xp, attn_norm, w_in_b, q_norm, k_norm, sg_norm,
                                                cos_p, sin_p, BF16, seq,
                                                casts=(w_gate[0], w_up[0]))
    mix_p, wd, wo = _prompt_mix(sink_vec, q, k, v, u, g, sgw, sgb, attn_out_norm, sg_out_norm,
                                w_down[0], w_o[0], batch, seq)
    k_win_prompt = _windows_from_native(kwin)
    v_win_prompt = _windows_from_native(vwin)

    xs = x_sample.reshape(dec_batch * dec_seq, d_model)
    q, k, v, u, g = _inproj(xs, attn_norm, w_in_b, q_norm, k_norm, sg_norm, cos_s, sin_s, F32, None)
    ck = _windows_to_native(cache_k_win[0])
    cv = _windows_to_native(cache_v_win[0])
    mix_s, knt, vnt = _sample_mix(sink_vec, q, k, v, ck, cv, u, g, sgw, sgb,
                                  attn_out_norm, sg_out_norm, dec_seq)

    y_prompt, kw, vw = _tail(xp, mix_p, wo, ffn_norm, wg, wu, wd, windows=(ck, cv, knt, vnt, dec_seq))
    y_prompt = y_prompt.reshape(x_prompt.shape)
    (y_sample,) = _tail(xs, mix_s, wo, ffn_norm, wg, wu, wd)
    y_sample = y_sample.reshape(x_sample.shape)
    k_win_sample = _windows_from_native(kw)
    v_win_sample = _windows_from_native(vw)
    sg_v_sample = g.reshape(1, dec_batch, dec_seq, SG_WIDTH)

    return (y_prompt, y_sample, k_win_prompt, v_win_prompt, k_win_sample, v_win_sample, sg_v_sample)
```

```python
import functools

import jax
import jax.numpy as jnp
from jax import lax
from jax.experimental import pallas as pl
from jax.experimental.pallas import tpu as pltpu

HEAD_DIM = 64
N_KV_HEADS = 4
Q_PER_KV = 4
N_Q_HEADS = N_KV_HEADS * Q_PER_KV
KV_WIDTH = N_KV_HEADS * HEAD_DIM
ATTN_WIDTH = N_Q_HEADS * HEAD_DIM
WINDOW = 128
N_SG_HEADS = 8
SG_HEAD_DIM = 128
SG_WIDTH = N_SG_HEADS * SG_HEAD_DIM
CHUNK = 128
PAST_LEN = 16384
ROPE_THETA = 10000.0
EPS = 1e-6

LANES = 128
VMEM_LIMIT_BYTES = 56 * 1024 * 1024
TAIL_VMEM_LIMIT_BYTES = 61 * 1024 * 1024

ROW_TILE = 512
INPROJ_SUB_BLOCKS = 4
FF_TILE = 512
SEQS_PER_STEP = 16

F32 = jnp.float32
BF16 = jnp.bfloat16


def _rms(x, gain_row):
    r = lax.rsqrt(jnp.mean(x * x, axis=-1, keepdims=True) + EPS)
    return x * r * gain_row


def _swapped_head(c):
    return Q_PER_KV * (c % N_KV_HEADS) + c // N_KV_HEADS


def _swap_heads(t):
    lane = lax.broadcasted_iota(jnp.int32, (t.shape[0], LANES), 1)
    lo = lane < HEAD_DIM
    outs = []
    for j in range(ATTN_WIDTH // LANES):
        halves = []
        for half in range(2):
            a = _swapped_head(2 * j + half)
            src = t[:, LANES * (a // 2):LANES * (a // 2 + 1)]
            halves.append(src if a % 2 == half else pltpu.roll(src, HEAD_DIM, 1))
        outs.append(jnp.where(lo, halves[0], halves[1]))
    return jnp.concatenate(outs, axis=1)


def _head_norm_rope(t, gain_row, cos, sin_signed):
    lane = lax.broadcasted_iota(jnp.int32, (t.shape[0], LANES), 1)
    lo_head = lane < HEAD_DIM
    first_half = (lane % HEAD_DIM) < (HEAD_DIM // 2)
    outs = []
    for j in range(t.shape[1] // LANES):
        blk = t[:, LANES * j:LANES * (j + 1)]
        sq = blk * blk
        s_all = jnp.sum(sq, axis=-1, keepdims=True)
        s_lo = jnp.sum(jnp.where(lo_head, sq, 0.0), axis=-1, keepdims=True)
        s_hi = s_all - s_lo
        r = jnp.where(lo_head,
                      lax.rsqrt(s_lo * (1.0 / HEAD_DIM) + EPS),
                      lax.rsqrt(s_hi * (1.0 / HEAD_DIM) + EPS))
        y = blk * r * gain_row
        rot = jnp.where(first_half,
                        pltpu.roll(y, LANES - HEAD_DIM // 2, 1),
                        pltpu.roll(y, HEAD_DIM // 2, 1))
        outs.append(y * cos + rot * sin_signed)
    return jnp.concatenate(outs, axis=1)


def _inproj_body(x_ref, an_ref, w_ref, qn_ref, kn_ref, sgn_ref, cos_ref, sin_ref, *rest,
                 tiles_per_seq, n_cast):
    casts_in, rest = rest[:n_cast], rest[n_cast:]
    q_ref, k_ref, v_ref, u_ref, g_ref = rest[:5]
    casts_out, rest = rest[5:5 + n_cast], rest[5 + n_cast:]
    proj_scr = rest[-1]
    n_sub, sub = proj_scr.shape[0], proj_scr.shape[1]
    q_gain = jnp.concatenate([qn_ref[...]] * (LANES // HEAD_DIM), axis=1)
    k_gain = jnp.concatenate([kn_ref[...]] * (LANES // HEAD_DIM), axis=1)
    c0, c1, c2, c3 = ATTN_WIDTH, ATTN_WIDTH + KV_WIDTH, ATTN_WIDTH + 2 * KV_WIDTH, \
        ATTN_WIDTH + 2 * KV_WIDTH + SG_WIDTH

    def project(r):
        xn = _rms(x_ref[sub * r:sub * (r + 1), :], an_ref[...]).astype(BF16)
        proj_scr[r] = jnp.dot(xn, w_ref[...], preferred_element_type=F32)

    def finish(r):
        rows = slice(sub * r, sub * (r + 1))
        cos = cos_ref[rows, :]
        sin = sin_ref[rows, :]
        q = _head_norm_rope(proj_scr[r, :, 0:c0], q_gain, cos, sin) * (HEAD_DIM ** -0.5)
        q_ref[rows, :] = _swap_heads(q).astype(q_ref.dtype)
        k = _head_norm_rope(proj_scr[r, :, c0:c1], k_gain, cos, sin)
        k_ref[rows, :] = k.astype(k_ref.dtype)
        v = proj_scr[r, :, c1:c2]
        v_ref[rows, :] = v.astype(v_ref.dtype)
        u_ref[rows, :] = jax.nn.gelu(proj_scr[r, :, c2:c3]).astype(u_ref.dtype)
        g = jax.nn.gelu(proj_scr[r, :, c3:])
        g_ref[rows, :] = _rms(g, sgn_ref[...]).astype(g_ref.dtype)
        return k, v

    project(0)
    for src, dst in zip(casts_in, casts_out):
        dst[...] = src[...].astype(BF16)
    for r in range(1, n_sub):
        project(r)
        finish(r - 1)
    k, v = finish(n_sub - 1)

    if len(rest) > 1:
        kwin_ref, vwin_ref = rest[0], rest[1]

        @pl.when(pl.program_id(0) % tiles_per_seq == tiles_per_seq - 1)
        def _():
            kwin_ref[0] = k[sub - WINDOW:].T
            vwin_ref[0] = v[sub - WINDOW:].T


def _inproj(x, attn_norm, w_in, q_norm, k_norm, sg_norm, cos_t, sin_t, act_dtype, seq, casts=()):
    rows, d_model = x.shape
    tm = min(ROW_TILE, rows)
    steps = rows // tm
    n_tab = cos_t.shape[0] // tm
    const = lambda i: (0, 0)
    row = lambda i: (i, 0)
    tab = lambda i: (i % n_tab, 0)
    slabs = [pl.BlockSpec((w.shape[0] // steps, w.shape[1]), row) for w in casts]
    out_specs = [
        pl.BlockSpec((tm, ATTN_WIDTH), row),
        pl.BlockSpec((tm, KV_WIDTH), row),
        pl.BlockSpec((tm, KV_WIDTH), row),
        pl.BlockSpec((tm, SG_WIDTH), row),
        pl.BlockSpec((tm, SG_WIDTH), row),
    ]
    out_shape = [
        jax.ShapeDtypeStruct((rows, ATTN_WIDTH), act_dtype),
        jax.ShapeDtypeStruct((rows, KV_WIDTH), act_dtype),
        jax.ShapeDtypeStruct((rows, KV_WIDTH), act_dtype),
        jax.ShapeDtypeStruct((rows, SG_WIDTH), BF16),
        jax.ShapeDtypeStruct((rows, SG_WIDTH), act_dtype),
    ]
    out_specs += slabs
    out_shape += [jax.ShapeDtypeStruct(w.shape, BF16) for w in casts]
    tiles_per_seq = 1
    if seq is not None:
        tiles_per_seq = seq // tm
        win = lambda i: (i // tiles_per_seq, 0, 0)
        out_specs += [pl.BlockSpec((1, KV_WIDTH, WINDOW), win)] * 2
        out_shape += [jax.ShapeDtypeStruct((rows // seq, KV_WIDTH, WINDOW), F32)] * 2
    return pl.pallas_call(
        functools.partial(_inproj_body, tiles_per_seq=tiles_per_seq, n_cast=len(casts)),
        grid=(steps,),
        in_specs=[
            pl.BlockSpec((tm, d_model), row),
            pl.BlockSpec((1, d_model), const),
            pl.BlockSpec(w_in.shape, const, pipeline_mode=pl.Buffered(1)),
            pl.BlockSpec((1, HEAD_DIM), const),
            pl.BlockSpec((1, HEAD_DIM), const),
            pl.BlockSpec((1, SG_WIDTH), const),
            pl.BlockSpec((tm, LANES), tab),
            pl.BlockSpec((tm, LANES), tab),
        ] + slabs,
        out_specs=out_specs,
        out_shape=out_shape,
        scratch_shapes=[pltpu.VMEM((INPROJ_SUB_BLOCKS, tm // INPROJ_SUB_BLOCKS, w_in.shape[1]), F32)],
        compiler_params=pltpu.CompilerParams(
            dimension_semantics=("arbitrary",), vmem_limit_bytes=VMEM_LIMIT_BYTES),
        name="inproj",
    )(x, attn_norm, w_in, q_norm, k_norm, sg_norm, cos_t, sin_t, *casts)


def _softmax_sink(s, mask, sink):
    s = jnp.where(mask, s, -jnp.inf)
    m = jnp.maximum(jnp.max(s, axis=-1, keepdims=True), sink)
    p = jnp.exp(s - m)
    denom = jnp.sum(p, axis=-1, keepdims=True) + jnp.exp(sink - m)
    return p, 1.0 / denom


def _spatial_gate(u, g, w_ref, b_ref, rows_per_seq):
    r = lax.broadcasted_iota(jnp.int32, (CHUNK, CHUNK), 0)
    c = lax.broadcasted_iota(jnp.int32, (CHUNK, CHUNK), 1)
    causal = (r // rows_per_seq == c // rows_per_seq) & (c % rows_per_seq <= r % rows_per_seq)
    gb = g.astype(BF16)
    b_t = b_ref[...].T[0:rows_per_seq, :]
    b_t = jnp.concatenate([b_t] * (CHUNK // rows_per_seq), axis=0)
    periodic = ((r < rows_per_seq) & (c % rows_per_seq == r)).astype(BF16)
    outs = []
    for h in range(N_SG_HEADS):
        if rows_per_seq == CHUNK:
            w = w_ref[h]
        else:
            w = jnp.dot(w_ref[h, 0:rows_per_seq, :].astype(BF16), periodic, preferred_element_type=F32)
            w = jnp.concatenate([w] * (CHUNK // rows_per_seq), axis=0)
        w = jnp.where(causal, w, 0.0).astype(BF16)
        cols = slice(SG_HEAD_DIM * h, SG_HEAD_DIM * (h + 1))
        bias = jnp.broadcast_to(b_t[:, h:h + 1], (CHUNK, SG_HEAD_DIM))
        parts = []
        for ch in range(u.shape[0] // CHUNK):
            rws = slice(CHUNK * ch, CHUNK * (ch + 1))
            mixed = jnp.dot(w, gb[rws, cols], preferred_element_type=F32) + bias
            parts.append(u[rws, cols].astype(F32) * mixed)
        outs.append(jnp.concatenate(parts, axis=0))
    return jnp.concatenate(outs, axis=1)


def _normed_mix(a_swapped, sgo, aon, son):
    return jnp.concatenate([_rms(a_swapped, _swap_heads(aon)), _rms(sgo, son)], axis=1).astype(BF16)


def _prompt_mix_body(sinks_ref, q_ref, k_ref, v_ref, u_ref, g_ref, sgw_ref, bias_ref,
                     aon_ref, son_ref, wd32, wo32_lo, wo32_hi, mix_ref, wd16, wo16, a_scr):
    wd16[...] = wd32[...].astype(BF16)
    wo16[0:HEAD_DIM, :] = wo32_lo[...].astype(BF16)
    wo16[HEAD_DIM:2 * HEAD_DIM, :] = wo32_hi[...].astype(BF16)

    j = pl.program_id(1)
    blocks = q_ref.shape[0] // WINDOW
    quarter = lax.broadcasted_iota(jnp.int32, (WINDOW, KV_WIDTH), 1) // HEAD_DIM
    qi = lax.broadcasted_iota(jnp.int32, (Q_PER_KV * WINDOW, WINDOW), 0) % WINDOW
    kj = lax.broadcasted_iota(jnp.int32, (Q_PER_KV * WINDOW, WINDOW), 1)
    from_prev = kj > qi
    row_g = lax.broadcasted_iota(jnp.int32, (Q_PER_KV * WINDOW, 1), 0) // WINDOW
    sink_cols = []
    for h in range(N_KV_HEADS):
        col = jnp.zeros((Q_PER_KV * WINDOW, 1), F32)
        for g in range(Q_PER_KV):
            col = jnp.where(row_g == g, sinks_ref[Q_PER_KV * h + g], col)
        sink_cols.append(col)

    def block(r, carry):
        n = j * blocks + r
        prev = pl.multiple_of(jnp.maximum(n - 1, 0) * WINDOW, WINDOW)
        cur = pl.multiple_of(n * WINDOW, WINDOW)
        no_prev = jnp.where(n > 0, 0.0, -jnp.inf)
        q = q_ref[pl.ds(pl.multiple_of(r * WINDOW, WINDOW), WINDOW), :]
        kb = jnp.concatenate([k_ref[pl.ds(prev, WINDOW), :], k_ref[pl.ds(cur, WINDOW), :]], axis=0)
        vb = jnp.concatenate([v_ref[pl.ds(prev, WINDOW), :], v_ref[pl.ds(cur, WINDOW), :]], axis=0)
        groups = None
        for h in range(N_KV_HEADS):
            sel = quarter == h
            lhs = jnp.concatenate(
                [jnp.where(sel, q[:, KV_WIDTH * g:KV_WIDTH * (g + 1)], 0) for g in range(Q_PER_KV)],
                axis=0)
            s = lax.dot_general(lhs, kb, (((1,), (1,)), ((), ())), preferred_element_type=F32)
            s = jnp.where(from_prev, s[:, :WINDOW] + no_prev, s[:, WINDOW:])
            sink = sink_cols[h]
            m = jnp.maximum(jnp.max(s, axis=-1, keepdims=True), sink)
            p = jnp.exp(s - m)
            inv = 1.0 / (jnp.sum(p, axis=-1, keepdims=True) + jnp.exp(sink - m))
            p_band = jnp.concatenate(
                [jnp.where(from_prev, p, 0.0), jnp.where(from_prev, 0.0, p)], axis=1).astype(BF16)
            o = jnp.dot(p_band, vb, preferred_element_type=F32) * inv
            parts = [o[WINDOW * g:WINDOW * (g + 1)] for g in range(Q_PER_KV)]
            groups = parts if groups is None else [
                jnp.where(sel, part, acc) for part, acc in zip(parts, groups)]
        a_scr[pl.ds(pl.multiple_of(r * WINDOW, WINDOW), WINDOW), :] = jnp.concatenate(groups, axis=1)
        return carry

    lax.fori_loop(0, blocks, block, 0, unroll=2)

    sgo = _spatial_gate(u_ref[...], g_ref[...], sgw_ref, bias_ref, CHUNK)
    mix_ref[...] = _normed_mix(a_scr[...], sgo, aon_ref[...], son_ref[...])


def _prompt_mix(sinks, q, k, v, u, g, sg_w, bias, aon, son, wd, wo, batch, seq):
    tm = ROW_TILE
    tiles = seq // tm
    steps = batch * tiles
    d_model = wo.shape[1]
    assert wo.shape[0] == 2 * HEAD_DIM * steps
    row = lambda b, j: (b * tiles + j, 0)
    per_seq = lambda b, j: (b, 0)
    const2 = lambda b, j: (0, 0)
    const3 = lambda b, j: (0, 0, 0)
    slab = lambda w: pl.BlockSpec((w.shape[0] // steps, w.shape[1]), row)
    bf16_like = lambda w: jax.ShapeDtypeStruct(w.shape, BF16)

    def wo_block(half):
        def index(b, j):
            c = 2 * (b * tiles + j) + half
            return jnp.where(c < N_Q_HEADS, _swapped_head(c), c), 0
        return pl.BlockSpec((HEAD_DIM, d_model), index)

    return pl.pallas_call(
        _prompt_mix_body,
        grid=(batch, tiles),
        in_specs=[
            pl.BlockSpec(memory_space=pltpu.SMEM),
            pl.BlockSpec((tm, ATTN_WIDTH), row),
            pl.BlockSpec((seq, KV_WIDTH), per_seq),
            pl.BlockSpec((seq, KV_WIDTH), per_seq),
            pl.BlockSpec((tm, SG_WIDTH), row),
            pl.BlockSpec((tm, SG_WIDTH), row),
            pl.BlockSpec(sg_w.shape, const3),
            pl.BlockSpec(bias.shape, const2),
            pl.BlockSpec((1, ATTN_WIDTH), const2),
            pl.BlockSpec((1, SG_WIDTH), const2),
            slab(wd), wo_block(0), wo_block(1),
        ],
        out_specs=[pl.BlockSpec((tm, d_model), row), slab(wd), slab(wo)],
        out_shape=[jax.ShapeDtypeStruct((batch * seq, d_model), BF16), bf16_like(wd), bf16_like(wo)],
        scratch_shapes=[pltpu.VMEM((tm, ATTN_WIDTH), F32)],
        compiler_params=pltpu.CompilerParams(
            dimension_semantics=("arbitrary", "arbitrary"), vmem_limit_bytes=VMEM_LIMIT_BYTES),
        name="prompt_mix",
    )(sinks, q, k, v, u, g, sg_w, bias, aon, son, wd, wo, wo)


def _sample_mix_body(sinks_ref, q_ref, k_ref, v_ref, ck_ref, cv_ref, u_ref, g_ref, sgw_ref,
                     bias_ref, aon_ref, son_ref, mix_ref, kw_ref, vw_ref, a_scr, *, dec_seq):
    rows = N_Q_HEADS * dec_seq
    quarter = lax.broadcasted_iota(jnp.int32, (dec_seq, KV_WIDTH), 1) // HEAD_DIM
    t = lax.broadcasted_iota(jnp.int32, (rows, 2 * WINDOW), 0) % dec_seq
    kj = lax.broadcasted_iota(jnp.int32, (rows, 2 * WINDOW), 1)
    pos_lane = lax.broadcasted_iota(jnp.int32, (KV_WIDTH, WINDOW), 1)
    row_head = lax.broadcasted_iota(jnp.int32, (rows, 1), 0) // dec_seq
    sink = jnp.zeros((rows, 1), F32)
    for a in range(N_Q_HEADS):
        sink = jnp.where(row_head == a, sinks_ref[a], sink)
    k_new_t = k_ref[...].T
    v_new_t = v_ref[...].T

    def one_seq(b, carry):
        r0 = pl.multiple_of(b * dec_seq, dec_seq)
        q = q_ref[pl.ds(r0, dec_seq), :]
        ck = ck_ref[b]
        cv = cv_ref[b]
        own = kj - WINDOW - b * dec_seq
        mask = ((kj < WINDOW) & (kj > t)) | ((own >= 0) & (own <= t))
        lhs = jnp.concatenate(
            [jnp.where(quarter == h, q[:, KV_WIDTH * g:KV_WIDTH * (g + 1)], 0.0)
             for h in range(N_KV_HEADS) for g in range(Q_PER_KV)], axis=0).astype(BF16)
        k_all = jnp.concatenate([ck, k_new_t], axis=1).astype(BF16)
        v_all = jnp.concatenate([cv, v_new_t], axis=1).astype(BF16)
        s = jnp.dot(lhs, k_all, preferred_element_type=F32)
        p, inv = _softmax_sink(s, mask, sink)
        o = lax.dot_general(p.astype(BF16), v_all, (((1,), (1,)), ((), ())),
                            preferred_element_type=F32) * inv
        groups = []
        for g in range(Q_PER_KV):
            piece = lambda h: o[(Q_PER_KV * h + g) * dec_seq:(Q_PER_KV * h + g + 1) * dec_seq]
            acc = piece(N_KV_HEADS - 1)
            for h in range(N_KV_HEADS - 2, -1, -1):
                acc = jnp.where(quarter == h, piece(h), acc)
            groups.append(acc)
        a_scr[pl.ds(r0, dec_seq), :] = jnp.concatenate(groups, axis=1)
        return carry

    lax.fori_loop(0, ck_ref.shape[0], one_seq, 0, unroll=4)

    keep = pos_lane < WINDOW - dec_seq
    for b in range(ck_ref.shape[0]):
        shift = WINDOW - dec_seq - b * dec_seq
        kw_ref[b] = jnp.where(keep, pltpu.roll(ck_ref[b], WINDOW - dec_seq, 1),
                              pltpu.roll(k_new_t, shift, 1))
        vw_ref[b] = jnp.where(keep, pltpu.roll(cv_ref[b], WINDOW - dec_seq, 1),
                              pltpu.roll(v_new_t, shift, 1))

    sgo = _spatial_gate(u_ref[...], g_ref[...], sgw_ref, bias_ref, dec_seq)
    mix_ref[...] = _normed_mix(a_scr[...], sgo, aon_ref[...], son_ref[...])


def _sample_mix(sinks, q, k, v, ck, cv, u, g, sg_w, bias, aon, son, dec_seq):
    nseq = ck.shape[0]
    sb = SEQS_PER_STEP
    tm = sb * dec_seq
    assert tm == WINDOW == CHUNK and dec_seq % 8 == 0
    row = lambda i: (i, 0)
    seq3 = lambda i: (i, 0, 0)
    const2 = lambda i: (0, 0)
    const3 = lambda i: (0, 0, 0)
    return pl.pallas_call(
        functools.partial(_sample_mix_body, dec_seq=dec_seq),
        grid=(nseq // sb,),
        in_specs=[
            pl.BlockSpec(memory_space=pltpu.SMEM),
            pl.BlockSpec((tm, ATTN_WIDTH), row),
            pl.BlockSpec((tm, KV_WIDTH), row),
            pl.BlockSpec((tm, KV_WIDTH), row),
            pl.BlockSpec((sb, KV_WIDTH, WINDOW), seq3),
            pl.BlockSpec((sb, KV_WIDTH, WINDOW), seq3),
            pl.BlockSpec((tm, SG_WIDTH), row),
            pl.BlockSpec((tm, SG_WIDTH), row),
            pl.BlockSpec(sg_w.shape, const3),
            pl.BlockSpec(bias.shape, const2),
            pl.BlockSpec((1, ATTN_WIDTH), const2),
            pl.BlockSpec((1, SG_WIDTH), const2),
        ],
        out_specs=[
            pl.BlockSpec((tm, ATTN_WIDTH + SG_WIDTH), row),
            pl.BlockSpec((sb, KV_WIDTH, WINDOW), seq3),
            pl.BlockSpec((sb, KV_WIDTH, WINDOW), seq3),
        ],
        out_shape=[
            jax.ShapeDtypeStruct((nseq * dec_seq, ATTN_WIDTH + SG_WIDTH), BF16),
            jax.ShapeDtypeStruct(ck.shape, F32),
            jax.ShapeDtypeStruct(cv.shape, F32),
        ],
        scratch_shapes=[pltpu.VMEM((tm, ATTN_WIDTH), F32)],
        compiler_params=pltpu.CompilerParams(
            dimension_semantics=("arbitrary",), vmem_limit_bytes=VMEM_LIMIT_BYTES),
        name="sample_mix",
    )(sinks, q, k, v, ck, cv, u, g, sg_w, bias, aon, son)


def _tail_body(x_ref, mix_ref, wo_ref, fn_ref, wg_a, wu_a, wd_a, wg_b, wu_b, wd_b, y_ref, hn_scr,
               *, odd_chunks):
    def ffn_chunk(wg_ref, wu_ref, wd_ref):
        hn = hn_scr[...]
        gate = jnp.dot(hn, wg_ref[...], preferred_element_type=F32)
        up = jnp.dot(hn, wu_ref[...], preferred_element_type=F32)
        act = (jax.nn.silu(gate) * up).astype(BF16)
        y_ref[...] += jnp.dot(act, wd_ref[...], preferred_element_type=F32)

    j = pl.program_id(1)
    last = pl.num_programs(1) - 1

    @pl.when(j == 0)
    def _():
        half = x_ref.shape[0] // 2
        for rows in (slice(0, half), slice(half, 2 * half)):
            y_ref[rows, :] = x_ref[rows, :] + jnp.dot(
                mix_ref[rows, :], wo_ref[...], preferred_element_type=F32)
        for rows in (slice(0, half), slice(half, 2 * half)):
            hn_scr[rows, :] = _rms(y_ref[rows, :], fn_ref[...]).astype(BF16)
        ffn_chunk(wg_a, wu_a, wd_a)
        ffn_chunk(wg_b, wu_b, wd_b)

    @pl.when((j > 0) & ((j < last) | (not odd_chunks)))
    def _():
        ffn_chunk(wg_a, wu_a, wd_a)
        ffn_chunk(wg_b, wu_b, wd_b)

    if odd_chunks:
        @pl.when((j > 0) & (j == last))
        def _():
            ffn_chunk(wg_a, wu_a, wd_a)


def _tail(x, mix, wo, ffn_norm, wg, wu, wd):
    rows, d_model = x.shape
    d_ff = wg.shape[1]
    tm = min(ROW_TILE, rows)
    tf = FF_TILE
    chunks = d_ff // tf
    steps = (chunks + 1) // 2
    assert steps >= 2
    chunk_a = lambda j: 2 * j
    chunk_b = lambda j: jnp.minimum(2 * j + 1, chunks - 1)
    cols = lambda chunk: pl.BlockSpec((d_model, tf), lambda i, j: (0, chunk(j)))
    rows_of = lambda chunk: pl.BlockSpec((tf, d_model), lambda i, j: (chunk(j), 0))
    return pl.pallas_call(
        functools.partial(_tail_body, odd_chunks=chunks % 2 == 1),
        grid=(rows // tm, steps),
        in_specs=[
            pl.BlockSpec((tm, d_model), lambda i, j: (i, 0)),
            pl.BlockSpec((tm, mix.shape[1]), lambda i, j: (i, 0)),
            pl.BlockSpec(wo.shape, lambda i, j: (0, 0), pipeline_mode=pl.Buffered(1)),
            pl.BlockSpec((1, d_model), lambda i, j: (0, 0)),
            cols(chunk_a), cols(chunk_a), rows_of(chunk_a),
            cols(chunk_b), cols(chunk_b), rows_of(chunk_b),
        ],
        out_specs=pl.BlockSpec((tm, d_model), lambda i, j: (i, 0)),
        out_shape=jax.ShapeDtypeStruct(x.shape, F32),
        scratch_shapes=[pltpu.VMEM((tm, d_model), BF16)],
        compiler_params=pltpu.CompilerParams(
            dimension_semantics=("arbitrary", "arbitrary"), vmem_limit_bytes=TAIL_VMEM_LIMIT_BYTES),
        name="tail",
    )(x, mix, wo, ffn_norm, wg, wu, wd, wg, wu, wd)


def _rope_tables(pos):
    half = HEAD_DIM // 2
    lane = jnp.arange(LANES)
    inv = ROPE_THETA ** (-(lane % half).astype(F32) / half)
    ang = pos.astype(F32)[:, None] * inv[None, :]
    sign = jnp.where(lane % HEAD_DIM < half, -1.0, 1.0).astype(F32)
    return jnp.cos(ang), jnp.sin(ang) * sign[None, :]


def _windows_to_native(w):
    return jnp.transpose(w, (0, 2, 3, 1)).reshape(w.shape[0], KV_WIDTH, WINDOW)


def _windows_from_native(w):
    return jnp.transpose(w.reshape(w.shape[0], N_KV_HEADS, HEAD_DIM, WINDOW), (0, 3, 1, 2))[None]


def kernel(x_prompt, x_sample, cache_k_win, cache_v_win, attn_norm, w_in, q_norm, k_norm, sinks,
           sg_norm, sg_w, sg_b, attn_out_norm, sg_out_norm, w_o, ffn_norm, w_gate, w_up, w_down):
    assert w_in.shape[0] == 1, "single-layer step only"
    batch, seq, d_model = x_prompt.shape
    dec_batch, dec_seq, _ = x_sample.shape

    w_in_b = w_in[0].astype(BF16)
    sink_vec = sinks[0]
    sgw = sg_w[0][:, :CHUNK, :CHUNK]
    sgb = sg_b[0][:, :CHUNK]

    cos_p, sin_p = _rope_tables(jnp.arange(seq, dtype=jnp.int32))
    tm_s = min(ROW_TILE, dec_batch * dec_seq)
    cos_s, sin_s = _rope_tables(PAST_LEN + jnp.arange(tm_s, dtype=jnp.int32) % dec_seq)

    xp = x_prompt.reshape(batch * seq, d_model)
    q, k, v, u, g, wg, wu, kwin, vwin = _inproj(xp, attn_norm, w_in_b, q_norm, k_norm, sg_norm,
                                                cos_p, sin_p, BF16, seq,
                                                casts=(w_gate[0], w_up[0]))
    mix_p, wd, wo = _prompt_mix(sink_vec, q, k, v, u, g, sgw, sgb, attn_out_norm, sg_out_norm,
                                w_down[0], w_o[0], batch, seq)
    y_prompt = _tail(xp, mix_p, wo, ffn_norm, wg, wu, wd).reshape(x_prompt.shape)
    k_win_prompt = _windows_from_native(kwin)
    v_win_prompt = _windows_from_native(vwin)

    xs = x_sample.reshape(dec_batch * dec_seq, d_model)
    q, k, v, u, g = _inproj(xs, attn_norm, w_in_b, q_norm, k_norm, sg_norm, cos_s, sin_s, F32, None)
    mix_s, kw, vw = _sample_mix(sink_vec, q, k, v, _windows_to_native(cache_k_win[0]),
                                _windows_to_native(cache_v_win[0]), u, g, sgw, sgb,
                                attn_out_norm, sg_out_norm, dec_seq)
    y_sample = _tail(xs, mix_s, wo, ffn_norm, wg, wu, wd).reshape(x_sample.shape)
    k_win_sample = _windows_from_native(kw)
    v_win_sample = _windows_from_native(vw)
    sg_v_sample = g.reshape(1, dec_batch, dec_seq, SG_WIDTH)

    return (y_prompt, y_sample, k_win_prompt, v_win_prompt, k_win_sample, v_win_sample, sg_v_sample)
```

```python
import functools

import jax
import jax.numpy as jnp
from jax import lax
from jax.experimental import pallas as pl
from jax.experimental.pallas import tpu as pltpu

HEAD_DIM = 64
N_KV_HEADS = 4
Q_PER_KV = 4
N_Q_HEADS = N_KV_HEADS * Q_PER_KV
KV_WIDTH = N_KV_HEADS * HEAD_DIM
ATTN_WIDTH = N_Q_HEADS * HEAD_DIM
WINDOW = 128
N_SG_HEADS = 8
SG_HEAD_DIM = 128
SG_WIDTH = N_SG_HEADS * SG_HEAD_DIM
CHUNK = 128
PAST_LEN = 16384
ROPE_THETA = 10000.0
EPS = 1e-6

LANES = 128
VMEM_LIMIT_BYTES = 56 * 1024 * 1024

ROW_TILE = 512
INPROJ_SUB_BLOCKS = 4
FF_TILE = 256
SEQS_PER_STEP = 16

F32 = jnp.float32
BF16 = jnp.bfloat16


def _rms(x, gain_row):
    r = lax.rsqrt(jnp.mean(x * x, axis=-1, keepdims=True) + EPS)
    return x * r * gain_row


def _swapped_head(c):
    return Q_PER_KV * (c % N_KV_HEADS) + c // N_KV_HEADS


def _swap_heads(t):
    lane = lax.broadcasted_iota(jnp.int32, (t.shape[0], LANES), 1)
    lo = lane < HEAD_DIM
    outs = []
    for j in range(ATTN_WIDTH // LANES):
        halves = []
        for half in range(2):
            a = _swapped_head(2 * j + half)
            src = t[:, LANES * (a // 2):LANES * (a // 2 + 1)]
            halves.append(src if a % 2 == half else pltpu.roll(src, HEAD_DIM, 1))
        outs.append(jnp.where(lo, halves[0], halves[1]))
    return jnp.concatenate(outs, axis=1)


def _head_norm_rope(t, gain_row, cos, sin_signed):
    lane = lax.broadcasted_iota(jnp.int32, (t.shape[0], LANES), 1)
    lo_head = lane < HEAD_DIM
    first_half = (lane % HEAD_DIM) < (HEAD_DIM // 2)
    outs = []
    for j in range(t.shape[1] // LANES):
        blk = t[:, LANES * j:LANES * (j + 1)]
        sq = blk * blk
        s_all = jnp.sum(sq, axis=-1, keepdims=True)
        s_lo = jnp.sum(jnp.where(lo_head, sq, 0.0), axis=-1, keepdims=True)
        s_hi = s_all - s_lo
        r = jnp.where(lo_head,
                      lax.rsqrt(s_lo * (1.0 / HEAD_DIM) + EPS),
                      lax.rsqrt(s_hi * (1.0 / HEAD_DIM) + EPS))
        y = blk * r * gain_row
        rot = jnp.where(first_half,
                        pltpu.roll(y, LANES - HEAD_DIM // 2, 1),
                        pltpu.roll(y, HEAD_DIM // 2, 1))
        outs.append(y * cos + rot * sin_signed)
    return jnp.concatenate(outs, axis=1)


def _inproj_body(x_ref, an_ref, w_ref, qn_ref, kn_ref, sgn_ref, cos_ref, sin_ref, *rest,
                 tiles_per_seq, n_cast):
    casts_in, rest = rest[:n_cast], rest[n_cast:]
    q_ref, k_ref, v_ref, u_ref, g_ref = rest[:5]
    casts_out, rest = rest[5:5 + n_cast], rest[5 + n_cast:]
    proj_scr = rest[-1]
    n_sub, sub = proj_scr.shape[0], proj_scr.shape[1]
    q_gain = jnp.concatenate([qn_ref[...]] * (LANES // HEAD_DIM), axis=1)
    k_gain = jnp.concatenate([kn_ref[...]] * (LANES // HEAD_DIM), axis=1)
    c0, c1, c2, c3 = ATTN_WIDTH, ATTN_WIDTH + KV_WIDTH, ATTN_WIDTH + 2 * KV_WIDTH, \
        ATTN_WIDTH + 2 * KV_WIDTH + SG_WIDTH

    def project(r):
        xn = _rms(x_ref[sub * r:sub * (r + 1), :], an_ref[...]).astype(BF16)
        proj_scr[r] = jnp.dot(xn, w_ref[...], preferred_element_type=F32)

    def finish(r):
        rows = slice(sub * r, sub * (r + 1))
        cos = cos_ref[rows, :]
        sin = sin_ref[rows, :]
        q = _head_norm_rope(proj_scr[r, :, 0:c0], q_gain, cos, sin) * (HEAD_DIM ** -0.5)
        q_ref[rows, :] = _swap_heads(q).astype(q_ref.dtype)
        k = _head_norm_rope(proj_scr[r, :, c0:c1], k_gain, cos, sin)
        k_ref[rows, :] = k.astype(k_ref.dtype)
        v = proj_scr[r, :, c1:c2]
        v_ref[rows, :] = v.astype(v_ref.dtype)
        u_ref[rows, :] = jax.nn.gelu(proj_scr[r, :, c2:c3]).astype(u_ref.dtype)
        g = jax.nn.gelu(proj_scr[r, :, c3:])
        g_ref[rows, :] = _rms(g, sgn_ref[...]).astype(g_ref.dtype)
        return k, v

    project(0)
    for src, dst in zip(casts_in, casts_out):
        dst[...] = src[...].astype(BF16)
    for r in range(1, n_sub):
        project(r)
        finish(r - 1)
    k, v = finish(n_sub - 1)

    if len(rest) > 1:
        kwin_ref, vwin_ref = rest[0], rest[1]

        @pl.when(pl.program_id(0) % tiles_per_seq == tiles_per_seq - 1)
        def _():
            kwin_ref[0] = k[sub - WINDOW:].T
            vwin_ref[0] = v[sub - WINDOW:].T


def _inproj(x, attn_norm, w_in, q_norm, k_norm, sg_norm, cos_t, sin_t, act_dtype, seq, casts=()):
    rows, d_model = x.shape
    tm = min(ROW_TILE, rows)
    steps = rows // tm
    n_tab = cos_t.shape[0] // tm
    const = lambda i: (0, 0)
    row = lambda i: (i, 0)
    tab = lambda i: (i % n_tab, 0)
    slabs = [pl.BlockSpec((w.shape[0] // steps, w.shape[1]), row) for w in casts]
    out_specs = [
        pl.BlockSpec((tm, ATTN_WIDTH), row),
        pl.BlockSpec((tm, KV_WIDTH), row),
        pl.BlockSpec((tm, KV_WIDTH), row),
        pl.BlockSpec((tm, SG_WIDTH), row),
        pl.BlockSpec((tm, SG_WIDTH), row),
    ]
    out_shape = [
        jax.ShapeDtypeStruct((rows, ATTN_WIDTH), act_dtype),
        jax.ShapeDtypeStruct((rows, KV_WIDTH), act_dtype),
        jax.ShapeDtypeStruct((rows, KV_WIDTH), act_dtype),
        jax.ShapeDtypeStruct((rows, SG_WIDTH), BF16),
        jax.ShapeDtypeStruct((rows, SG_WIDTH), act_dtype),
    ]
    out_specs += slabs
    out_shape += [jax.ShapeDtypeStruct(w.shape, BF16) for w in casts]
    tiles_per_seq = 1
    if seq is not None:
        tiles_per_seq = seq // tm
        win = lambda i: (i // tiles_per_seq, 0, 0)
        out_specs += [pl.BlockSpec((1, KV_WIDTH, WINDOW), win)] * 2
        out_shape += [jax.ShapeDtypeStruct((rows // seq, KV_WIDTH, WINDOW), F32)] * 2
    return pl.pallas_call(
        functools.partial(_inproj_body, tiles_per_seq=tiles_per_seq, n_cast=len(casts)),
        grid=(steps,),
        in_specs=[
            pl.BlockSpec((tm, d_model), row),
            pl.BlockSpec((1, d_model), const),
            pl.BlockSpec(w_in.shape, const, pipeline_mode=pl.Buffered(1)),
            pl.BlockSpec((1, HEAD_DIM), const),
            pl.BlockSpec((1, HEAD_DIM), const),
            pl.BlockSpec((1, SG_WIDTH), const),
            pl.BlockSpec((tm, LANES), tab),
            pl.BlockSpec((tm, LANES), tab),
        ] + slabs,
        out_specs=out_specs,
        out_shape=out_shape,
        scratch_shapes=[pltpu.VMEM((INPROJ_SUB_BLOCKS, tm // INPROJ_SUB_BLOCKS, w_in.shape[1]), F32)],
        compiler_params=pltpu.CompilerParams(
            dimension_semantics=("arbitrary",), vmem_limit_bytes=VMEM_LIMIT_BYTES),
        name="inproj",
    )(x, attn_norm, w_in, q_norm, k_norm, sg_norm, cos_t, sin_t, *casts)


def _softmax_sink(s, mask, sink):
    s = jnp.where(mask, s, -jnp.inf)
    m = jnp.maximum(jnp.max(s, axis=-1, keepdims=True), sink)
    p = jnp.exp(s - m)
    denom = jnp.sum(p, axis=-1, keepdims=True) + jnp.exp(sink - m)
    return p, 1.0 / denom


def _spatial_gate(u, g, w_ref, b_ref, rows_per_seq):
    r = lax.broadcasted_iota(jnp.int32, (CHUNK, CHUNK), 0)
    c = lax.broadcasted_iota(jnp.int32, (CHUNK, CHUNK), 1)
    causal = (r // rows_per_seq == c // rows_per_seq) & (c % rows_per_seq <= r % rows_per_seq)
    gb = g.astype(BF16)
    b_t = b_ref[...].T[0:rows_per_seq, :]
    b_t = jnp.concatenate([b_t] * (CHUNK // rows_per_seq), axis=0)
    periodic = ((r < rows_per_seq) & (c % rows_per_seq == r)).astype(BF16)
    outs = []
    for h in range(N_SG_HEADS):
        if rows_per_seq == CHUNK:
            w = w_ref[h]
        else:
            w = jnp.dot(w_ref[h, 0:rows_per_seq, :].astype(BF16), periodic, preferred_element_type=F32)
            w = jnp.concatenate([w] * (CHUNK // rows_per_seq), axis=0)
        w = jnp.where(causal, w, 0.0).astype(BF16)
        cols = slice(SG_HEAD_DIM * h, SG_HEAD_DIM * (h + 1))
        bias = jnp.broadcast_to(b_t[:, h:h + 1], (CHUNK, SG_HEAD_DIM))
        parts = []
        for ch in range(u.shape[0] // CHUNK):
            rws = slice(CHUNK * ch, CHUNK * (ch + 1))
            mixed = jnp.dot(w, gb[rws, cols], preferred_element_type=F32) + bias
            parts.append(u[rws, cols].astype(F32) * mixed)
        outs.append(jnp.concatenate(parts, axis=0))
    return jnp.concatenate(outs, axis=1)


def _normed_mix(a_swapped, sgo, aon, son):
    return jnp.concatenate([_rms(a_swapped, _swap_heads(aon)), _rms(sgo, son)], axis=1).astype(BF16)


def _prompt_mix_body(sinks_ref, q_ref, k_ref, v_ref, u_ref, g_ref, sgw_ref, bias_ref,
                     aon_ref, son_ref, wd32, wo32_lo, wo32_hi, mix_ref, wd16, wo16, a_scr):
    wd16[...] = wd32[...].astype(BF16)
    wo16[0:HEAD_DIM, :] = wo32_lo[...].astype(BF16)
    wo16[HEAD_DIM:2 * HEAD_DIM, :] = wo32_hi[...].astype(BF16)

    j = pl.program_id(1)
    blocks = q_ref.shape[0] // WINDOW
    quarter = lax.broadcasted_iota(jnp.int32, (WINDOW, KV_WIDTH), 1) // HEAD_DIM
    qi = lax.broadcasted_iota(jnp.int32, (Q_PER_KV * WINDOW, WINDOW), 0) % WINDOW
    kj = lax.broadcasted_iota(jnp.int32, (Q_PER_KV * WINDOW, WINDOW), 1)
    from_prev = kj > qi
    row_g = lax.broadcasted_iota(jnp.int32, (Q_PER_KV * WINDOW, 1), 0) // WINDOW
    sink_cols = []
    for h in range(N_KV_HEADS):
        col = jnp.zeros((Q_PER_KV * WINDOW, 1), F32)
        for g in range(Q_PER_KV):
            col = jnp.where(row_g == g, sinks_ref[Q_PER_KV * h + g], col)
        sink_cols.append(col)

    def block(r, carry):
        n = j * blocks + r
        prev = pl.multiple_of(jnp.maximum(n - 1, 0) * WINDOW, WINDOW)
        cur = pl.multiple_of(n * WINDOW, WINDOW)
        no_prev = jnp.where(n > 0, 0.0, -jnp.inf)
        q = q_ref[pl.ds(pl.multiple_of(r * WINDOW, WINDOW), WINDOW), :]
        kb = jnp.concatenate([k_ref[pl.ds(prev, WINDOW), :], k_ref[pl.ds(cur, WINDOW), :]], axis=0)
        vb = jnp.concatenate([v_ref[pl.ds(prev, WINDOW), :], v_ref[pl.ds(cur, WINDOW), :]], axis=0)
        groups = None
        for h in range(N_KV_HEADS):
            sel = quarter == h
            lhs = jnp.concatenate(
                [jnp.where(sel, q[:, KV_WIDTH * g:KV_WIDTH * (g + 1)], 0) for g in range(Q_PER_KV)],
                axis=0)
            s = lax.dot_general(lhs, kb, (((1,), (1,)), ((), ())), preferred_element_type=F32)
            s = jnp.where(from_prev, s[:, :WINDOW] + no_prev, s[:, WINDOW:])
            sink = sink_cols[h]
            m = jnp.maximum(jnp.max(s, axis=-1, keepdims=True), sink)
            p = jnp.exp(s - m)
            inv = 1.0 / (jnp.sum(p, axis=-1, keepdims=True) + jnp.exp(sink - m))
            p_band = jnp.concatenate(
                [jnp.where(from_prev, p, 0.0), jnp.where(from_prev, 0.0, p)], axis=1).astype(BF16)
            o = jnp.dot(p_band, vb, preferred_element_type=F32) * inv
            parts = [o[WINDOW * g:WINDOW * (g + 1)] for g in range(Q_PER_KV)]
            groups = parts if groups is None else [
                jnp.where(sel, part, acc) for part, acc in zip(parts, groups)]
        a_scr[pl.ds(pl.multiple_of(r * WINDOW, WINDOW), WINDOW), :] = jnp.concatenate(groups, axis=1)
        return carry

    lax.fori_loop(0, blocks, block, 0, unroll=2)

    sgo = _spatial_gate(u_ref[...], g_ref[...], sgw_ref, bias_ref, CHUNK)
    mix_ref[...] = _normed_mix(a_scr[...], sgo, aon_ref[...], son_ref[...])


def _prompt_mix(sinks, q, k, v, u, g, sg_w, bias, aon, son, wd, wo, batch, seq):
    tm = ROW_TILE
    tiles = seq // tm
    steps = batch * tiles
    d_model = wo.shape[1]
    assert wo.shape[0] == 2 * HEAD_DIM * steps
    row = lambda b, j: (b * tiles + j, 0)
    per_seq = lambda b, j: (b, 0)
    const2 = lambda b, j: (0, 0)
    const3 = lambda b, j: (0, 0, 0)
    slab = lambda w: pl.BlockSpec((w.shape[0] // steps, w.shape[1]), row)
    bf16_like = lambda w: jax.ShapeDtypeStruct(w.shape, BF16)

    def wo_block(half):
        def index(b, j):
            c = 2 * (b * tiles + j) + half
            return jnp.where(c < N_Q_HEADS, _swapped_head(c), c), 0
        return pl.BlockSpec((HEAD_DIM, d_model), index)

    return pl.pallas_call(
        _prompt_mix_body,
        grid=(batch, tiles),
        in_specs=[
            pl.BlockSpec(memory_space=pltpu.SMEM),
            pl.BlockSpec((tm, ATTN_WIDTH), row),
            pl.BlockSpec((seq, KV_WIDTH), per_seq),
            pl.BlockSpec((seq, KV_WIDTH), per_seq),
            pl.BlockSpec((tm, SG_WIDTH), row),
            pl.BlockSpec((tm, SG_WIDTH), row),
            pl.BlockSpec(sg_w.shape, const3),
            pl.BlockSpec(bias.shape, const2),
            pl.BlockSpec((1, ATTN_WIDTH), const2),
            pl.BlockSpec((1, SG_WIDTH), const2),
            slab(wd), wo_block(0), wo_block(1),
        ],
        out_specs=[pl.BlockSpec((tm, d_model), row), slab(wd), slab(wo)],
        out_shape=[jax.ShapeDtypeStruct((batch * seq, d_model), BF16), bf16_like(wd), bf16_like(wo)],
        scratch_shapes=[pltpu.VMEM((tm, ATTN_WIDTH), F32)],
        compiler_params=pltpu.CompilerParams(
            dimension_semantics=("arbitrary", "arbitrary"), vmem_limit_bytes=VMEM_LIMIT_BYTES),
        name="prompt_mix",
    )(sinks, q, k, v, u, g, sg_w, bias, aon, son, wd, wo, wo)


def _sample_mix_body(sinks_ref, q_ref, k_ref, v_ref, ck_ref, cv_ref, u_ref, g_ref, sgw_ref,
                     bias_ref, aon_ref, son_ref, mix_ref, kw_ref, vw_ref, a_scr, *, dec_seq):
    rows = N_Q_HEADS * dec_seq
    quarter = lax.broadcasted_iota(jnp.int32, (dec_seq, KV_WIDTH), 1) // HEAD_DIM
    t = lax.broadcasted_iota(jnp.int32, (rows, 2 * WINDOW), 0) % dec_seq
    kj = lax.broadcasted_iota(jnp.int32, (rows, 2 * WINDOW), 1)
    pos_lane = lax.broadcasted_iota(jnp.int32, (KV_WIDTH, WINDOW), 1)
    row_head = lax.broadcasted_iota(jnp.int32, (rows, 1), 0) // dec_seq
    sink = jnp.zeros((rows, 1), F32)
    for a in range(N_Q_HEADS):
        sink = jnp.where(row_head == a, sinks_ref[a], sink)
    k_new_t = k_ref[...].T
    v_new_t = v_ref[...].T

    def one_seq(b, carry):
        r0 = pl.multiple_of(b * dec_seq, dec_seq)
        q = q_ref[pl.ds(r0, dec_seq), :]
        ck = ck_ref[b]
        cv = cv_ref[b]
        own = kj - WINDOW - b * dec_seq
        mask = ((kj < WINDOW) & (kj > t)) | ((own >= 0) & (own <= t))
        lhs = jnp.concatenate(
            [jnp.where(quarter == h, q[:, KV_WIDTH * g:KV_WIDTH * (g + 1)], 0.0)
             for h in range(N_KV_HEADS) for g in range(Q_PER_KV)], axis=0).astype(BF16)
        k_all = jnp.concatenate([ck, k_new_t], axis=1).astype(BF16)
        v_all = jnp.concatenate([cv, v_new_t], axis=1).astype(BF16)
        s = jnp.dot(lhs, k_all, preferred_element_type=F32)
        p, inv = _softmax_sink(s, mask, sink)
        o = lax.dot_general(p.astype(BF16), v_all, (((1,), (1,)), ((), ())),
                            preferred_element_type=F32) * inv
        groups = []
        for g in range(Q_PER_KV):
            piece = lambda h: o[(Q_PER_KV * h + g) * dec_seq:(Q_PER_KV * h + g + 1) * dec_seq]
            acc = piece(N_KV_HEADS - 1)
            for h in range(N_KV_HEADS - 2, -1, -1):
                acc = jnp.where(quarter == h, piece(h), acc)
            groups.append(acc)
        a_scr[pl.ds(r0, dec_seq), :] = jnp.concatenate(groups, axis=1)
        return carry

    lax.fori_loop(0, ck_ref.shape[0], one_seq, 0, unroll=4)

    keep = pos_lane < WINDOW - dec_seq
    for b in range(ck_ref.shape[0]):
        shift = WINDOW - dec_seq - b * dec_seq
        kw_ref[b] = jnp.where(keep, pltpu.roll(ck_ref[b], WINDOW - dec_seq, 1),
                              pltpu.roll(k_new_t, shift, 1))
        vw_ref[b] = jnp.where(keep, pltpu.roll(cv_ref[b], WINDOW - dec_seq, 1),
                              pltpu.roll(v_new_t, shift, 1))

    sgo = _spatial_gate(u_ref[...], g_ref[...], sgw_ref, bias_ref, dec_seq)
    mix_ref[...] = _normed_mix(a_scr[...], sgo, aon_ref[...], son_ref[...])


def _sample_mix(sinks, q, k, v, ck, cv, u, g, sg_w, bias, aon, son, dec_seq):
    nseq = ck.shape[0]
    sb = SEQS_PER_STEP
    tm = sb * dec_seq
    assert tm == WINDOW == CHUNK and dec_seq % 8 == 0
    row = lambda i: (i, 0)
    seq3 = lambda i: (i, 0, 0)
    const2 = lambda i: (0, 0)
    const3 = lambda i: (0, 0, 0)
    return pl.pallas_call(
        functools.partial(_sample_mix_body, dec_seq=dec_seq),
        grid=(nseq // sb,),
        in_specs=[
            pl.BlockSpec(memory_space=pltpu.SMEM),
            pl.BlockSpec((tm, ATTN_WIDTH), row),
            pl.BlockSpec((tm, KV_WIDTH), row),
            pl.BlockSpec((tm, KV_WIDTH), row),
            pl.BlockSpec((sb, KV_WIDTH, WINDOW), seq3),
            pl.BlockSpec((sb, KV_WIDTH, WINDOW), seq3),
            pl.BlockSpec((tm, SG_WIDTH), row),
            pl.BlockSpec((tm, SG_WIDTH), row),
            pl.BlockSpec(sg_w.shape, const3),
            pl.BlockSpec(bias.shape, const2),
            pl.BlockSpec((1, ATTN_WIDTH), const2),
            pl.BlockSpec((1, SG_WIDTH), const2),
        ],
        out_specs=[
            pl.BlockSpec((tm, ATTN_WIDTH + SG_WIDTH), row),
            pl.BlockSpec((sb, KV_WIDTH, WINDOW), seq3),
            pl.BlockSpec((sb, KV_WIDTH, WINDOW), seq3),
        ],
        out_shape=[
            jax.ShapeDtypeStruct((nseq * dec_seq, ATTN_WIDTH + SG_WIDTH), BF16),
            jax.ShapeDtypeStruct(ck.shape, F32),
            jax.ShapeDtypeStruct(cv.shape, F32),
        ],
        scratch_shapes=[pltpu.VMEM((tm, ATTN_WIDTH), F32)],
        compiler_params=pltpu.CompilerParams(
            dimension_semantics=("arbitrary",), vmem_limit_bytes=VMEM_LIMIT_BYTES),
        name="sample_mix",
    )(sinks, q, k, v, ck, cv, u, g, sg_w, bias, aon, son)


def _tail_body(x_ref, mix_ref, wo_ref, fn_ref, wg_ref, wu_ref, wd_ref, y_ref, hn_scr):
    def ffn_chunk():
        hn = hn_scr[...]
        gate = jnp.dot(hn, wg_ref[...], preferred_element_type=F32)
        up = jnp.dot(hn, wu_ref[...], preferred_element_type=F32)
        act = (jax.nn.silu(gate) * up).astype(BF16)
        y_ref[...] += jnp.dot(act, wd_ref[...], preferred_element_type=F32)

    @pl.when(pl.program_id(1) == 0)
    def _():
        half = x_ref.shape[0] // 2
        for rows in (slice(0, half), slice(half, 2 * half)):
            y_ref[rows, :] = x_ref[rows, :] + jnp.dot(
                mix_ref[rows, :], wo_ref[...], preferred_element_type=F32)
        for rows in (slice(0, half), slice(half, 2 * half)):
            hn_scr[rows, :] = _rms(y_ref[rows, :], fn_ref[...]).astype(BF16)
        ffn_chunk()

    @pl.when(pl.program_id(1) > 0)
    def _():
        ffn_chunk()


def _tail(x, mix, wo, ffn_norm, wg, wu, wd):
    rows, d_model = x.shape
    d_ff = wg.shape[1]
    tm = min(ROW_TILE, rows)
    tf = FF_TILE
    return pl.pallas_call(
        _tail_body,
        grid=(rows // tm, d_ff // tf),
        in_specs=[
            pl.BlockSpec((tm, d_model), lambda i, j: (i, 0)),
            pl.BlockSpec((tm, mix.shape[1]), lambda i, j: (i, 0)),
            pl.BlockSpec(wo.shape, lambda i, j: (0, 0), pipeline_mode=pl.Buffered(1)),
            pl.BlockSpec((1, d_model), lambda i, j: (0, 0)),
            pl.BlockSpec((d_model, tf), lambda i, j: (0, j)),
            pl.BlockSpec((d_model, tf), lambda i, j: (0, j)),
            pl.BlockSpec((tf, d_model), lambda i, j: (j, 0)),
        ],
        out_specs=pl.BlockSpec((tm, d_model), lambda i, j: (i, 0)),
        out_shape=jax.ShapeDtypeStruct(x.shape, F32),
        scratch_shapes=[pltpu.VMEM((tm, d_model), BF16)],
        compiler_params=pltpu.CompilerParams(
            dimension_semantics=("arbitrary", "arbitrary"), vmem_limit_bytes=VMEM_LIMIT_BYTES),
        name="tail",
    )(x, mix, wo, ffn_norm, wg, wu, wd)


def _rope_tables(pos):
    half = HEAD_DIM // 2
    lane = jnp.arange(LANES)
    inv = ROPE_THETA ** (-(lane % half).astype(F32) / half)
    ang = pos.astype(F32)[:, None] * inv[None, :]
    sign = jnp.where(lane % HEAD_DIM < half, -1.0, 1.0).astype(F32)
    return jnp.cos(ang), jnp.sin(ang) * sign[None, :]


def _windows_to_native(w):
    return jnp.transpose(w, (0, 2, 3, 1)).reshape(w.shape[0], KV_WIDTH, WINDOW)


def _windows_from_native(w):
    return jnp.transpose(w.reshape(w.shape[0], N_KV_HEADS, HEAD_DIM, WINDOW), (0, 3, 1, 2))[None]


def kernel(x_prompt, x_sample, cache_k_win, cache_v_win, attn_norm, w_in, q_norm, k_norm, sinks,
           sg_norm, sg_w, sg_b, attn_out_norm, sg_out_norm, w_o, ffn_norm, w_gate, w_up, w_down):
    assert w_in.shape[0] == 1, "single-layer step only"
    batch, seq, d_model = x_prompt.shape
    dec_batch, dec_seq, _ = x_sample.shape

    w_in_b = w_in[0].astype(BF16)
    sink_vec = sinks[0]
    sgw = sg_w[0][:, :CHUNK, :CHUNK]
    sgb = sg_b[0][:, :CHUNK]

    cos_p, sin_p = _rope_tables(jnp.arange(seq, dtype=jnp.int32))
    tm_s = min(ROW_TILE, dec_batch * dec_seq)
    cos_s, sin_s = _rope_tables(PAST_LEN + jnp.arange(tm_s, dtype=jnp.int32) % dec_seq)

    xp = x_prompt.reshape(batch * seq, d_model)
    q, k, v, u, g, wg, wu, kwin, vwin = _inproj(xp, attn_norm, w_in_b, q_norm, k_norm, sg_norm,
                                                cos_p, sin_p, BF16, seq,
                                                casts=(w_gate[0], w_up[0]))
    mix_p, wd, wo = _prompt_mix(sink_vec, q, k, v, u, g, sgw, sgb, attn_out_norm, sg_out_norm,
                                w_down[0], w_o[0], batch, seq)
    y_prompt = _tail(xp, mix_p, wo, ffn_norm, wg, wu, wd).reshape(x_prompt.shape)
    k_win_prompt = _windows_from_native(kwin)
    v_win_prompt = _windows_from_native(vwin)

    xs = x_sample.reshape(dec_batch * dec_seq, d_model)
    q, k, v, u, g = _inproj(xs, attn_norm, w_in_b, q_norm, k_norm, sg_norm, cos_s, sin_s, F32, None)
    mix_s, kw, vw = _sample_mix(sink_vec, q, k, v, _windows_to_native(cache_k_win[0]),
                                _windows_to_native(cache_v_win[0]), u, g, sgw, sgb,
                                attn_out_norm, sg_out_norm, dec_seq)
    y_sample = _tail(xs, mix_s, wo, ffn_norm, wg, wu, wd).reshape(x_sample.shape)
    k_win_sample = _windows_from_native(kw)
    v_win_sample = _windows_from_native(vw)
    sg_v_sample = g.reshape(1, dec_batch, dec_seq, SG_WIDTH)

    return (y_prompt, y_sample, k_win_prompt, v_win_prompt, k_win_sample, v_win_sample, sg_v_sample)
```

```python
import functools

import jax
import jax.numpy as jnp
from jax import lax
from jax.experimental import pallas as pl
from jax.experimental.pallas import tpu as pltpu

HEAD_DIM = 64
N_KV_HEADS = 4
Q_PER_KV = 4
N_Q_HEADS = N_KV_HEADS * Q_PER_KV
KV_WIDTH = N_KV_HEADS * HEAD_DIM
ATTN_WIDTH = N_Q_HEADS * HEAD_DIM
WINDOW = 128
N_SG_HEADS = 8
SG_HEAD_DIM = 128
SG_WIDTH = N_SG_HEADS * SG_HEAD_DIM
CHUNK = 128
PAST_LEN = 16384
ROPE_THETA = 10000.0
EPS = 1e-6

LANES = 128
VMEM_LIMIT_BYTES = 56 * 1024 * 1024
GAIN_ROWS = 8

ROW_TILE = 512
INPROJ_SUB_BLOCKS = 4
FF_TILE = 512
SEQS_PER_STEP = 16

F32 = jnp.float32
BF16 = jnp.bfloat16


def _rms(x, gain_row):
    r = lax.rsqrt(jnp.mean(x * x, axis=-1, keepdims=True) + EPS)
    return x * r * gain_row


def _swapped_head(c):
    return Q_PER_KV * (c % N_KV_HEADS) + c // N_KV_HEADS


def _swap_heads(t):
    lane = lax.broadcasted_iota(jnp.int32, (t.shape[0], LANES), 1)
    lo = lane < HEAD_DIM
    outs = []
    for j in range(ATTN_WIDTH // LANES):
        halves = []
        for half in range(2):
            a = _swapped_head(2 * j + half)
            src = t[:, LANES * (a // 2):LANES * (a // 2 + 1)]
            halves.append(src if a % 2 == half else pltpu.roll(src, HEAD_DIM, 1))
        outs.append(jnp.where(lo, halves[0], halves[1]))
    return jnp.concatenate(outs, axis=1)


def _head_norm_rope(t, gain_row, cos, sin_signed):
    lane = lax.broadcasted_iota(jnp.int32, (t.shape[0], LANES), 1)
    lo_head = lane < HEAD_DIM
    first_half = (lane % HEAD_DIM) < (HEAD_DIM // 2)
    outs = []
    for j in range(t.shape[1] // LANES):
        blk = t[:, LANES * j:LANES * (j + 1)]
        sq = blk * blk
        s_all = jnp.sum(sq, axis=-1, keepdims=True)
        s_lo = jnp.sum(jnp.where(lo_head, sq, 0.0), axis=-1, keepdims=True)
        s_hi = s_all - s_lo
        r = jnp.where(lo_head,
                      lax.rsqrt(s_lo * (1.0 / HEAD_DIM) + EPS),
                      lax.rsqrt(s_hi * (1.0 / HEAD_DIM) + EPS))
        y = blk * r * gain_row
        rot = jnp.where(first_half,
                        pltpu.roll(y, LANES - HEAD_DIM // 2, 1),
                        pltpu.roll(y, HEAD_DIM // 2, 1))
        outs.append(y * cos + rot * sin_signed)
    return jnp.concatenate(outs, axis=1)


def _inproj_body(x_ref, an_ref, w_ref, qn_ref, kn_ref, sgn_ref, cos_ref, sin_ref, *rest,
                 tiles_per_seq, n_cast):
    casts_in, rest = rest[:n_cast], rest[n_cast:]
    q_ref, k_ref, v_ref, u_ref, g_ref = rest[:5]
    casts_out, rest = rest[5:5 + n_cast], rest[5 + n_cast:]
    proj_scr = rest[-1]
    n_sub, sub = proj_scr.shape[0], proj_scr.shape[1]
    q_gain = jnp.concatenate([qn_ref[...]] * (LANES // HEAD_DIM), axis=1)
    k_gain = jnp.concatenate([kn_ref[...]] * (LANES // HEAD_DIM), axis=1)
    c0, c1, c2, c3 = ATTN_WIDTH, ATTN_WIDTH + KV_WIDTH, ATTN_WIDTH + 2 * KV_WIDTH, \
        ATTN_WIDTH + 2 * KV_WIDTH + SG_WIDTH

    def project(r):
        xn = _rms(x_ref[sub * r:sub * (r + 1), :], an_ref[...]).astype(BF16)
        proj_scr[r] = jnp.dot(xn, w_ref[...], preferred_element_type=F32)

    def finish(r):
        rows = slice(sub * r, sub * (r + 1))
        cos = cos_ref[rows, :]
        sin = sin_ref[rows, :]
        q = _head_norm_rope(proj_scr[r, :, 0:c0], q_gain, cos, sin) * (HEAD_DIM ** -0.5)
        q_ref[rows, :] = _swap_heads(q).astype(q_ref.dtype)
        k = _head_norm_rope(proj_scr[r, :, c0:c1], k_gain, cos, sin)
        k_ref[rows, :] = k.astype(k_ref.dtype)
        v = proj_scr[r, :, c1:c2]
        v_ref[rows, :] = v.astype(v_ref.dtype)
        u_ref[rows, :] = jax.nn.gelu(proj_scr[r, :, c2:c3]).astype(u_ref.dtype)
        g = jax.nn.gelu(proj_scr[r, :, c3:])
        g_ref[rows, :] = _rms(g, sgn_ref[...]).astype(g_ref.dtype)
        return k, v

    project(0)
    for src, dst in zip(casts_in, casts_out):
        dst[...] = src[...].astype(BF16)
    for r in range(1, n_sub):
        project(r)
        finish(r - 1)
    k, v = finish(n_sub - 1)

    if len(rest) > 1:
        kwin_ref, vwin_ref = rest[0], rest[1]

        @pl.when(pl.program_id(0) % tiles_per_seq == tiles_per_seq - 1)
        def _():
            kwin_ref[0] = k[sub - WINDOW:].T
            vwin_ref[0] = v[sub - WINDOW:].T


def _inproj(x, attn_norm, w_in, q_norm, k_norm, sg_norm, cos_t, sin_t, act_dtype, seq, casts=()):
    rows, d_model = x.shape
    tm = min(ROW_TILE, rows)
    steps = rows // tm
    n_tab = cos_t.shape[0] // tm
    const = lambda i: (0, 0)
    row = lambda i: (i, 0)
    tab = lambda i: (i % n_tab, 0)
    slabs = [pl.BlockSpec((w.shape[0] // steps, w.shape[1]), row) for w in casts]
    out_specs = [
        pl.BlockSpec((tm, ATTN_WIDTH), row),
        pl.BlockSpec((tm, KV_WIDTH), row),
        pl.BlockSpec((tm, KV_WIDTH), row),
        pl.BlockSpec((tm, SG_WIDTH), row),
        pl.BlockSpec((tm, SG_WIDTH), row),
    ]
    out_shape = [
        jax.ShapeDtypeStruct((rows, ATTN_WIDTH), act_dtype),
        jax.ShapeDtypeStruct((rows, KV_WIDTH), act_dtype),
        jax.ShapeDtypeStruct((rows, KV_WIDTH), act_dtype),
        jax.ShapeDtypeStruct((rows, SG_WIDTH), BF16),
        jax.ShapeDtypeStruct((rows, SG_WIDTH), act_dtype),
    ]
    out_specs += slabs
    out_shape += [jax.ShapeDtypeStruct(w.shape, BF16) for w in casts]
    tiles_per_seq = 1
    if seq is not None:
        tiles_per_seq = seq // tm
        win = lambda i: (i // tiles_per_seq, 0, 0)
        out_specs += [pl.BlockSpec((1, KV_WIDTH, WINDOW), win)] * 2
        out_shape += [jax.ShapeDtypeStruct((rows // seq, KV_WIDTH, WINDOW), F32)] * 2
    return pl.pallas_call(
        functools.partial(_inproj_body, tiles_per_seq=tiles_per_seq, n_cast=len(casts)),
        grid=(steps,),
        in_specs=[
            pl.BlockSpec((tm, d_model), row),
            pl.BlockSpec((1, d_model), const),
            pl.BlockSpec(w_in.shape, const, pipeline_mode=pl.Buffered(1)),
            pl.BlockSpec((1, HEAD_DIM), const),
            pl.BlockSpec((1, HEAD_DIM), const),
            pl.BlockSpec((1, SG_WIDTH), const),
            pl.BlockSpec((tm, LANES), tab),
            pl.BlockSpec((tm, LANES), tab),
        ] + slabs,
        out_specs=out_specs,
        out_shape=out_shape,
        scratch_shapes=[pltpu.VMEM((INPROJ_SUB_BLOCKS, tm // INPROJ_SUB_BLOCKS, w_in.shape[1]), F32)],
        compiler_params=pltpu.CompilerParams(
            dimension_semantics=("arbitrary",), vmem_limit_bytes=VMEM_LIMIT_BYTES),
        name="inproj",
    )(x, attn_norm, w_in, q_norm, k_norm, sg_norm, cos_t, sin_t, *casts)


def _softmax_sink(s, mask, sink):
    s = jnp.where(mask, s, -jnp.inf)
    m = jnp.maximum(jnp.max(s, axis=-1, keepdims=True), sink)
    p = jnp.exp(s - m)
    denom = jnp.sum(p, axis=-1, keepdims=True) + jnp.exp(sink - m)
    return p, 1.0 / denom


def _spatial_gate(u, g, w_ref, b_ref, rows_per_seq):
    r = lax.broadcasted_iota(jnp.int32, (CHUNK, CHUNK), 0)
    c = lax.broadcasted_iota(jnp.int32, (CHUNK, CHUNK), 1)
    causal = (r // rows_per_seq == c // rows_per_seq) & (c % rows_per_seq <= r % rows_per_seq)
    gb = g.astype(BF16)
    b_t = b_ref[...].T[0:rows_per_seq, :]
    b_t = jnp.concatenate([b_t] * (CHUNK // rows_per_seq), axis=0)
    periodic = ((r < rows_per_seq) & (c % rows_per_seq == r)).astype(BF16)
    outs = []
    for h in range(N_SG_HEADS):
        if rows_per_seq == CHUNK:
            w = w_ref[h]
        else:
            w = jnp.dot(w_ref[h, 0:rows_per_seq, :].astype(BF16), periodic, preferred_element_type=F32)
            w = jnp.concatenate([w] * (CHUNK // rows_per_seq), axis=0)
        w = jnp.where(causal, w, 0.0).astype(BF16)
        cols = slice(SG_HEAD_DIM * h, SG_HEAD_DIM * (h + 1))
        bias = jnp.broadcast_to(b_t[:, h:h + 1], (CHUNK, SG_HEAD_DIM))
        parts = []
        for ch in range(u.shape[0] // CHUNK):
            rws = slice(CHUNK * ch, CHUNK * (ch + 1))
            mixed = jnp.dot(w, gb[rws, cols], preferred_element_type=F32) + bias
            parts.append(u[rws, cols].astype(F32) * mixed)
        outs.append(jnp.concatenate(parts, axis=0))
    return jnp.concatenate(outs, axis=1)


def _normed_mix(a_swapped, sgo, aon, son):
    return jnp.concatenate([_rms(a_swapped, _swap_heads(aon)), _rms(sgo, son)], axis=1).astype(BF16)


def _prompt_mix_body(sinks_ref, q_ref, k_ref, v_ref, u_ref, g_ref, sgw_ref, bias_ref,
                     aon_ref, son_ref, wd32, wo32_lo, wo32_hi, mix_ref, wd16, wo16, a_scr):
    wd16[...] = wd32[...].astype(BF16)
    wo16[0:HEAD_DIM, :] = wo32_lo[...].astype(BF16)
    wo16[HEAD_DIM:2 * HEAD_DIM, :] = wo32_hi[...].astype(BF16)

    j = pl.program_id(1)
    blocks = q_ref.shape[0] // WINDOW
    quarter = lax.broadcasted_iota(jnp.int32, (WINDOW, KV_WIDTH), 1) // HEAD_DIM
    qi = lax.broadcasted_iota(jnp.int32, (Q_PER_KV * WINDOW, WINDOW), 0) % WINDOW
    kj = lax.broadcasted_iota(jnp.int32, (Q_PER_KV * WINDOW, WINDOW), 1)
    from_prev = kj > qi
    row_g = lax.broadcasted_iota(jnp.int32, (Q_PER_KV * WINDOW, 1), 0) // WINDOW
    sink_cols = []
    for h in range(N_KV_HEADS):
        col = jnp.zeros((Q_PER_KV * WINDOW, 1), F32)
        for g in range(Q_PER_KV):
            col = jnp.where(row_g == g, sinks_ref[Q_PER_KV * h + g], col)
        sink_cols.append(col)

    def block(r, carry):
        n = j * blocks + r
        prev = pl.multiple_of(jnp.maximum(n - 1, 0) * WINDOW, WINDOW)
        cur = pl.multiple_of(n * WINDOW, WINDOW)
        no_prev = jnp.where(n > 0, 0.0, -jnp.inf)
        q = q_ref[pl.ds(pl.multiple_of(r * WINDOW, WINDOW), WINDOW), :]
        kb = jnp.concatenate([k_ref[pl.ds(prev, WINDOW), :], k_ref[pl.ds(cur, WINDOW), :]], axis=0)
        vb = jnp.concatenate([v_ref[pl.ds(prev, WINDOW), :], v_ref[pl.ds(cur, WINDOW), :]], axis=0)
        groups = None
        for h in range(N_KV_HEADS):
            sel = quarter == h
            lhs = jnp.concatenate(
                [jnp.where(sel, q[:, KV_WIDTH * g:KV_WIDTH * (g + 1)], 0) for g in range(Q_PER_KV)],
                axis=0)
            s = lax.dot_general(lhs, kb, (((1,), (1,)), ((), ())), preferred_element_type=F32)
            s = jnp.where(from_prev, s[:, :WINDOW] + no_prev, s[:, WINDOW:])
            sink = sink_cols[h]
            m = jnp.maximum(jnp.max(s, axis=-1, keepdims=True), sink)
            p = jnp.exp(s - m)
            inv = 1.0 / (jnp.sum(p, axis=-1, keepdims=True) + jnp.exp(sink - m))
            p_band = jnp.concatenate(
                [jnp.where(from_prev, p, 0.0), jnp.where(from_prev, 0.0, p)], axis=1).astype(BF16)
            o = jnp.dot(p_band, vb, preferred_element_type=F32) * inv
            parts = [o[WINDOW * g:WINDOW * (g + 1)] for g in range(Q_PER_KV)]
            groups = parts if groups is None else [
                jnp.where(sel, part, acc) for part, acc in zip(parts, groups)]
        a_scr[pl.ds(pl.multiple_of(r * WINDOW, WINDOW), WINDOW), :] = jnp.concatenate(groups, axis=1)
        return carry

    lax.fori_loop(0, blocks, block, 0, unroll=2)

    sgo = _spatial_gate(u_ref[...], g_ref[...], sgw_ref, bias_ref, CHUNK)
    mix_ref[...] = _normed_mix(a_scr[...], sgo, aon_ref[...], son_ref[...])


def _prompt_mix(sinks, q, k, v, u, g, sg_w, bias, aon, son, wd, wo, batch, seq):
    tm = ROW_TILE
    tiles = seq // tm
    steps = batch * tiles
    d_model = wo.shape[1]
    assert wo.shape[0] == 2 * HEAD_DIM * steps
    row = lambda b, j: (b * tiles + j, 0)
    per_seq = lambda b, j: (b, 0)
    const2 = lambda b, j: (0, 0)
    const3 = lambda b, j: (0, 0, 0)
    slab = lambda w: pl.BlockSpec((w.shape[0] // steps, w.shape[1]), row)
    bf16_like = lambda w: jax.ShapeDtypeStruct(w.shape, BF16)

    def wo_block(half):
        def index(b, j):
            c = 2 * (b * tiles + j) + half
            return jnp.where(c < N_Q_HEADS, _swapped_head(c), c), 0
        return pl.BlockSpec((HEAD_DIM, d_model), index)

    return pl.pallas_call(
        _prompt_mix_body,
        grid=(batch, tiles),
        in_specs=[
            pl.BlockSpec(memory_space=pltpu.SMEM),
            pl.BlockSpec((tm, ATTN_WIDTH), row),
            pl.BlockSpec((seq, KV_WIDTH), per_seq),
            pl.BlockSpec((seq, KV_WIDTH), per_seq),
            pl.BlockSpec((tm, SG_WIDTH), row),
            pl.BlockSpec((tm, SG_WIDTH), row),
            pl.BlockSpec(sg_w.shape, const3),
            pl.BlockSpec(bias.shape, const2),
            pl.BlockSpec((1, ATTN_WIDTH), const2),
            pl.BlockSpec((1, SG_WIDTH), const2),
            slab(wd), wo_block(0), wo_block(1),
        ],
        out_specs=[pl.BlockSpec((tm, d_model), row), slab(wd), slab(wo)],
        out_shape=[jax.ShapeDtypeStruct((batch * seq, d_model), BF16), bf16_like(wd), bf16_like(wo)],
        scratch_shapes=[pltpu.VMEM((tm, ATTN_WIDTH), F32)],
        compiler_params=pltpu.CompilerParams(
            dimension_semantics=("arbitrary", "arbitrary"), vmem_limit_bytes=VMEM_LIMIT_BYTES),
        name="prompt_mix",
    )(sinks, q, k, v, u, g, sg_w, bias, aon, son, wd, wo, wo)


def _sample_mix_body(sinks_ref, q_ref, k_ref, v_ref, ck_ref, cv_ref, u_ref, g_ref, sgw_ref,
                     bias_ref, aon_ref, son_ref, mix_ref, kw_ref, vw_ref, a_scr, *, dec_seq):
    rows = N_Q_HEADS * dec_seq
    quarter = lax.broadcasted_iota(jnp.int32, (dec_seq, KV_WIDTH), 1) // HEAD_DIM
    t = lax.broadcasted_iota(jnp.int32, (rows, 2 * WINDOW), 0) % dec_seq
    kj = lax.broadcasted_iota(jnp.int32, (rows, 2 * WINDOW), 1)
    pos_lane = lax.broadcasted_iota(jnp.int32, (KV_WIDTH, WINDOW), 1)
    row_head = lax.broadcasted_iota(jnp.int32, (rows, 1), 0) // dec_seq
    sink = jnp.zeros((rows, 1), F32)
    for a in range(N_Q_HEADS):
        sink = jnp.where(row_head == a, sinks_ref[a], sink)
    k_new_t = k_ref[...].T
    v_new_t = v_ref[...].T

    def one_seq(b, carry):
        r0 = pl.multiple_of(b * dec_seq, dec_seq)
        q = q_ref[pl.ds(r0, dec_seq), :]
        ck = ck_ref[b]
        cv = cv_ref[b]
        own = kj - WINDOW - b * dec_seq
        mask = ((kj < WINDOW) & (kj > t)) | ((own >= 0) & (own <= t))
        lhs = jnp.concatenate(
            [jnp.where(quarter == h, q[:, KV_WIDTH * g:KV_WIDTH * (g + 1)], 0.0)
             for h in range(N_KV_HEADS) for g in range(Q_PER_KV)], axis=0).astype(BF16)
        k_all = jnp.concatenate([ck, k_new_t], axis=1).astype(BF16)
        v_all = jnp.concatenate([cv, v_new_t], axis=1).astype(BF16)
        s = jnp.dot(lhs, k_all, preferred_element_type=F32)
        p, inv = _softmax_sink(s, mask, sink)
        o = lax.dot_general(p.astype(BF16), v_all, (((1,), (1,)), ((), ())),
                            preferred_element_type=F32) * inv
        groups = []
        for g in range(Q_PER_KV):
            piece = lambda h: o[(Q_PER_KV * h + g) * dec_seq:(Q_PER_KV * h + g + 1) * dec_seq]
            acc = piece(N_KV_HEADS - 1)
            for h in range(N_KV_HEADS - 2, -1, -1):
                acc = jnp.where(quarter == h, piece(h), acc)
            groups.append(acc)
        a_scr[pl.ds(r0, dec_seq), :] = jnp.concatenate(groups, axis=1)
        return carry

    lax.fori_loop(0, ck_ref.shape[0], one_seq, 0, unroll=4)

    keep = pos_lane < WINDOW - dec_seq
    for b in range(ck_ref.shape[0]):
        shift = WINDOW - dec_seq - b * dec_seq
        kw_ref[b] = jnp.where(keep, pltpu.roll(ck_ref[b], WINDOW - dec_seq, 1),
                              pltpu.roll(k_new_t, shift, 1))
        vw_ref[b] = jnp.where(keep, pltpu.roll(cv_ref[b], WINDOW - dec_seq, 1),
                              pltpu.roll(v_new_t, shift, 1))

    sgo = _spatial_gate(u_ref[...], g_ref[...], sgw_ref, bias_ref, dec_seq)
    mix_ref[...] = _normed_mix(a_scr[...], sgo, aon_ref[...], son_ref[...])


def _sample_mix(sinks, q, k, v, ck, cv, u, g, sg_w, bias, aon, son, dec_seq):
    nseq = ck.shape[0]
    sb = SEQS_PER_STEP
    tm = sb * dec_seq
    assert tm == WINDOW == CHUNK and dec_seq % 8 == 0
    row = lambda i: (i, 0)
    seq3 = lambda i: (i, 0, 0)
    const2 = lambda i: (0, 0)
    const3 = lambda i: (0, 0, 0)
    return pl.pallas_call(
        functools.partial(_sample_mix_body, dec_seq=dec_seq),
        grid=(nseq // sb,),
        in_specs=[
            pl.BlockSpec(memory_space=pltpu.SMEM),
            pl.BlockSpec((tm, ATTN_WIDTH), row),
            pl.BlockSpec((tm, KV_WIDTH), row),
            pl.BlockSpec((tm, KV_WIDTH), row),
            pl.BlockSpec((sb, KV_WIDTH, WINDOW), seq3),
            pl.BlockSpec((sb, KV_WIDTH, WINDOW), seq3),
            pl.BlockSpec((tm, SG_WIDTH), row),
            pl.BlockSpec((tm, SG_WIDTH), row),
            pl.BlockSpec(sg_w.shape, const3),
            pl.BlockSpec(bias.shape, const2),
            pl.BlockSpec((1, ATTN_WIDTH), const2),
            pl.BlockSpec((1, SG_WIDTH), const2),
        ],
        out_specs=[
            pl.BlockSpec((tm, ATTN_WIDTH + SG_WIDTH), row),
            pl.BlockSpec((sb, KV_WIDTH, WINDOW), seq3),
            pl.BlockSpec((sb, KV_WIDTH, WINDOW), seq3),
        ],
        out_shape=[
            jax.ShapeDtypeStruct((nseq * dec_seq, ATTN_WIDTH + SG_WIDTH), BF16),
            jax.ShapeDtypeStruct(ck.shape, F32),
            jax.ShapeDtypeStruct(cv.shape, F32),
        ],
        scratch_shapes=[pltpu.VMEM((tm, ATTN_WIDTH), F32)],
        compiler_params=pltpu.CompilerParams(
            dimension_semantics=("arbitrary",), vmem_limit_bytes=VMEM_LIMIT_BYTES),
        name="sample_mix",
    )(sinks, q, k, v, ck, cv, u, g, sg_w, bias, aon, son)


def _tail_body(x_ref, mix_ref, wo_ref, fn_ref, wg_ref, wu_ref, wd_ref, y_ref, hn_scr):
    def ffn_chunk():
        hn = hn_scr[...]
        gate = jnp.dot(hn, wg_ref[...], preferred_element_type=F32)
        up = jnp.dot(hn, wu_ref[...], preferred_element_type=F32)
        act = (jax.nn.silu(gate) * up).astype(BF16)
        y_ref[...] += jnp.dot(act, wd_ref[...], preferred_element_type=F32)

    @pl.when(pl.program_id(1) == 0)
    def _():
        half = x_ref.shape[0] // 2
        for rows in (slice(0, half), slice(half, 2 * half)):
            y_ref[rows, :] = x_ref[rows, :] + jnp.dot(
                mix_ref[rows, :], wo_ref[...], preferred_element_type=F32)
        for rows in (slice(0, half), slice(half, 2 * half)):
            hn_scr[rows, :] = _rms(y_ref[rows, :], fn_ref[0:1, :]).astype(BF16)
        ffn_chunk()

    @pl.when(pl.program_id(1) > 0)
    def _():
        ffn_chunk()


def _tail(x, mix, wo, ffn_norm, wg, wu, wd):
    rows, d_model = x.shape
    d_ff = wg.shape[1]
    tm = min(ROW_TILE, rows)
    tf = FF_TILE
    return pl.pallas_call(
        _tail_body,
        grid=(rows // tm, d_ff // tf),
        in_specs=[
            pl.BlockSpec((tm, d_model), lambda i, j: (i, 0)),
            pl.BlockSpec((tm, mix.shape[1]), lambda i, j: (i, 0)),
            pl.BlockSpec(wo.shape, lambda i, j: (0, 0), pipeline_mode=pl.Buffered(1)),
            pl.BlockSpec((GAIN_ROWS, d_model), lambda i, j: (0, 0)),
            pl.BlockSpec((d_model, tf), lambda i, j: (0, j)),
            pl.BlockSpec((d_model, tf), lambda i, j: (0, j)),
            pl.BlockSpec((tf, d_model), lambda i, j: (j, 0)),
        ],
        out_specs=pl.BlockSpec((tm, d_model), lambda i, j: (i, 0)),
        out_shape=jax.ShapeDtypeStruct(x.shape, F32),
        scratch_shapes=[pltpu.VMEM((tm, d_model), BF16)],
        compiler_params=pltpu.CompilerParams(
            dimension_semantics=("arbitrary", "arbitrary"), vmem_limit_bytes=VMEM_LIMIT_BYTES),
        name="tail",
    )(x, mix, wo, jnp.broadcast_to(ffn_norm, (GAIN_ROWS, d_model)), wg, wu, wd)


def _rope_tables(pos):
    half = HEAD_DIM // 2
    lane = jnp.arange(LANES)
    inv = ROPE_THETA ** (-(lane % half).astype(F32) / half)
    ang = pos.astype(F32)[:, None] * inv[None, :]
    sign = jnp.where(lane % HEAD_DIM < half, -1.0, 1.0).astype(F32)
    return jnp.cos(ang), jnp.sin(ang) * sign[None, :]


def _windows_to_native(w):
    return jnp.transpose(w, (0, 2, 3, 1)).reshape(w.shape[0], KV_WIDTH, WINDOW)


def _windows_from_native(w):
    return jnp.transpose(w.reshape(w.shape[0], N_KV_HEADS, HEAD_DIM, WINDOW), (0, 3, 1, 2))[None]


def kernel(x_prompt, x_sample, cache_k_win, cache_v_win, attn_norm, w_in, q_norm, k_norm, sinks,
           sg_norm, sg_w, sg_b, attn_out_norm, sg_out_norm, w_o, ffn_norm, w_gate, w_up, w_down):
    assert w_in.shape[0] == 1, "single-layer step only"
    batch, seq, d_model = x_prompt.shape
    dec_batch, dec_seq, _ = x_sample.shape

    w_in_b = w_in[0].astype(BF16)
    sink_vec = sinks[0]
    sgw = sg_w[0][:, :CHUNK, :CHUNK]
    sgb = sg_b[0][:, :CHUNK]

    cos_p, sin_p = _rope_tables(jnp.arange(seq, dtype=jnp.int32))
    tm_s = min(ROW_TILE, dec_batch * dec_seq)
    cos_s, sin_s = _rope_tables(PAST_LEN + jnp.arange(tm_s, dtype=jnp.int32) % dec_seq)

    xp = x_prompt.reshape(batch * seq, d_model)
    q, k, v, u, g, wg, wu, kwin, vwin = _inproj(xp, attn_norm, w_in_b, q_norm, k_norm, sg_norm,
                                                cos_p, sin_p, BF16, seq,
                                                casts=(w_gate[0], w_up[0]))
    mix_p, wd, wo = _prompt_mix(sink_vec, q, k, v, u, g, sgw, sgb, attn_out_norm, sg_out_norm,
                                w_down[0], w_o[0], batch, seq)
    y_prompt = _tail(xp, mix_p, wo, ffn_norm, wg, wu, wd).reshape(x_prompt.shape)
    k_win_prompt = _windows_from_native(kwin)
    v_win_prompt = _windows_from_native(vwin)

    xs = x_sample.reshape(dec_batch * dec_seq, d_model)
    q, k, v, u, g = _inproj(xs, attn_norm, w_in_b, q_norm, k_norm, sg_norm, cos_s, sin_s, F32, None)
    mix_s, kw, vw = _sample_mix(sink_vec, q, k, v, _windows_to_native(cache_k_win[0]),
                                _windows_to_native(cache_v_win[0]), u, g, sgw, sgb,
                                attn_out_norm, sg_out_norm, dec_seq)
    y_sample = _tail(xs, mix_s, wo, ffn_norm, wg, wu, wd).reshape(x_sample.shape)
    k_win_sample = _windows_from_native(kw)
    v_win_sample = _windows_from_native(vw)
    sg_v_sample = g.reshape(1, dec_batch, dec_seq, SG_WIDTH)

    return (y_prompt, y_sample, k_win_prompt, v_win_prompt, k_win_sample, v_win_sample, sg_v_sample)
```

```python
import functools

import jax
import jax.numpy as jnp
from jax import lax
from jax.experimental import pallas as pl
from jax.experimental.pallas import tpu as pltpu

HEAD_DIM = 64
N_KV_HEADS = 4
Q_PER_KV = 4
N_Q_HEADS = N_KV_HEADS * Q_PER_KV
KV_WIDTH = N_KV_HEADS * HEAD_DIM
ATTN_WIDTH = N_Q_HEADS * HEAD_DIM
WINDOW = 128
N_SG_HEADS = 8
SG_HEAD_DIM = 128
SG_WIDTH = N_SG_HEADS * SG_HEAD_DIM
CHUNK = 128
PAST_LEN = 16384
ROPE_THETA = 10000.0
EPS = 1e-6

LANES = 128
VMEM_LIMIT_BYTES = 56 * 1024 * 1024

ROW_TILE = 512
INPROJ_SUB_BLOCKS = 4
FF_TILE = 512
WEIGHT_SLOTS = 3
SEQS_PER_STEP = 16

F32 = jnp.float32
BF16 = jnp.bfloat16


def _rms(x, gain_row):
    r = lax.rsqrt(jnp.mean(x * x, axis=-1, keepdims=True) + EPS)
    return x * r * gain_row


def _swapped_head(c):
    return Q_PER_KV * (c % N_KV_HEADS) + c // N_KV_HEADS


def _swap_heads(t):
    lane = lax.broadcasted_iota(jnp.int32, (t.shape[0], LANES), 1)
    lo = lane < HEAD_DIM
    outs = []
    for j in range(ATTN_WIDTH // LANES):
        halves = []
        for half in range(2):
            a = _swapped_head(2 * j + half)
            src = t[:, LANES * (a // 2):LANES * (a // 2 + 1)]
            halves.append(src if a % 2 == half else pltpu.roll(src, HEAD_DIM, 1))
        outs.append(jnp.where(lo, halves[0], halves[1]))
    return jnp.concatenate(outs, axis=1)


def _head_norm_rope(t, gain_row, cos, sin_signed):
    lane = lax.broadcasted_iota(jnp.int32, (t.shape[0], LANES), 1)
    lo_head = lane < HEAD_DIM
    first_half = (lane % HEAD_DIM) < (HEAD_DIM // 2)
    outs = []
    for j in range(t.shape[1] // LANES):
        blk = t[:, LANES * j:LANES * (j + 1)]
        sq = blk * blk
        s_all = jnp.sum(sq, axis=-1, keepdims=True)
        s_lo = jnp.sum(jnp.where(lo_head, sq, 0.0), axis=-1, keepdims=True)
        s_hi = s_all - s_lo
        r = jnp.where(lo_head,
                      lax.rsqrt(s_lo * (1.0 / HEAD_DIM) + EPS),
                      lax.rsqrt(s_hi * (1.0 / HEAD_DIM) + EPS))
        y = blk * r * gain_row
        rot = jnp.where(first_half,
                        pltpu.roll(y, LANES - HEAD_DIM // 2, 1),
                        pltpu.roll(y, HEAD_DIM // 2, 1))
        outs.append(y * cos + rot * sin_signed)
    return jnp.concatenate(outs, axis=1)


def _inproj_body(x_ref, an_ref, w_ref, qn_ref, kn_ref, sgn_ref, cos_ref, sin_ref, *rest,
                 tiles_per_seq, n_cast):
    casts_in, rest = rest[:n_cast], rest[n_cast:]
    q_ref, k_ref, v_ref, u_ref, g_ref = rest[:5]
    casts_out, rest = rest[5:5 + n_cast], rest[5 + n_cast:]
    proj_scr = rest[-1]
    n_sub, sub = proj_scr.shape[0], proj_scr.shape[1]
    q_gain = jnp.concatenate([qn_ref[...]] * (LANES // HEAD_DIM), axis=1)
    k_gain = jnp.concatenate([kn_ref[...]] * (LANES // HEAD_DIM), axis=1)
    c0, c1, c2, c3 = ATTN_WIDTH, ATTN_WIDTH + KV_WIDTH, ATTN_WIDTH + 2 * KV_WIDTH, \
        ATTN_WIDTH + 2 * KV_WIDTH + SG_WIDTH

    def project(r):
        xn = _rms(x_ref[sub * r:sub * (r + 1), :], an_ref[...]).astype(BF16)
        proj_scr[r] = jnp.dot(xn, w_ref[...], preferred_element_type=F32)

    def finish(r):
        rows = slice(sub * r, sub * (r + 1))
        cos = cos_ref[rows, :]
        sin = sin_ref[rows, :]
        q = _head_norm_rope(proj_scr[r, :, 0:c0], q_gain, cos, sin) * (HEAD_DIM ** -0.5)
        q_ref[rows, :] = _swap_heads(q).astype(q_ref.dtype)
        k = _head_norm_rope(proj_scr[r, :, c0:c1], k_gain, cos, sin)
        k_ref[rows, :] = k.astype(k_ref.dtype)
        v = proj_scr[r, :, c1:c2]
        v_ref[rows, :] = v.astype(v_ref.dtype)
        u_ref[rows, :] = jax.nn.gelu(proj_scr[r, :, c2:c3]).astype(u_ref.dtype)
        g = jax.nn.gelu(proj_scr[r, :, c3:])
        g_ref[rows, :] = _rms(g, sgn_ref[...]).astype(g_ref.dtype)
        return k, v

    project(0)
    for src, dst in zip(casts_in, casts_out):
        dst[...] = src[...].astype(BF16)
    for r in range(1, n_sub):
        project(r)
        finish(r - 1)
    k, v = finish(n_sub - 1)

    if len(rest) > 1:
        kwin_ref, vwin_ref = rest[0], rest[1]

        @pl.when(pl.program_id(0) % tiles_per_seq == tiles_per_seq - 1)
        def _():
            kwin_ref[0] = k[sub - WINDOW:].T
            vwin_ref[0] = v[sub - WINDOW:].T


def _inproj(x, attn_norm, w_in, q_norm, k_norm, sg_norm, cos_t, sin_t, act_dtype, seq, casts=()):
    rows, d_model = x.shape
    tm = min(ROW_TILE, rows)
    steps = rows // tm
    n_tab = cos_t.shape[0] // tm
    const = lambda i: (0, 0)
    row = lambda i: (i, 0)
    tab = lambda i: (i % n_tab, 0)
    slabs = [pl.BlockSpec((w.shape[0] // steps, w.shape[1]), row) for w in casts]
    out_specs = [
        pl.BlockSpec((tm, ATTN_WIDTH), row),
        pl.BlockSpec((tm, KV_WIDTH), row),
        pl.BlockSpec((tm, KV_WIDTH), row),
        pl.BlockSpec((tm, SG_WIDTH), row),
        pl.BlockSpec((tm, SG_WIDTH), row),
    ]
    out_shape = [
        jax.ShapeDtypeStruct((rows, ATTN_WIDTH), act_dtype),
        jax.ShapeDtypeStruct((rows, KV_WIDTH), act_dtype),
        jax.ShapeDtypeStruct((rows, KV_WIDTH), act_dtype),
        jax.ShapeDtypeStruct((rows, SG_WIDTH), BF16),
        jax.ShapeDtypeStruct((rows, SG_WIDTH), act_dtype),
    ]
    out_specs += slabs
    out_shape += [jax.ShapeDtypeStruct(w.shape, BF16) for w in casts]
    tiles_per_seq = 1
    if seq is not None:
        tiles_per_seq = seq // tm
        win = lambda i: (i // tiles_per_seq, 0, 0)
        out_specs += [pl.BlockSpec((1, KV_WIDTH, WINDOW), win)] * 2
        out_shape += [jax.ShapeDtypeStruct((rows // seq, KV_WIDTH, WINDOW), F32)] * 2
    return pl.pallas_call(
        functools.partial(_inproj_body, tiles_per_seq=tiles_per_seq, n_cast=len(casts)),
        grid=(steps,),
        in_specs=[
            pl.BlockSpec((tm, d_model), row),
            pl.BlockSpec((1, d_model), const),
            pl.BlockSpec(w_in.shape, const, pipeline_mode=pl.Buffered(1)),
            pl.BlockSpec((1, HEAD_DIM), const),
            pl.BlockSpec((1, HEAD_DIM), const),
            pl.BlockSpec((1, SG_WIDTH), const),
            pl.BlockSpec((tm, LANES), tab),
            pl.BlockSpec((tm, LANES), tab),
        ] + slabs,
        out_specs=out_specs,
        out_shape=out_shape,
        scratch_shapes=[pltpu.VMEM((INPROJ_SUB_BLOCKS, tm // INPROJ_SUB_BLOCKS, w_in.shape[1]), F32)],
        compiler_params=pltpu.CompilerParams(
            dimension_semantics=("arbitrary",), vmem_limit_bytes=VMEM_LIMIT_BYTES),
        name="inproj",
    )(x, attn_norm, w_in, q_norm, k_norm, sg_norm, cos_t, sin_t, *casts)


def _softmax_sink(s, mask, sink):
    s = jnp.where(mask, s, -jnp.inf)
    m = jnp.maximum(jnp.max(s, axis=-1, keepdims=True), sink)
    p = jnp.exp(s - m)
    denom = jnp.sum(p, axis=-1, keepdims=True) + jnp.exp(sink - m)
    return p, 1.0 / denom


def _spatial_gate(u, g, w_ref, b_ref, rows_per_seq):
    r = lax.broadcasted_iota(jnp.int32, (CHUNK, CHUNK), 0)
    c = lax.broadcasted_iota(jnp.int32, (CHUNK, CHUNK), 1)
    causal = (r // rows_per_seq == c // rows_per_seq) & (c % rows_per_seq <= r % rows_per_seq)
    gb = g.astype(BF16)
    b_t = b_ref[...].T[0:rows_per_seq, :]
    b_t = jnp.concatenate([b_t] * (CHUNK // rows_per_seq), axis=0)
    periodic = ((r < rows_per_seq) & (c % rows_per_seq == r)).astype(BF16)
    outs = []
    for h in range(N_SG_HEADS):
        if rows_per_seq == CHUNK:
            w = w_ref[h]
        else:
            w = jnp.dot(w_ref[h, 0:rows_per_seq, :].astype(BF16), periodic, preferred_element_type=F32)
            w = jnp.concatenate([w] * (CHUNK // rows_per_seq), axis=0)
        w = jnp.where(causal, w, 0.0).astype(BF16)
        cols = slice(SG_HEAD_DIM * h, SG_HEAD_DIM * (h + 1))
        bias = jnp.broadcast_to(b_t[:, h:h + 1], (CHUNK, SG_HEAD_DIM))
        parts = []
        for ch in range(u.shape[0] // CHUNK):
            rws = slice(CHUNK * ch, CHUNK * (ch + 1))
            mixed = jnp.dot(w, gb[rws, cols], preferred_element_type=F32) + bias
            parts.append(u[rws, cols].astype(F32) * mixed)
        outs.append(jnp.concatenate(parts, axis=0))
    return jnp.concatenate(outs, axis=1)


def _normed_mix(a_swapped, sgo, aon, son):
    return jnp.concatenate([_rms(a_swapped, _swap_heads(aon)), _rms(sgo, son)], axis=1).astype(BF16)


def _prompt_mix_body(sinks_ref, q_ref, k_ref, v_ref, u_ref, g_ref, sgw_ref, bias_ref,
                     aon_ref, son_ref, wd32, wo32_lo, wo32_hi, mix_ref, wd16, wo16, a_scr):
    wd16[...] = wd32[...].astype(BF16)
    wo16[0:HEAD_DIM, :] = wo32_lo[...].astype(BF16)
    wo16[HEAD_DIM:2 * HEAD_DIM, :] = wo32_hi[...].astype(BF16)

    j = pl.program_id(1)
    blocks = q_ref.shape[0] // WINDOW
    quarter = lax.broadcasted_iota(jnp.int32, (WINDOW, KV_WIDTH), 1) // HEAD_DIM
    qi = lax.broadcasted_iota(jnp.int32, (Q_PER_KV * WINDOW, WINDOW), 0) % WINDOW
    kj = lax.broadcasted_iota(jnp.int32, (Q_PER_KV * WINDOW, WINDOW), 1)
    from_prev = kj > qi
    row_g = lax.broadcasted_iota(jnp.int32, (Q_PER_KV * WINDOW, 1), 0) // WINDOW
    sink_cols = []
    for h in range(N_KV_HEADS):
        col = jnp.zeros((Q_PER_KV * WINDOW, 1), F32)
        for g in range(Q_PER_KV):
            col = jnp.where(row_g == g, sinks_ref[Q_PER_KV * h + g], col)
        sink_cols.append(col)

    def block(r, carry):
        n = j * blocks + r
        prev = pl.multiple_of(jnp.maximum(n - 1, 0) * WINDOW, WINDOW)
        cur = pl.multiple_of(n * WINDOW, WINDOW)
        no_prev = jnp.where(n > 0, 0.0, -jnp.inf)
        q = q_ref[pl.ds(pl.multiple_of(r * WINDOW, WINDOW), WINDOW), :]
        kb = jnp.concatenate([k_ref[pl.ds(prev, WINDOW), :], k_ref[pl.ds(cur, WINDOW), :]], axis=0)
        vb = jnp.concatenate([v_ref[pl.ds(prev, WINDOW), :], v_ref[pl.ds(cur, WINDOW), :]], axis=0)
        groups = None
        for h in range(N_KV_HEADS):
            sel = quarter == h
            lhs = jnp.concatenate(
                [jnp.where(sel, q[:, KV_WIDTH * g:KV_WIDTH * (g + 1)], 0) for g in range(Q_PER_KV)],
                axis=0)
            s = lax.dot_general(lhs, kb, (((1,), (1,)), ((), ())), preferred_element_type=F32)
            s = jnp.where(from_prev, s[:, :WINDOW] + no_prev, s[:, WINDOW:])
            sink = sink_cols[h]
            m = jnp.maximum(jnp.max(s, axis=-1, keepdims=True), sink)
            p = jnp.exp(s - m)
            inv = 1.0 / (jnp.sum(p, axis=-1, keepdims=True) + jnp.exp(sink - m))
            p_band = jnp.concatenate(
                [jnp.where(from_prev, p, 0.0), jnp.where(from_prev, 0.0, p)], axis=1).astype(BF16)
            o = jnp.dot(p_band, vb, preferred_element_type=F32) * inv
            parts = [o[WINDOW * g:WINDOW * (g + 1)] for g in range(Q_PER_KV)]
            groups = parts if groups is None else [
                jnp.where(sel, part, acc) for part, acc in zip(parts, groups)]
        a_scr[pl.ds(pl.multiple_of(r * WINDOW, WINDOW), WINDOW), :] = jnp.concatenate(groups, axis=1)
        return carry

    lax.fori_loop(0, blocks, block, 0, unroll=2)

    sgo = _spatial_gate(u_ref[...], g_ref[...], sgw_ref, bias_ref, CHUNK)
    mix_ref[...] = _normed_mix(a_scr[...], sgo, aon_ref[...], son_ref[...])


def _prompt_mix(sinks, q, k, v, u, g, sg_w, bias, aon, son, wd, wo, batch, seq):
    tm = ROW_TILE
    tiles = seq // tm
    steps = batch * tiles
    d_model = wo.shape[1]
    assert wo.shape[0] == 2 * HEAD_DIM * steps
    row = lambda b, j: (b * tiles + j, 0)
    per_seq = lambda b, j: (b, 0)
    const2 = lambda b, j: (0, 0)
    const3 = lambda b, j: (0, 0, 0)
    slab = lambda w: pl.BlockSpec((w.shape[0] // steps, w.shape[1]), row)
    bf16_like = lambda w: jax.ShapeDtypeStruct(w.shape, BF16)

    def wo_block(half):
        def index(b, j):
            c = 2 * (b * tiles + j) + half
            return jnp.where(c < N_Q_HEADS, _swapped_head(c), c), 0
        return pl.BlockSpec((HEAD_DIM, d_model), index)

    return pl.pallas_call(
        _prompt_mix_body,
        grid=(batch, tiles),
        in_specs=[
            pl.BlockSpec(memory_space=pltpu.SMEM),
            pl.BlockSpec((tm, ATTN_WIDTH), row),
            pl.BlockSpec((seq, KV_WIDTH), per_seq),
            pl.BlockSpec((seq, KV_WIDTH), per_seq),
            pl.BlockSpec((tm, SG_WIDTH), row),
            pl.BlockSpec((tm, SG_WIDTH), row),
            pl.BlockSpec(sg_w.shape, const3),
            pl.BlockSpec(bias.shape, const2),
            pl.BlockSpec((1, ATTN_WIDTH), const2),
            pl.BlockSpec((1, SG_WIDTH), const2),
            slab(wd), wo_block(0), wo_block(1),
        ],
        out_specs=[pl.BlockSpec((tm, d_model), row), slab(wd), slab(wo)],
        out_shape=[jax.ShapeDtypeStruct((batch * seq, d_model), BF16), bf16_like(wd), bf16_like(wo)],
        scratch_shapes=[pltpu.VMEM((tm, ATTN_WIDTH), F32)],
        compiler_params=pltpu.CompilerParams(
            dimension_semantics=("arbitrary", "arbitrary"), vmem_limit_bytes=VMEM_LIMIT_BYTES),
        name="prompt_mix",
    )(sinks, q, k, v, u, g, sg_w, bias, aon, son, wd, wo, wo)


def _sample_mix_body(sinks_ref, q_ref, k_ref, v_ref, ck_ref, cv_ref, u_ref, g_ref, sgw_ref,
                     bias_ref, aon_ref, son_ref, mix_ref, kw_ref, vw_ref, a_scr, *, dec_seq):
    rows = N_Q_HEADS * dec_seq
    quarter = lax.broadcasted_iota(jnp.int32, (dec_seq, KV_WIDTH), 1) // HEAD_DIM
    t = lax.broadcasted_iota(jnp.int32, (rows, 2 * WINDOW), 0) % dec_seq
    kj = lax.broadcasted_iota(jnp.int32, (rows, 2 * WINDOW), 1)
    pos_lane = lax.broadcasted_iota(jnp.int32, (KV_WIDTH, WINDOW), 1)
    row_head = lax.broadcasted_iota(jnp.int32, (rows, 1), 0) // dec_seq
    sink = jnp.zeros((rows, 1), F32)
    for a in range(N_Q_HEADS):
        sink = jnp.where(row_head == a, sinks_ref[a], sink)
    k_new_t = k_ref[...].T
    v_new_t = v_ref[...].T

    def one_seq(b, carry):
        r0 = pl.multiple_of(b * dec_seq, dec_seq)
        q = q_ref[pl.ds(r0, dec_seq), :]
        ck = ck_ref[b]
        cv = cv_ref[b]
        own = kj - WINDOW - b * dec_seq
        mask = ((kj < WINDOW) & (kj > t)) | ((own >= 0) & (own <= t))
        lhs = jnp.concatenate(
            [jnp.where(quarter == h, q[:, KV_WIDTH * g:KV_WIDTH * (g + 1)], 0.0)
             for h in range(N_KV_HEADS) for g in range(Q_PER_KV)], axis=0).astype(BF16)
        k_all = jnp.concatenate([ck, k_new_t], axis=1).astype(BF16)
        v_all = jnp.concatenate([cv, v_new_t], axis=1).astype(BF16)
        s = jnp.dot(lhs, k_all, preferred_element_type=F32)
        p, inv = _softmax_sink(s, mask, sink)
        o = lax.dot_general(p.astype(BF16), v_all, (((1,), (1,)), ((), ())),
                            preferred_element_type=F32) * inv
        groups = []
        for g in range(Q_PER_KV):
            piece = lambda h: o[(Q_PER_KV * h + g) * dec_seq:(Q_PER_KV * h + g + 1) * dec_seq]
            acc = piece(N_KV_HEADS - 1)
            for h in range(N_KV_HEADS - 2, -1, -1):
                acc = jnp.where(quarter == h, piece(h), acc)
            groups.append(acc)
        a_scr[pl.ds(r0, dec_seq), :] = jnp.concatenate(groups, axis=1)
        return carry

    lax.fori_loop(0, ck_ref.shape[0], one_seq, 0, unroll=4)

    keep = pos_lane < WINDOW - dec_seq
    for b in range(ck_ref.shape[0]):
        shift = WINDOW - dec_seq - b * dec_seq
        kw_ref[b] = jnp.where(keep, pltpu.roll(ck_ref[b], WINDOW - dec_seq, 1),
                              pltpu.roll(k_new_t, shift, 1))
        vw_ref[b] = jnp.where(keep, pltpu.roll(cv_ref[b], WINDOW - dec_seq, 1),
                              pltpu.roll(v_new_t, shift, 1))

    sgo = _spatial_gate(u_ref[...], g_ref[...], sgw_ref, bias_ref, dec_seq)
    mix_ref[...] = _normed_mix(a_scr[...], sgo, aon_ref[...], son_ref[...])


def _sample_mix(sinks, q, k, v, ck, cv, u, g, sg_w, bias, aon, son, dec_seq):
    nseq = ck.shape[0]
    sb = SEQS_PER_STEP
    tm = sb * dec_seq
    assert tm == WINDOW == CHUNK and dec_seq % 8 == 0
    row = lambda i: (i, 0)
    seq3 = lambda i: (i, 0, 0)
    const2 = lambda i: (0, 0)
    const3 = lambda i: (0, 0, 0)
    return pl.pallas_call(
        functools.partial(_sample_mix_body, dec_seq=dec_seq),
        grid=(nseq // sb,),
        in_specs=[
            pl.BlockSpec(memory_space=pltpu.SMEM),
            pl.BlockSpec((tm, ATTN_WIDTH), row),
            pl.BlockSpec((tm, KV_WIDTH), row),
            pl.BlockSpec((tm, KV_WIDTH), row),
            pl.BlockSpec((sb, KV_WIDTH, WINDOW), seq3),
            pl.BlockSpec((sb, KV_WIDTH, WINDOW), seq3),
            pl.BlockSpec((tm, SG_WIDTH), row),
            pl.BlockSpec((tm, SG_WIDTH), row),
            pl.BlockSpec(sg_w.shape, const3),
            pl.BlockSpec(bias.shape, const2),
            pl.BlockSpec((1, ATTN_WIDTH), const2),
            pl.BlockSpec((1, SG_WIDTH), const2),
        ],
        out_specs=[
            pl.BlockSpec((tm, ATTN_WIDTH + SG_WIDTH), row),
            pl.BlockSpec((sb, KV_WIDTH, WINDOW), seq3),
            pl.BlockSpec((sb, KV_WIDTH, WINDOW), seq3),
        ],
        out_shape=[
            jax.ShapeDtypeStruct((nseq * dec_seq, ATTN_WIDTH + SG_WIDTH), BF16),
            jax.ShapeDtypeStruct(ck.shape, F32),
            jax.ShapeDtypeStruct(cv.shape, F32),
        ],
        scratch_shapes=[pltpu.VMEM((tm, ATTN_WIDTH), F32)],
        compiler_params=pltpu.CompilerParams(
            dimension_semantics=("arbitrary",), vmem_limit_bytes=VMEM_LIMIT_BYTES),
        name="sample_mix",
    )(sinks, q, k, v, ck, cv, u, g, sg_w, bias, aon, son)


def _tail_body(x_hbm, mix_hbm, wo_ref, fn_ref, wg_hbm, wu_hbm, wd_hbm, y_ref,
               hn_scr, wg_buf, wu_buf, wd_buf, sems, x_buf, mix_buf, row_sems, *, n_tiles, n_chunks):
    n_steps = n_tiles * n_chunks
    tf = wg_buf.shape[2]
    depth = WEIGHT_SLOTS - 1
    step = pl.program_id(0) * n_chunks + pl.program_id(1)

    def chunk_copies(s):
        slot = s % WEIGHT_SLOTS
        cols = pl.ds(pl.multiple_of((s % n_chunks) * tf, tf), tf)
        return (
            pltpu.make_async_copy(wg_hbm.at[:, cols], wg_buf.at[slot], sems.at[0, slot]),
            pltpu.make_async_copy(wu_hbm.at[:, cols], wu_buf.at[slot], sems.at[1, slot]),
            pltpu.make_async_copy(wd_hbm.at[cols, :], wd_buf.at[slot], sems.at[2, slot]),
        )

    tile = pl.program_id(0)
    tm = x_buf.shape[1]

    def row_copies(t):
        rows = pl.ds(pl.multiple_of(t * tm, tm), tm)
        return (
            pltpu.make_async_copy(x_hbm.at[rows, :], x_buf.at[t % 2], row_sems.at[0, t % 2]),
            pltpu.make_async_copy(mix_hbm.at[rows, :], mix_buf.at[t % 2], row_sems.at[1, t % 2]),
        )

    @pl.when(step == 0)
    def _():
        for copy in row_copies(0):
            copy.start()
        for s in range(depth):
            for copy in chunk_copies(s):
                copy.start()

    @pl.when((pl.program_id(1) == 1) & (tile + 1 < n_tiles))
    def _():
        for copy in row_copies(tile + 1):
            copy.start()

    @pl.when(step + depth < n_steps)
    def _():
        for copy in chunk_copies(step + depth):
            copy.start()

    for copy in chunk_copies(step):
        copy.wait()
    slot = step % WEIGHT_SLOTS

    def ffn_chunk():
        hn = hn_scr[...]
        gate = jnp.dot(hn, wg_buf[slot], preferred_element_type=F32)
        up = jnp.dot(hn, wu_buf[slot], preferred_element_type=F32)
        act = (jax.nn.silu(gate) * up).astype(BF16)
        y_ref[...] += jnp.dot(act, wd_buf[slot], preferred_element_type=F32)

    @pl.when(pl.program_id(1) == 0)
    def _():
        for copy in row_copies(tile):
            copy.wait()
        half = tm // 2
        for rows in (slice(0, half), slice(half, 2 * half)):
            y_ref[rows, :] = x_buf[tile % 2, rows, :] + jnp.dot(
                mix_buf[tile % 2, rows, :], wo_ref[...], preferred_element_type=F32)
        for rows in (slice(0, half), slice(half, 2 * half)):
            hn_scr[rows, :] = _rms(y_ref[rows, :], fn_ref[...]).astype(BF16)
        ffn_chunk()

    @pl.when(pl.program_id(1) > 0)
    def _():
        ffn_chunk()


def _tail(x, mix, wo, ffn_norm, wg, wu, wd):
    rows, d_model = x.shape
    d_ff = wg.shape[1]
    tm = min(ROW_TILE, rows)
    tf = FF_TILE
    n_tiles, n_chunks = rows // tm, d_ff // tf
    assert n_tiles * n_chunks >= WEIGHT_SLOTS and n_chunks >= 2
    return pl.pallas_call(
        functools.partial(_tail_body, n_tiles=n_tiles, n_chunks=n_chunks),
        grid=(n_tiles, n_chunks),
        in_specs=[
            pl.BlockSpec(memory_space=pltpu.HBM),
            pl.BlockSpec(memory_space=pltpu.HBM),
            pl.BlockSpec(wo.shape, lambda i, j: (0, 0), pipeline_mode=pl.Buffered(1)),
            pl.BlockSpec((1, d_model), lambda i, j: (0, 0)),
            pl.BlockSpec(memory_space=pltpu.HBM),
            pl.BlockSpec(memory_space=pltpu.HBM),
            pl.BlockSpec(memory_space=pltpu.HBM),
        ],
        out_specs=pl.BlockSpec((tm, d_model), lambda i, j: (i, 0)),
        out_shape=jax.ShapeDtypeStruct(x.shape, F32),
        scratch_shapes=[
            pltpu.VMEM((tm, d_model), BF16),
            pltpu.VMEM((WEIGHT_SLOTS, d_model, tf), BF16),
            pltpu.VMEM((WEIGHT_SLOTS, d_model, tf), BF16),
            pltpu.VMEM((WEIGHT_SLOTS, tf, d_model), BF16),
            pltpu.SemaphoreType.DMA((3, WEIGHT_SLOTS)),
            pltpu.VMEM((2, tm, d_model), x.dtype),
            pltpu.VMEM((2, tm, mix.shape[1]), mix.dtype),
            pltpu.SemaphoreType.DMA((2, 2)),
        ],
        compiler_params=pltpu.CompilerParams(
            dimension_semantics=("arbitrary", "arbitrary"), vmem_limit_bytes=VMEM_LIMIT_BYTES),
        name="tail",
    )(x, mix, wo, ffn_norm, wg, wu, wd)


def _rope_tables(pos):
    half = HEAD_DIM // 2
    lane = jnp.arange(LANES)
    inv = ROPE_THETA ** (-(lane % half).astype(F32) / half)
    ang = pos.astype(F32)[:, None] * inv[None, :]
    sign = jnp.where(lane % HEAD_DIM < half, -1.0, 1.0).astype(F32)
    return jnp.cos(ang), jnp.sin(ang) * sign[None, :]


def _windows_to_native(w):
    return jnp.transpose(w, (0, 2, 3, 1)).reshape(w.shape[0], KV_WIDTH, WINDOW)


def _windows_from_native(w):
    return jnp.transpose(w.reshape(w.shape[0], N_KV_HEADS, HEAD_DIM, WINDOW), (0, 3, 1, 2))[None]


def kernel(x_prompt, x_sample, cache_k_win, cache_v_win, attn_norm, w_in, q_norm, k_norm, sinks,
           sg_norm, sg_w, sg_b, attn_out_norm, sg_out_norm, w_o, ffn_norm, w_gate, w_up, w_down):
    assert w_in.shape[0] == 1, "single-layer step only"
    batch, seq, d_model = x_prompt.shape
    dec_batch, dec_seq, _ = x_sample.shape

    w_in_b = w_in[0].astype(BF16)
    sink_vec = sinks[0]
    sgw = sg_w[0][:, :CHUNK, :CHUNK]
    sgb = sg_b[0][:, :CHUNK]

    cos_p, sin_p = _rope_tables(jnp.arange(seq, dtype=jnp.int32))
    tm_s = min(ROW_TILE, dec_batch * dec_seq)
    cos_s, sin_s = _rope_tables(PAST_LEN + jnp.arange(tm_s, dtype=jnp.int32) % dec_seq)

    xp = x_prompt.reshape(batch * seq, d_model)
    q, k, v, u, g, wg, wu, kwin, vwin = _inproj(xp, attn_norm, w_in_b, q_norm, k_norm, sg_norm,
                                                cos_p, sin_p, BF16, seq,
                                                casts=(w_gate[0], w_up[0]))
    mix_p, wd, wo = _prompt_mix(sink_vec, q, k, v, u, g, sgw, sgb, attn_out_norm, sg_out_norm,
                                w_down[0], w_o[0], batch, seq)
    y_prompt = _tail(xp, mix_p, wo, ffn_norm, wg, wu, wd).reshape(x_prompt.shape)
    k_win_prompt = _windows_from_native(kwin)
    v_win_prompt = _windows_from_native(vwin)

    xs = x_sample.reshape(dec_batch * dec_seq, d_model)
    q, k, v, u, g = _inproj(xs, attn_norm, w_in_b, q_norm, k_norm, sg_norm, cos_s, sin_s, F32, None)
    mix_s, kw, vw = _sample_mix(sink_vec, q, k, v, _windows_to_native(cache_k_win[0]),
                                _windows_to_native(cache_v_win[0]), u, g, sgw, sgb,
                                attn_out_norm, sg_out_norm, dec_seq)
    y_sample = _tail(xs, mix_s, wo, ffn_norm, wg, wu, wd).reshape(x_sample.shape)
    k_win_sample = _windows_from_native(kw)
    v_win_sample = _windows_from_native(vw)
    sg_v_sample = g.reshape(1, dec_batch, dec_seq, SG_WIDTH)

    return (y_prompt, y_sample, k_win_prompt, v_win_prompt, k_win_sample, v_win_sample, sg_v_sample)
```

```python
import functools

import jax
import jax.numpy as jnp
from jax import lax
from jax.experimental import pallas as pl
from jax.experimental.pallas import tpu as pltpu

HEAD_DIM = 64
N_KV_HEADS = 4
Q_PER_KV = 4
N_Q_HEADS = N_KV_HEADS * Q_PER_KV
KV_WIDTH = N_KV_HEADS * HEAD_DIM
ATTN_WIDTH = N_Q_HEADS * HEAD_DIM
WINDOW = 128
N_SG_HEADS = 8
SG_HEAD_DIM = 128
SG_WIDTH = N_SG_HEADS * SG_HEAD_DIM
CHUNK = 128
PAST_LEN = 16384
ROPE_THETA = 10000.0
EPS = 1e-6

LANES = 128
VMEM_LIMIT_BYTES = 56 * 1024 * 1024

ROW_TILE = 512
INPROJ_SUB_BLOCKS = 4
FF_TILE = 512
WEIGHT_SLOTS = 3
RING_DMA_PRIORITY = 1
SEQS_PER_STEP = 16

F32 = jnp.float32
BF16 = jnp.bfloat16


def _rms(x, gain_row):
    r = lax.rsqrt(jnp.mean(x * x, axis=-1, keepdims=True) + EPS)
    return x * r * gain_row


def _swapped_head(c):
    return Q_PER_KV * (c % N_KV_HEADS) + c // N_KV_HEADS


def _swap_heads(t):
    lane = lax.broadcasted_iota(jnp.int32, (t.shape[0], LANES), 1)
    lo = lane < HEAD_DIM
    outs = []
    for j in range(ATTN_WIDTH // LANES):
        halves = []
        for half in range(2):
            a = _swapped_head(2 * j + half)
            src = t[:, LANES * (a // 2):LANES * (a // 2 + 1)]
            halves.append(src if a % 2 == half else pltpu.roll(src, HEAD_DIM, 1))
        outs.append(jnp.where(lo, halves[0], halves[1]))
    return jnp.concatenate(outs, axis=1)


def _head_norm_rope(t, gain_row, cos, sin_signed):
    lane = lax.broadcasted_iota(jnp.int32, (t.shape[0], LANES), 1)
    lo_head = lane < HEAD_DIM
    first_half = (lane % HEAD_DIM) < (HEAD_DIM // 2)
    outs = []
    for j in range(t.shape[1] // LANES):
        blk = t[:, LANES * j:LANES * (j + 1)]
        sq = blk * blk
        s_all = jnp.sum(sq, axis=-1, keepdims=True)
        s_lo = jnp.sum(jnp.where(lo_head, sq, 0.0), axis=-1, keepdims=True)
        s_hi = s_all - s_lo
        r = jnp.where(lo_head,
                      lax.rsqrt(s_lo * (1.0 / HEAD_DIM) + EPS),
                      lax.rsqrt(s_hi * (1.0 / HEAD_DIM) + EPS))
        y = blk * r * gain_row
        rot = jnp.where(first_half,
                        pltpu.roll(y, LANES - HEAD_DIM // 2, 1),
                        pltpu.roll(y, HEAD_DIM // 2, 1))
        outs.append(y * cos + rot * sin_signed)
    return jnp.concatenate(outs, axis=1)


def _inproj_body(x_ref, an_ref, w_ref, qn_ref, kn_ref, sgn_ref, cos_ref, sin_ref, *rest,
                 tiles_per_seq, n_cast):
    casts_in, rest = rest[:n_cast], rest[n_cast:]
    q_ref, k_ref, v_ref, u_ref, g_ref = rest[:5]
    casts_out, rest = rest[5:5 + n_cast], rest[5 + n_cast:]
    proj_scr = rest[-1]
    n_sub, sub = proj_scr.shape[0], proj_scr.shape[1]
    q_gain = jnp.concatenate([qn_ref[...]] * (LANES // HEAD_DIM), axis=1)
    k_gain = jnp.concatenate([kn_ref[...]] * (LANES // HEAD_DIM), axis=1)
    c0, c1, c2, c3 = ATTN_WIDTH, ATTN_WIDTH + KV_WIDTH, ATTN_WIDTH + 2 * KV_WIDTH, \
        ATTN_WIDTH + 2 * KV_WIDTH + SG_WIDTH

    def project(r):
        xn = _rms(x_ref[sub * r:sub * (r + 1), :], an_ref[...]).astype(BF16)
        proj_scr[r] = jnp.dot(xn, w_ref[...], preferred_element_type=F32)

    def finish(r):
        rows = slice(sub * r, sub * (r + 1))
        cos = cos_ref[rows, :]
        sin = sin_ref[rows, :]
        q = _head_norm_rope(proj_scr[r, :, 0:c0], q_gain, cos, sin) * (HEAD_DIM ** -0.5)
        q_ref[rows, :] = _swap_heads(q).astype(q_ref.dtype)
        k = _head_norm_rope(proj_scr[r, :, c0:c1], k_gain, cos, sin)
        k_ref[rows, :] = k.astype(k_ref.dtype)
        v = proj_scr[r, :, c1:c2]
        v_ref[rows, :] = v.astype(v_ref.dtype)
        u_ref[rows, :] = jax.nn.gelu(proj_scr[r, :, c2:c3]).astype(u_ref.dtype)
        g = jax.nn.gelu(proj_scr[r, :, c3:])
        g_ref[rows, :] = _rms(g, sgn_ref[...]).astype(g_ref.dtype)
        return k, v

    project(0)
    for src, dst in zip(casts_in, casts_out):
        dst[...] = src[...].astype(BF16)
    for r in range(1, n_sub):
        project(r)
        finish(r - 1)
    k, v = finish(n_sub - 1)

    if len(rest) > 1:
        kwin_ref, vwin_ref = rest[0], rest[1]

        @pl.when(pl.program_id(0) % tiles_per_seq == tiles_per_seq - 1)
        def _():
            kwin_ref[0] = k[sub - WINDOW:].T
            vwin_ref[0] = v[sub - WINDOW:].T


def _inproj(x, attn_norm, w_in, q_norm, k_norm, sg_norm, cos_t, sin_t, act_dtype, seq, casts=()):
    rows, d_model = x.shape
    tm = min(ROW_TILE, rows)
    steps = rows // tm
    n_tab = cos_t.shape[0] // tm
    const = lambda i: (0, 0)
    row = lambda i: (i, 0)
    tab = lambda i: (i % n_tab, 0)
    slabs = [pl.BlockSpec((w.shape[0] // steps, w.shape[1]), row) for w in casts]
    out_specs = [
        pl.BlockSpec((tm, ATTN_WIDTH), row),
        pl.BlockSpec((tm, KV_WIDTH), row),
        pl.BlockSpec((tm, KV_WIDTH), row),
        pl.BlockSpec((tm, SG_WIDTH), row),
        pl.BlockSpec((tm, SG_WIDTH), row),
    ]
    out_shape = [
        jax.ShapeDtypeStruct((rows, ATTN_WIDTH), act_dtype),
        jax.ShapeDtypeStruct((rows, KV_WIDTH), act_dtype),
        jax.ShapeDtypeStruct((rows, KV_WIDTH), act_dtype),
        jax.ShapeDtypeStruct((rows, SG_WIDTH), BF16),
        jax.ShapeDtypeStruct((rows, SG_WIDTH), act_dtype),
    ]
    out_specs += slabs
    out_shape += [jax.ShapeDtypeStruct(w.shape, BF16) for w in casts]
    tiles_per_seq = 1
    if seq is not None:
        tiles_per_seq = seq // tm
        win = lambda i: (i // tiles_per_seq, 0, 0)
        out_specs += [pl.BlockSpec((1, KV_WIDTH, WINDOW), win)] * 2
        out_shape += [jax.ShapeDtypeStruct((rows // seq, KV_WIDTH, WINDOW), F32)] * 2
    return pl.pallas_call(
        functools.partial(_inproj_body, tiles_per_seq=tiles_per_seq, n_cast=len(casts)),
        grid=(steps,),
        in_specs=[
            pl.BlockSpec((tm, d_model), row),
            pl.BlockSpec((1, d_model), const),
            pl.BlockSpec(w_in.shape, const, pipeline_mode=pl.Buffered(1)),
            pl.BlockSpec((1, HEAD_DIM), const),
            pl.BlockSpec((1, HEAD_DIM), const),
            pl.BlockSpec((1, SG_WIDTH), const),
            pl.BlockSpec((tm, LANES), tab),
            pl.BlockSpec((tm, LANES), tab),
        ] + slabs,
        out_specs=out_specs,
        out_shape=out_shape,
        scratch_shapes=[pltpu.VMEM((INPROJ_SUB_BLOCKS, tm // INPROJ_SUB_BLOCKS, w_in.shape[1]), F32)],
        compiler_params=pltpu.CompilerParams(
            dimension_semantics=("arbitrary",), vmem_limit_bytes=VMEM_LIMIT_BYTES),
        name="inproj",
    )(x, attn_norm, w_in, q_norm, k_norm, sg_norm, cos_t, sin_t, *casts)


def _softmax_sink(s, mask, sink):
    s = jnp.where(mask, s, -jnp.inf)
    m = jnp.maximum(jnp.max(s, axis=-1, keepdims=True), sink)
    p = jnp.exp(s - m)
    denom = jnp.sum(p, axis=-1, keepdims=True) + jnp.exp(sink - m)
    return p, 1.0 / denom


def _spatial_gate(u, g, w_ref, b_ref, rows_per_seq):
    r = lax.broadcasted_iota(jnp.int32, (CHUNK, CHUNK), 0)
    c = lax.broadcasted_iota(jnp.int32, (CHUNK, CHUNK), 1)
    causal = (r // rows_per_seq == c // rows_per_seq) & (c % rows_per_seq <= r % rows_per_seq)
    gb = g.astype(BF16)
    b_t = b_ref[...].T[0:rows_per_seq, :]
    b_t = jnp.concatenate([b_t] * (CHUNK // rows_per_seq), axis=0)
    periodic = ((r < rows_per_seq) & (c % rows_per_seq == r)).astype(BF16)
    outs = []
    for h in range(N_SG_HEADS):
        if rows_per_seq == CHUNK:
            w = w_ref[h]
        else:
            w = jnp.dot(w_ref[h, 0:rows_per_seq, :].astype(BF16), periodic, preferred_element_type=F32)
            w = jnp.concatenate([w] * (CHUNK // rows_per_seq), axis=0)
        w = jnp.where(causal, w, 0.0).astype(BF16)
        cols = slice(SG_HEAD_DIM * h, SG_HEAD_DIM * (h + 1))
        bias = jnp.broadcast_to(b_t[:, h:h + 1], (CHUNK, SG_HEAD_DIM))
        parts = []
        for ch in range(u.shape[0] // CHUNK):
            rws = slice(CHUNK * ch, CHUNK * (ch + 1))
            mixed = jnp.dot(w, gb[rws, cols], preferred_element_type=F32) + bias
            parts.append(u[rws, cols].astype(F32) * mixed)
        outs.append(jnp.concatenate(parts, axis=0))
    return jnp.concatenate(outs, axis=1)


def _normed_mix(a_swapped, sgo, aon, son):
    return jnp.concatenate([_rms(a_swapped, _swap_heads(aon)), _rms(sgo, son)], axis=1).astype(BF16)


def _prompt_mix_body(sinks_ref, q_ref, k_ref, v_ref, u_ref, g_ref, sgw_ref, bias_ref,
                     aon_ref, son_ref, wd32, wo32_lo, wo32_hi, mix_ref, wd16, wo16, a_scr):
    wd16[...] = wd32[...].astype(BF16)
    wo16[0:HEAD_DIM, :] = wo32_lo[...].astype(BF16)
    wo16[HEAD_DIM:2 * HEAD_DIM, :] = wo32_hi[...].astype(BF16)

    j = pl.program_id(1)
    blocks = q_ref.shape[0] // WINDOW
    quarter = lax.broadcasted_iota(jnp.int32, (WINDOW, KV_WIDTH), 1) // HEAD_DIM
    qi = lax.broadcasted_iota(jnp.int32, (Q_PER_KV * WINDOW, WINDOW), 0) % WINDOW
    kj = lax.broadcasted_iota(jnp.int32, (Q_PER_KV * WINDOW, WINDOW), 1)
    from_prev = kj > qi
    row_g = lax.broadcasted_iota(jnp.int32, (Q_PER_KV * WINDOW, 1), 0) // WINDOW
    sink_cols = []
    for h in range(N_KV_HEADS):
        col = jnp.zeros((Q_PER_KV * WINDOW, 1), F32)
        for g in range(Q_PER_KV):
            col = jnp.where(row_g == g, sinks_ref[Q_PER_KV * h + g], col)
        sink_cols.append(col)

    def block(r, carry):
        n = j * blocks + r
        prev = pl.multiple_of(jnp.maximum(n - 1, 0) * WINDOW, WINDOW)
        cur = pl.multiple_of(n * WINDOW, WINDOW)
        no_prev = jnp.where(n > 0, 0.0, -jnp.inf)
        q = q_ref[pl.ds(pl.multiple_of(r * WINDOW, WINDOW), WINDOW), :]
        kb = jnp.concatenate([k_ref[pl.ds(prev, WINDOW), :], k_ref[pl.ds(cur, WINDOW), :]], axis=0)
        vb = jnp.concatenate([v_ref[pl.ds(prev, WINDOW), :], v_ref[pl.ds(cur, WINDOW), :]], axis=0)
        groups = None
        for h in range(N_KV_HEADS):
            sel = quarter == h
            lhs = jnp.concatenate(
                [jnp.where(sel, q[:, KV_WIDTH * g:KV_WIDTH * (g + 1)], 0) for g in range(Q_PER_KV)],
                axis=0)
            s = lax.dot_general(lhs, kb, (((1,), (1,)), ((), ())), preferred_element_type=F32)
            s = jnp.where(from_prev, s[:, :WINDOW] + no_prev, s[:, WINDOW:])
            sink = sink_cols[h]
            m = jnp.maximum(jnp.max(s, axis=-1, keepdims=True), sink)
            p = jnp.exp(s - m)
            inv = 1.0 / (jnp.sum(p, axis=-1, keepdims=True) + jnp.exp(sink - m))
            p_band = jnp.concatenate(
                [jnp.where(from_prev, p, 0.0), jnp.where(from_prev, 0.0, p)], axis=1).astype(BF16)
            o = jnp.dot(p_band, vb, preferred_element_type=F32) * inv
            parts = [o[WINDOW * g:WINDOW * (g + 1)] for g in range(Q_PER_KV)]
            groups = parts if groups is None else [
                jnp.where(sel, part, acc) for part, acc in zip(parts, groups)]
        a_scr[pl.ds(pl.multiple_of(r * WINDOW, WINDOW), WINDOW), :] = jnp.concatenate(groups, axis=1)
        return carry

    lax.fori_loop(0, blocks, block, 0, unroll=2)

    sgo = _spatial_gate(u_ref[...], g_ref[...], sgw_ref, bias_ref, CHUNK)
    mix_ref[...] = _normed_mix(a_scr[...], sgo, aon_ref[...], son_ref[...])


def _prompt_mix(sinks, q, k, v, u, g, sg_w, bias, aon, son, wd, wo, batch, seq):
    tm = ROW_TILE
    tiles = seq // tm
    steps = batch * tiles
    d_model = wo.shape[1]
    assert wo.shape[0] == 2 * HEAD_DIM * steps
    row = lambda b, j: (b * tiles + j, 0)
    per_seq = lambda b, j: (b, 0)
    const2 = lambda b, j: (0, 0)
    const3 = lambda b, j: (0, 0, 0)
    slab = lambda w: pl.BlockSpec((w.shape[0] // steps, w.shape[1]), row)
    bf16_like = lambda w: jax.ShapeDtypeStruct(w.shape, BF16)

    def wo_block(half):
        def index(b, j):
            c = 2 * (b * tiles + j) + half
            return jnp.where(c < N_Q_HEADS, _swapped_head(c), c), 0
        return pl.BlockSpec((HEAD_DIM, d_model), index)

    return pl.pallas_call(
        _prompt_mix_body,
        grid=(batch, tiles),
        in_specs=[
            pl.BlockSpec(memory_space=pltpu.SMEM),
            pl.BlockSpec((tm, ATTN_WIDTH), row),
            pl.BlockSpec((seq, KV_WIDTH), per_seq),
            pl.BlockSpec((seq, KV_WIDTH), per_seq),
            pl.BlockSpec((tm, SG_WIDTH), row),
            pl.BlockSpec((tm, SG_WIDTH), row),
            pl.BlockSpec(sg_w.shape, const3),
            pl.BlockSpec(bias.shape, const2),
            pl.BlockSpec((1, ATTN_WIDTH), const2),
            pl.BlockSpec((1, SG_WIDTH), const2),
            slab(wd), wo_block(0), wo_block(1),
        ],
        out_specs=[pl.BlockSpec((tm, d_model), row), slab(wd), slab(wo)],
        out_shape=[jax.ShapeDtypeStruct((batch * seq, d_model), BF16), bf16_like(wd), bf16_like(wo)],
        scratch_shapes=[pltpu.VMEM((tm, ATTN_WIDTH), F32)],
        compiler_params=pltpu.CompilerParams(
            dimension_semantics=("arbitrary", "arbitrary"), vmem_limit_bytes=VMEM_LIMIT_BYTES),
        name="prompt_mix",
    )(sinks, q, k, v, u, g, sg_w, bias, aon, son, wd, wo, wo)


def _sample_mix_body(sinks_ref, q_ref, k_ref, v_ref, ck_ref, cv_ref, u_ref, g_ref, sgw_ref,
                     bias_ref, aon_ref, son_ref, mix_ref, kw_ref, vw_ref, a_scr, *, dec_seq):
    rows = N_Q_HEADS * dec_seq
    quarter = lax.broadcasted_iota(jnp.int32, (dec_seq, KV_WIDTH), 1) // HEAD_DIM
    t = lax.broadcasted_iota(jnp.int32, (rows, 2 * WINDOW), 0) % dec_seq
    kj = lax.broadcasted_iota(jnp.int32, (rows, 2 * WINDOW), 1)
    pos_lane = lax.broadcasted_iota(jnp.int32, (KV_WIDTH, WINDOW), 1)
    row_head = lax.broadcasted_iota(jnp.int32, (rows, 1), 0) // dec_seq
    sink = jnp.zeros((rows, 1), F32)
    for a in range(N_Q_HEADS):
        sink = jnp.where(row_head == a, sinks_ref[a], sink)
    k_new_t = k_ref[...].T
    v_new_t = v_ref[...].T

    def one_seq(b, carry):
        r0 = pl.multiple_of(b * dec_seq, dec_seq)
        q = q_ref[pl.ds(r0, dec_seq), :]
        ck = ck_ref[b]
        cv = cv_ref[b]
        own = kj - WINDOW - b * dec_seq
        mask = ((kj < WINDOW) & (kj > t)) | ((own >= 0) & (own <= t))
        lhs = jnp.concatenate(
            [jnp.where(quarter == h, q[:, KV_WIDTH * g:KV_WIDTH * (g + 1)], 0.0)
             for h in range(N_KV_HEADS) for g in range(Q_PER_KV)], axis=0).astype(BF16)
        k_all = jnp.concatenate([ck, k_new_t], axis=1).astype(BF16)
        v_all = jnp.concatenate([cv, v_new_t], axis=1).astype(BF16)
        s = jnp.dot(lhs, k_all, preferred_element_type=F32)
        p, inv = _softmax_sink(s, mask, sink)
        o = lax.dot_general(p.astype(BF16), v_all, (((1,), (1,)), ((), ())),
                            preferred_element_type=F32) * inv
        groups = []
        for g in range(Q_PER_KV):
            piece = lambda h: o[(Q_PER_KV * h + g) * dec_seq:(Q_PER_KV * h + g + 1) * dec_seq]
            acc = piece(N_KV_HEADS - 1)
            for h in range(N_KV_HEADS - 2, -1, -1):
                acc = jnp.where(quarter == h, piece(h), acc)
            groups.append(acc)
        a_scr[pl.ds(r0, dec_seq), :] = jnp.concatenate(groups, axis=1)
        return carry

    lax.fori_loop(0, ck_ref.shape[0], one_seq, 0, unroll=4)

    keep = pos_lane < WINDOW - dec_seq
    for b in range(ck_ref.shape[0]):
        shift = WINDOW - dec_seq - b * dec_seq
        kw_ref[b] = jnp.where(keep, pltpu.roll(ck_ref[b], WINDOW - dec_seq, 1),
                              pltpu.roll(k_new_t, shift, 1))
        vw_ref[b] = jnp.where(keep, pltpu.roll(cv_ref[b], WINDOW - dec_seq, 1),
                              pltpu.roll(v_new_t, shift, 1))

    sgo = _spatial_gate(u_ref[...], g_ref[...], sgw_ref, bias_ref, dec_seq)
    mix_ref[...] = _normed_mix(a_scr[...], sgo, aon_ref[...], son_ref[...])


def _sample_mix(sinks, q, k, v, ck, cv, u, g, sg_w, bias, aon, son, dec_seq):
    nseq = ck.shape[0]
    sb = SEQS_PER_STEP
    tm = sb * dec_seq
    assert tm == WINDOW == CHUNK and dec_seq % 8 == 0
    row = lambda i: (i, 0)
    seq3 = lambda i: (i, 0, 0)
    const2 = lambda i: (0, 0)
    const3 = lambda i: (0, 0, 0)
    return pl.pallas_call(
        functools.partial(_sample_mix_body, dec_seq=dec_seq),
        grid=(nseq // sb,),
        in_specs=[
            pl.BlockSpec(memory_space=pltpu.SMEM),
            pl.BlockSpec((tm, ATTN_WIDTH), row),
            pl.BlockSpec((tm, KV_WIDTH), row),
            pl.BlockSpec((tm, KV_WIDTH), row),
            pl.BlockSpec((sb, KV_WIDTH, WINDOW), seq3),
            pl.BlockSpec((sb, KV_WIDTH, WINDOW), seq3),
            pl.BlockSpec((tm, SG_WIDTH), row),
            pl.BlockSpec((tm, SG_WIDTH), row),
            pl.BlockSpec(sg_w.shape, const3),
            pl.BlockSpec(bias.shape, const2),
            pl.BlockSpec((1, ATTN_WIDTH), const2),
            pl.BlockSpec((1, SG_WIDTH), const2),
        ],
        out_specs=[
            pl.BlockSpec((tm, ATTN_WIDTH + SG_WIDTH), row),
            pl.BlockSpec((sb, KV_WIDTH, WINDOW), seq3),
            pl.BlockSpec((sb, KV_WIDTH, WINDOW), seq3),
        ],
        out_shape=[
            jax.ShapeDtypeStruct((nseq * dec_seq, ATTN_WIDTH + SG_WIDTH), BF16),
            jax.ShapeDtypeStruct(ck.shape, F32),
            jax.ShapeDtypeStruct(cv.shape, F32),
        ],
        scratch_shapes=[pltpu.VMEM((tm, ATTN_WIDTH), F32)],
        compiler_params=pltpu.CompilerParams(
            dimension_semantics=("arbitrary",), vmem_limit_bytes=VMEM_LIMIT_BYTES),
        name="sample_mix",
    )(sinks, q, k, v, ck, cv, u, g, sg_w, bias, aon, son)


def _tail_body(x_ref, mix_ref, wo_ref, fn_ref, wg_hbm, wu_hbm, wd_hbm, y_ref,
               hn_scr, wg_buf, wu_buf, wd_buf, sems, *, n_tiles, n_chunks):
    n_steps = n_tiles * n_chunks
    tf = wg_buf.shape[2]
    depth = WEIGHT_SLOTS - 1
    step = pl.program_id(0) * n_chunks + pl.program_id(1)

    def chunk_copies(s):
        slot = s % WEIGHT_SLOTS
        cols = pl.ds(pl.multiple_of((s % n_chunks) * tf, tf), tf)
        return (
            pltpu.make_async_copy(wg_hbm.at[:, cols], wg_buf.at[slot], sems.at[0, slot]),
            pltpu.make_async_copy(wu_hbm.at[:, cols], wu_buf.at[slot], sems.at[1, slot]),
            pltpu.make_async_copy(wd_hbm.at[cols, :], wd_buf.at[slot], sems.at[2, slot]),
        )

    @pl.when(step == 0)
    def _():
        for s in range(depth):
            for copy in chunk_copies(s):
                copy.start(priority=RING_DMA_PRIORITY)

    @pl.when(step + depth < n_steps)
    def _():
        for copy in chunk_copies(step + depth):
            copy.start(priority=RING_DMA_PRIORITY)

    for copy in chunk_copies(step):
        copy.wait()
    slot = step % WEIGHT_SLOTS

    def ffn_chunk():
        hn = hn_scr[...]
        gate = jnp.dot(hn, wg_buf[slot], preferred_element_type=F32)
        up = jnp.dot(hn, wu_buf[slot], preferred_element_type=F32)
        act = (jax.nn.silu(gate) * up).astype(BF16)
        y_ref[...] += jnp.dot(act, wd_buf[slot], preferred_element_type=F32)

    @pl.when(pl.program_id(1) == 0)
    def _():
        half = x_ref.shape[0] // 2
        for rows in (slice(0, half), slice(half, 2 * half)):
            y_ref[rows, :] = x_ref[rows, :] + jnp.dot(
                mix_ref[rows, :], wo_ref[...], preferred_element_type=F32)
        for rows in (slice(0, half), slice(half, 2 * half)):
            hn_scr[rows, :] = _rms(y_ref[rows, :], fn_ref[...]).astype(BF16)
        ffn_chunk()

    @pl.when(pl.program_id(1) > 0)
    def _():
        ffn_chunk()


def _tail(x, mix, wo, ffn_norm, wg, wu, wd):
    rows, d_model = x.shape
    d_ff = wg.shape[1]
    tm = min(ROW_TILE, rows)
    tf = FF_TILE
    n_tiles, n_chunks = rows // tm, d_ff // tf
    assert n_tiles * n_chunks >= WEIGHT_SLOTS
    return pl.pallas_call(
        functools.partial(_tail_body, n_tiles=n_tiles, n_chunks=n_chunks),
        grid=(n_tiles, n_chunks),
        in_specs=[
            pl.BlockSpec((tm, d_model), lambda i, j: (i, 0)),
            pl.BlockSpec((tm, mix.shape[1]), lambda i, j: (i, 0)),
            pl.BlockSpec(wo.shape, lambda i, j: (0, 0), pipeline_mode=pl.Buffered(1)),
            pl.BlockSpec((1, d_model), lambda i, j: (0, 0)),
            pl.BlockSpec(memory_space=pltpu.HBM),
            pl.BlockSpec(memory_space=pltpu.HBM),
            pl.BlockSpec(memory_space=pltpu.HBM),
        ],
        out_specs=pl.BlockSpec((tm, d_model), lambda i, j: (i, 0)),
        out_shape=jax.ShapeDtypeStruct(x.shape, F32),
        scratch_shapes=[
            pltpu.VMEM((tm, d_model), BF16),
            pltpu.VMEM((WEIGHT_SLOTS, d_model, tf), BF16),
            pltpu.VMEM((WEIGHT_SLOTS, d_model, tf), BF16),
            pltpu.VMEM((WEIGHT_SLOTS, tf, d_model), BF16),
            pltpu.SemaphoreType.DMA((3, WEIGHT_SLOTS)),
        ],
        compiler_params=pltpu.CompilerParams(
            dimension_semantics=("arbitrary", "arbitrary"), vmem_limit_bytes=VMEM_LIMIT_BYTES),
        name="tail",
    )(x, mix, wo, ffn_norm, wg, wu, wd)


def _rope_tables(pos):
    half = HEAD_DIM // 2
    lane = jnp.arange(LANES)
    inv = ROPE_THETA ** (-(lane % half).astype(F32) / half)
    ang = pos.astype(F32)[:, None] * inv[None, :]
    sign = jnp.where(lane % HEAD_DIM < half, -1.0, 1.0).astype(F32)
    return jnp.cos(ang), jnp.sin(ang) * sign[None, :]


def _windows_to_native(w):
    return jnp.transpose(w, (0, 2, 3, 1)).reshape(w.shape[0], KV_WIDTH, WINDOW)


def _windows_from_native(w):
    return jnp.transpose(w.reshape(w.shape[0], N_KV_HEADS, HEAD_DIM, WINDOW), (0, 3, 1, 2))[None]


def kernel(x_prompt, x_sample, cache_k_win, cache_v_win, attn_norm, w_in, q_norm, k_norm, sinks,
           sg_norm, sg_w, sg_b, attn_out_norm, sg_out_norm, w_o, ffn_norm, w_gate, w_up, w_down):
    assert w_in.shape[0] == 1, "single-layer step only"
    batch, seq, d_model = x_prompt.shape
    dec_batch, dec_seq, _ = x_sample.shape

    w_in_b = w_in[0].astype(BF16)
    sink_vec = sinks[0]
    sgw = sg_w[0][:, :CHUNK, :CHUNK]
    sgb = sg_b[0][:, :CHUNK]

    cos_p, sin_p = _rope_tables(jnp.arange(seq, dtype=jnp.int32))
    tm_s = min(ROW_TILE, dec_batch * dec_seq)
    cos_s, sin_s = _rope_tables(PAST_LEN + jnp.arange(tm_s, dtype=jnp.int32) % dec_seq)

    xp = x_prompt.reshape(batch * seq, d_model)
    q, k, v, u, g, wg, wu, kwin, vwin = _inproj(xp, attn_norm, w_in_b, q_norm, k_norm, sg_norm,
                                                cos_p, sin_p, BF16, seq,
                                                casts=(w_gate[0], w_up[0]))
    mix_p, wd, wo = _prompt_mix(sink_vec, q, k, v, u, g, sgw, sgb, attn_out_norm, sg_out_norm,
                                w_down[0], w_o[0], batch, seq)
    y_prompt = _tail(xp, mix_p, wo, ffn_norm, wg, wu, wd).reshape(x_prompt.shape)
    k_win_prompt = _windows_from_native(kwin)
    v_win_prompt = _windows_from_native(vwin)

    xs = x_sample.reshape(dec_batch * dec_seq, d_model)
    q, k, v, u, g = _inproj(xs, attn_norm, w_in_b, q_norm, k_norm, sg_norm, cos_s, sin_s, F32, None)
    mix_s, kw, vw = _sample_mix(sink_vec, q, k, v, _windows_to_native(cache_k_win[0]),
                                _windows_to_native(cache_v_win[0]), u, g, sgw, sgb,
                                attn_out_norm, sg_out_norm, dec_seq)
    y_sample = _tail(xs, mix_s, wo, ffn_norm, wg, wu, wd).reshape(x_sample.shape)
    k_win_sample = _windows_from_native(kw)
    v_win_sample = _windows_from_native(vw)
    sg_v_sample = g.reshape(1, dec_batch, dec_seq, SG_WIDTH)

    return (y_prompt, y_sample, k_win_prompt, v_win_prompt, k_win_sample, v_win_sample, sg_v_sample)
```

```python
import functools

import jax
import jax.numpy as jnp
from jax import lax
from jax.experimental import pallas as pl
from jax.experimental.pallas import tpu as pltpu

HEAD_DIM = 64
N_KV_HEADS = 4
Q_PER_KV = 4
N_Q_HEADS = N_KV_HEADS * Q_PER_KV
KV_WIDTH = N_KV_HEADS * HEAD_DIM
ATTN_WIDTH = N_Q_HEADS * HEAD_DIM
WINDOW = 128
N_SG_HEADS = 8
SG_HEAD_DIM = 128
SG_WIDTH = N_SG_HEADS * SG_HEAD_DIM
CHUNK = 128
PAST_LEN = 16384
ROPE_THETA = 10000.0
EPS = 1e-6

LANES = 128
VMEM_LIMIT_BYTES = 56 * 1024 * 1024

ROW_TILE = 512
INPROJ_SUB_BLOCKS = 4
FF_TILE = 512
WEIGHT_SLOTS = 3
SEQS_PER_STEP = 16

F32 = jnp.float32
BF16 = jnp.bfloat16


def _rms(x, gain_row):
    r = lax.rsqrt(jnp.mean(x * x, axis=-1, keepdims=True) + EPS)
    return x * r * gain_row


def _swapped_head(c):
    return Q_PER_KV * (c % N_KV_HEADS) + c // N_KV_HEADS


def _swap_heads(t):
    lane = lax.broadcasted_iota(jnp.int32, (t.shape[0], LANES), 1)
    lo = lane < HEAD_DIM
    outs = []
    for j in range(ATTN_WIDTH // LANES):
        halves = []
        for half in range(2):
            a = _swapped_head(2 * j + half)
            src = t[:, LANES * (a // 2):LANES * (a // 2 + 1)]
            halves.append(src if a % 2 == half else pltpu.roll(src, HEAD_DIM, 1))
        outs.append(jnp.where(lo, halves[0], halves[1]))
    return jnp.concatenate(outs, axis=1)


def _head_norm_rope(t, gain_row, cos, sin_signed):
    lane = lax.broadcasted_iota(jnp.int32, (t.shape[0], LANES), 1)
    lo_head = lane < HEAD_DIM
    first_half = (lane % HEAD_DIM) < (HEAD_DIM // 2)
    outs = []
    for j in range(t.shape[1] // LANES):
        blk = t[:, LANES * j:LANES * (j + 1)]
        sq = blk * blk
        s_all = jnp.sum(sq, axis=-1, keepdims=True)
        s_lo = jnp.sum(jnp.where(lo_head, sq, 0.0), axis=-1, keepdims=True)
        s_hi = s_all - s_lo
        r = jnp.where(lo_head,
                      lax.rsqrt(s_lo * (1.0 / HEAD_DIM) + EPS),
                      lax.rsqrt(s_hi * (1.0 / HEAD_DIM) + EPS))
        y = blk * r * gain_row
        rot = jnp.where(first_half,
                        pltpu.roll(y, LANES - HEAD_DIM // 2, 1),
                        pltpu.roll(y, HEAD_DIM // 2, 1))
        outs.append(y * cos + rot * sin_signed)
    return jnp.concatenate(outs, axis=1)


def _inproj_body(x_ref, an_ref, w_ref, qn_ref, kn_ref, sgn_ref, cos_ref, sin_ref, *rest,
                 tiles_per_seq, n_cast):
    casts_in, rest = rest[:n_cast], rest[n_cast:]
    q_ref, k_ref, v_ref, u_ref, g_ref = rest[:5]
    casts_out, rest = rest[5:5 + n_cast], rest[5 + n_cast:]
    proj_scr = rest[-1]
    n_sub, sub = proj_scr.shape[0], proj_scr.shape[1]
    q_gain = jnp.concatenate([qn_ref[...]] * (LANES // HEAD_DIM), axis=1)
    k_gain = jnp.concatenate([kn_ref[...]] * (LANES // HEAD_DIM), axis=1)
    c0, c1, c2, c3 = ATTN_WIDTH, ATTN_WIDTH + KV_WIDTH, ATTN_WIDTH + 2 * KV_WIDTH, \
        ATTN_WIDTH + 2 * KV_WIDTH + SG_WIDTH

    def project(r):
        xn = _rms(x_ref[sub * r:sub * (r + 1), :], an_ref[...]).astype(BF16)
        proj_scr[r] = jnp.dot(xn, w_ref[...], preferred_element_type=F32)

    def finish(r):
        rows = slice(sub * r, sub * (r + 1))
        cos = cos_ref[rows, :]
        sin = sin_ref[rows, :]
        q = _head_norm_rope(proj_scr[r, :, 0:c0], q_gain, cos, sin) * (HEAD_DIM ** -0.5)
        q_ref[rows, :] = _swap_heads(q).astype(q_ref.dtype)
        k = _head_norm_rope(proj_scr[r, :, c0:c1], k_gain, cos, sin)
        k_ref[rows, :] = k.astype(k_ref.dtype)
        v = proj_scr[r, :, c1:c2]
        v_ref[rows, :] = v.astype(v_ref.dtype)
        u_ref[rows, :] = jax.nn.gelu(proj_scr[r, :, c2:c3]).astype(u_ref.dtype)
        g = jax.nn.gelu(proj_scr[r, :, c3:])
        g_ref[rows, :] = _rms(g, sgn_ref[...]).astype(g_ref.dtype)
        return k, v

    project(0)
    for src, dst in zip(casts_in, casts_out):
        dst[...] = src[...].astype(BF16)
    for r in range(1, n_sub):
        project(r)
        finish(r - 1)
    k, v = finish(n_sub - 1)

    if len(rest) > 1:
        kwin_ref, vwin_ref = rest[0], rest[1]

        @pl.when(pl.program_id(0) % tiles_per_seq == tiles_per_seq - 1)
        def _():
            kwin_ref[0] = k[sub - WINDOW:].T
            vwin_ref[0] = v[sub - WINDOW:].T


def _inproj(x, attn_norm, w_in, q_norm, k_norm, sg_norm, cos_t, sin_t, act_dtype, seq, casts=()):
    rows, d_model = x.shape
    tm = min(ROW_TILE, rows)
    steps = rows // tm
    n_tab = cos_t.shape[0] // tm
    const = lambda i: (0, 0)
    row = lambda i: (i, 0)
    tab = lambda i: (i % n_tab, 0)
    slabs = [pl.BlockSpec((w.shape[0] // steps, w.shape[1]), row) for w in casts]
    out_specs = [
        pl.BlockSpec((tm, ATTN_WIDTH), row),
        pl.BlockSpec((tm, KV_WIDTH), row),
        pl.BlockSpec((tm, KV_WIDTH), row),
        pl.BlockSpec((tm, SG_WIDTH), row),
        pl.BlockSpec((tm, SG_WIDTH), row),
    ]
    out_shape = [
        jax.ShapeDtypeStruct((rows, ATTN_WIDTH), act_dtype),
        jax.ShapeDtypeStruct((rows, KV_WIDTH), act_dtype),
        jax.ShapeDtypeStruct((rows, KV_WIDTH), act_dtype),
        jax.ShapeDtypeStruct((rows, SG_WIDTH), BF16),
        jax.ShapeDtypeStruct((rows, SG_WIDTH), act_dtype),
    ]
    out_specs += slabs
    out_shape += [jax.ShapeDtypeStruct(w.shape, BF16) for w in casts]
    tiles_per_seq = 1
    if seq is not None:
        tiles_per_seq = seq // tm
        win = lambda i: (i // tiles_per_seq, 0, 0)
        out_specs += [pl.BlockSpec((1, KV_WIDTH, WINDOW), win)] * 2
        out_shape += [jax.ShapeDtypeStruct((rows // seq, KV_WIDTH, WINDOW), F32)] * 2
    return pl.pallas_call(
        functools.partial(_inproj_body, tiles_per_seq=tiles_per_seq, n_cast=len(casts)),
        grid=(steps,),
        in_specs=[
            pl.BlockSpec((tm, d_model), row),
            pl.BlockSpec((1, d_model), const),
            pl.BlockSpec(w_in.shape, const, pipeline_mode=pl.Buffered(1)),
            pl.BlockSpec((1, HEAD_DIM), const),
            pl.BlockSpec((1, HEAD_DIM), const),
            pl.BlockSpec((1, SG_WIDTH), const),
            pl.BlockSpec((tm, LANES), tab),
            pl.BlockSpec((tm, LANES), tab),
        ] + slabs,
        out_specs=out_specs,
        out_shape=out_shape,
        scratch_shapes=[pltpu.VMEM((INPROJ_SUB_BLOCKS, tm // INPROJ_SUB_BLOCKS, w_in.shape[1]), F32)],
        compiler_params=pltpu.CompilerParams(
            dimension_semantics=("arbitrary",), vmem_limit_bytes=VMEM_LIMIT_BYTES),
        name="inproj",
    )(x, attn_norm, w_in, q_norm, k_norm, sg_norm, cos_t, sin_t, *casts)


def _softmax_sink(s, mask, sink):
    s = jnp.where(mask, s, -jnp.inf)
    m = jnp.maximum(jnp.max(s, axis=-1, keepdims=True), sink)
    p = jnp.exp(s - m)
    denom = jnp.sum(p, axis=-1, keepdims=True) + jnp.exp(sink - m)
    return p, 1.0 / denom


def _spatial_gate(u, g, w_ref, b_ref, rows_per_seq):
    r = lax.broadcasted_iota(jnp.int32, (CHUNK, CHUNK), 0)
    c = lax.broadcasted_iota(jnp.int32, (CHUNK, CHUNK), 1)
    causal = (r // rows_per_seq == c // rows_per_seq) & (c % rows_per_seq <= r % rows_per_seq)
    gb = g.astype(BF16)
    b_t = b_ref[...].T[0:rows_per_seq, :]
    b_t = jnp.concatenate([b_t] * (CHUNK // rows_per_seq), axis=0)
    periodic = ((r < rows_per_seq) & (c % rows_per_seq == r)).astype(BF16)
    outs = []
    for h in range(N_SG_HEADS):
        if rows_per_seq == CHUNK:
            w = w_ref[h]
        else:
            w = jnp.dot(w_ref[h, 0:rows_per_seq, :].astype(BF16), periodic, preferred_element_type=F32)
            w = jnp.concatenate([w] * (CHUNK // rows_per_seq), axis=0)
        w = jnp.where(causal, w, 0.0).astype(BF16)
        cols = slice(SG_HEAD_DIM * h, SG_HEAD_DIM * (h + 1))
        bias = jnp.broadcast_to(b_t[:, h:h + 1], (CHUNK, SG_HEAD_DIM))
        parts = []
        for ch in range(u.shape[0] // CHUNK):
            rws = slice(CHUNK * ch, CHUNK * (ch + 1))
            mixed = jnp.dot(w, gb[rws, cols], preferred_element_type=F32) + bias
            parts.append(u[rws, cols].astype(F32) * mixed)
        outs.append(jnp.concatenate(parts, axis=0))
    return jnp.concatenate(outs, axis=1)


def _normed_mix(a_swapped, sgo, aon, son):
    return jnp.concatenate([_rms(a_swapped, _swap_heads(aon)), _rms(sgo, son)], axis=1).astype(BF16)


def _prompt_mix_body(sinks_ref, q_ref, k_ref, v_ref, u_ref, g_ref, sgw_ref, bias_ref,
                     aon_ref, son_ref, wd32, wo32_lo, wo32_hi, mix_ref, wd16, wo16, a_scr):
    wd16[...] = wd32[...].astype(BF16)
    wo16[0:HEAD_DIM, :] = wo32_lo[...].astype(BF16)
    wo16[HEAD_DIM:2 * HEAD_DIM, :] = wo32_hi[...].astype(BF16)

    j = pl.program_id(1)
    blocks = q_ref.shape[0] // WINDOW
    quarter = lax.broadcasted_iota(jnp.int32, (WINDOW, KV_WIDTH), 1) // HEAD_DIM
    qi = lax.broadcasted_iota(jnp.int32, (Q_PER_KV * WINDOW, WINDOW), 0) % WINDOW
    kj = lax.broadcasted_iota(jnp.int32, (Q_PER_KV * WINDOW, WINDOW), 1)
    from_prev = kj > qi
    row_g = lax.broadcasted_iota(jnp.int32, (Q_PER_KV * WINDOW, 1), 0) // WINDOW
    sink_cols = []
    for h in range(N_KV_HEADS):
        col = jnp.zeros((Q_PER_KV * WINDOW, 1), F32)
        for g in range(Q_PER_KV):
            col = jnp.where(row_g == g, sinks_ref[Q_PER_KV * h + g], col)
        sink_cols.append(col)

    def block(r, carry):
        n = j * blocks + r
        prev = pl.multiple_of(jnp.maximum(n - 1, 0) * WINDOW, WINDOW)
        cur = pl.multiple_of(n * WINDOW, WINDOW)
        no_prev = jnp.where(n > 0, 0.0, -jnp.inf)
        q = q_ref[pl.ds(pl.multiple_of(r * WINDOW, WINDOW), WINDOW), :]
        kb = jnp.concatenate([k_ref[pl.ds(prev, WINDOW), :], k_ref[pl.ds(cur, WINDOW), :]], axis=0)
        vb = jnp.concatenate([v_ref[pl.ds(prev, WINDOW), :], v_ref[pl.ds(cur, WINDOW), :]], axis=0)
        groups = None
        for h in range(N_KV_HEADS):
            sel = quarter == h
            lhs = jnp.concatenate(
                [jnp.where(sel, q[:, KV_WIDTH * g:KV_WIDTH * (g + 1)], 0) for g in range(Q_PER_KV)],
                axis=0)
            s = lax.dot_general(lhs, kb, (((1,), (1,)), ((), ())), preferred_element_type=F32)
            s = jnp.where(from_prev, s[:, :WINDOW] + no_prev, s[:, WINDOW:])
            sink = sink_cols[h]
            m = jnp.maximum(jnp.max(s, axis=-1, keepdims=True), sink)
            p = jnp.exp(s - m)
            inv = 1.0 / (jnp.sum(p, axis=-1, keepdims=True) + jnp.exp(sink - m))
            p_band = jnp.concatenate(
                [jnp.where(from_prev, p, 0.0), jnp.where(from_prev, 0.0, p)], axis=1).astype(BF16)
            o = jnp.dot(p_band, vb, preferred_element_type=F32) * inv
            parts = [o[WINDOW * g:WINDOW * (g + 1)] for g in range(Q_PER_KV)]
            groups = parts if groups is None else [
                jnp.where(sel, part, acc) for part, acc in zip(parts, groups)]
        a_scr[pl.ds(pl.multiple_of(r * WINDOW, WINDOW), WINDOW), :] = jnp.concatenate(groups, axis=1)
        return carry

    lax.fori_loop(0, blocks, block, 0, unroll=2)

    sgo = _spatial_gate(u_ref[...], g_ref[...], sgw_ref, bias_ref, CHUNK)
    mix_ref[...] = _normed_mix(a_scr[...], sgo, aon_ref[...], son_ref[...])


def _prompt_mix(sinks, q, k, v, u, g, sg_w, bias, aon, son, wd, wo, batch, seq):
    tm = ROW_TILE
    tiles = seq // tm
    steps = batch * tiles
    d_model = wo.shape[1]
    assert wo.shape[0] == 2 * HEAD_DIM * steps
    row = lambda b, j: (b * tiles + j, 0)
    per_seq = lambda b, j: (b, 0)
    const2 = lambda b, j: (0, 0)
    const3 = lambda b, j: (0, 0, 0)
    slab = lambda w: pl.BlockSpec((w.shape[0] // steps, w.shape[1]), row)
    bf16_like = lambda w: jax.ShapeDtypeStruct(w.shape, BF16)

    def wo_block(half):
        def index(b, j):
            c = 2 * (b * tiles + j) + half
            return jnp.where(c < N_Q_HEADS, _swapped_head(c), c), 0
        return pl.BlockSpec((HEAD_DIM, d_model), index)

    return pl.pallas_call(
        _prompt_mix_body,
        grid=(batch, tiles),
        in_specs=[
            pl.BlockSpec(memory_space=pltpu.SMEM),
            pl.BlockSpec((tm, ATTN_WIDTH), row),
            pl.BlockSpec((seq, KV_WIDTH), per_seq),
            pl.BlockSpec((seq, KV_WIDTH), per_seq),
            pl.BlockSpec((tm, SG_WIDTH), row),
            pl.BlockSpec((tm, SG_WIDTH), row),
            pl.BlockSpec(sg_w.shape, const3),
            pl.BlockSpec(bias.shape, const2),
            pl.BlockSpec((1, ATTN_WIDTH), const2),
            pl.BlockSpec((1, SG_WIDTH), const2),
            slab(wd), wo_block(0), wo_block(1),
        ],
        out_specs=[pl.BlockSpec((tm, d_model), row), slab(wd), slab(wo)],
        out_shape=[jax.ShapeDtypeStruct((batch * seq, d_model), BF16), bf16_like(wd), bf16_like(wo)],
        scratch_shapes=[pltpu.VMEM((tm, ATTN_WIDTH), F32)],
        compiler_params=pltpu.CompilerParams(
            dimension_semantics=("arbitrary", "arbitrary"), vmem_limit_bytes=VMEM_LIMIT_BYTES),
        name="prompt_mix",
    )(sinks, q, k, v, u, g, sg_w, bias, aon, son, wd, wo, wo)


def _sample_mix_body(sinks_ref, q_ref, k_ref, v_ref, ck_ref, cv_ref, u_ref, g_ref, sgw_ref,
                     bias_ref, aon_ref, son_ref, mix_ref, kw_ref, vw_ref, a_scr, *, dec_seq):
    rows = N_Q_HEADS * dec_seq
    quarter = lax.broadcasted_iota(jnp.int32, (dec_seq, KV_WIDTH), 1) // HEAD_DIM
    t = lax.broadcasted_iota(jnp.int32, (rows, 2 * WINDOW), 0) % dec_seq
    kj = lax.broadcasted_iota(jnp.int32, (rows, 2 * WINDOW), 1)
    pos_lane = lax.broadcasted_iota(jnp.int32, (KV_WIDTH, WINDOW), 1)
    row_head = lax.broadcasted_iota(jnp.int32, (rows, 1), 0) // dec_seq
    sink = jnp.zeros((rows, 1), F32)
    for a in range(N_Q_HEADS):
        sink = jnp.where(row_head == a, sinks_ref[a], sink)
    k_new_t = k_ref[...].T
    v_new_t = v_ref[...].T

    def one_seq(b, carry):
        r0 = pl.multiple_of(b * dec_seq, dec_seq)
        q = q_ref[pl.ds(r0, dec_seq), :]
        ck = ck_ref[b]
        cv = cv_ref[b]
        own = kj - WINDOW - b * dec_seq
        mask = ((kj < WINDOW) & (kj > t)) | ((own >= 0) & (own <= t))
        lhs = jnp.concatenate(
            [jnp.where(quarter == h, q[:, KV_WIDTH * g:KV_WIDTH * (g + 1)], 0.0)
             for h in range(N_KV_HEADS) for g in range(Q_PER_KV)], axis=0).astype(BF16)
        k_all = jnp.concatenate([ck, k_new_t], axis=1).astype(BF16)
        v_all = jnp.concatenate([cv, v_new_t], axis=1).astype(BF16)
        s = jnp.dot(lhs, k_all, preferred_element_type=F32)
        p, inv = _softmax_sink(s, mask, sink)
        o = lax.dot_general(p.astype(BF16), v_all, (((1,), (1,)), ((), ())),
                            preferred_element_type=F32) * inv
        groups = []
        for g in range(Q_PER_KV):
            piece = lambda h: o[(Q_PER_KV * h + g) * dec_seq:(Q_PER_KV * h + g + 1) * dec_seq]
            acc = piece(N_KV_HEADS - 1)
            for h in range(N_KV_HEADS - 2, -1, -1):
                acc = jnp.where(quarter == h, piece(h), acc)
            groups.append(acc)
        a_scr[pl.ds(r0, dec_seq), :] = jnp.concatenate(groups, axis=1)
        return carry

    lax.fori_loop(0, ck_ref.shape[0], one_seq, 0, unroll=4)

    keep = pos_lane < WINDOW - dec_seq
    for b in range(ck_ref.shape[0]):
        shift = WINDOW - dec_seq - b * dec_seq
        kw_ref[b] = jnp.where(keep, pltpu.roll(ck_ref[b], WINDOW - dec_seq, 1),
                              pltpu.roll(k_new_t, shift, 1))
        vw_ref[b] = jnp.where(keep, pltpu.roll(cv_ref[b], WINDOW - dec_seq, 1),
                              pltpu.roll(v_new_t, shift, 1))

    sgo = _spatial_gate(u_ref[...], g_ref[...], sgw_ref, bias_ref, dec_seq)
    mix_ref[...] = _normed_mix(a_scr[...], sgo, aon_ref[...], son_ref[...])


def _sample_mix(sinks, q, k, v, ck, cv, u, g, sg_w, bias, aon, son, dec_seq):
    nseq = ck.shape[0]
    sb = SEQS_PER_STEP
    tm = sb * dec_seq
    assert tm == WINDOW == CHUNK and dec_seq % 8 == 0
    row = lambda i: (i, 0)
    seq3 = lambda i: (i, 0, 0)
    const2 = lambda i: (0, 0)
    const3 = lambda i: (0, 0, 0)
    return pl.pallas_call(
        functools.partial(_sample_mix_body, dec_seq=dec_seq),
        grid=(nseq // sb,),
        in_specs=[
            pl.BlockSpec(memory_space=pltpu.SMEM),
            pl.BlockSpec((tm, ATTN_WIDTH), row),
            pl.BlockSpec((tm, KV_WIDTH), row),
            pl.BlockSpec((tm, KV_WIDTH), row),
            pl.BlockSpec((sb, KV_WIDTH, WINDOW), seq3),
            pl.BlockSpec((sb, KV_WIDTH, WINDOW), seq3),
            pl.BlockSpec((tm, SG_WIDTH), row),
            pl.BlockSpec((tm, SG_WIDTH), row),
            pl.BlockSpec(sg_w.shape, const3),
            pl.BlockSpec(bias.shape, const2),
            pl.BlockSpec((1, ATTN_WIDTH), const2),
            pl.BlockSpec((1, SG_WIDTH), const2),
        ],
        out_specs=[
            pl.BlockSpec((tm, ATTN_WIDTH + SG_WIDTH), row),
            pl.BlockSpec((sb, KV_WIDTH, WINDOW), seq3),
            pl.BlockSpec((sb, KV_WIDTH, WINDOW), seq3),
        ],
        out_shape=[
            jax.ShapeDtypeStruct((nseq * dec_seq, ATTN_WIDTH + SG_WIDTH), BF16),
            jax.ShapeDtypeStruct(ck.shape, F32),
            jax.ShapeDtypeStruct(cv.shape, F32),
        ],
        scratch_shapes=[pltpu.VMEM((tm, ATTN_WIDTH), F32)],
        compiler_params=pltpu.CompilerParams(
            dimension_semantics=("arbitrary",), vmem_limit_bytes=VMEM_LIMIT_BYTES),
        name="sample_mix",
    )(sinks, q, k, v, ck, cv, u, g, sg_w, bias, aon, son)


def _tail_body(x_ref, mix_ref, wo_ref, fn_ref, wg_hbm, wu_hbm, wd_hbm, y_ref,
               hn_scr, wg_buf, wu_buf, wd_buf, sems, *, n_tiles, n_chunks):
    n_steps = n_tiles * n_chunks
    tf = wg_buf.shape[2]
    depth = WEIGHT_SLOTS - 1
    step = pl.program_id(0) * n_chunks + pl.program_id(1)

    def chunk_copies(s):
        slot = s % WEIGHT_SLOTS
        cols = pl.ds(pl.multiple_of((s % n_chunks) * tf, tf), tf)
        return (
            pltpu.make_async_copy(wg_hbm.at[:, cols], wg_buf.at[slot], sems.at[0, slot]),
            pltpu.make_async_copy(wu_hbm.at[:, cols], wu_buf.at[slot], sems.at[1, slot]),
            pltpu.make_async_copy(wd_hbm.at[cols, :], wd_buf.at[slot], sems.at[2, slot]),
        )

    @pl.when(step == 0)
    def _():
        for copy in chunk_copies(0):
            copy.start()

    for copy in chunk_copies(step):
        copy.wait()
    slot = step % WEIGHT_SLOTS

    @pl.when(step == 0)
    def _():
        for s in range(1, depth):
            for copy in chunk_copies(s):
                copy.start()

    @pl.when(step + depth < n_steps)
    def _():
        for copy in chunk_copies(step + depth):
            copy.start()

    def ffn_chunk():
        hn = hn_scr[...]
        gate = jnp.dot(hn, wg_buf[slot], preferred_element_type=F32)
        up = jnp.dot(hn, wu_buf[slot], preferred_element_type=F32)
        act = (jax.nn.silu(gate) * up).astype(BF16)
        y_ref[...] += jnp.dot(act, wd_buf[slot], preferred_element_type=F32)

    @pl.when(pl.program_id(1) == 0)
    def _():
        half = x_ref.shape[0] // 2
        for rows in (slice(0, half), slice(half, 2 * half)):
            y_ref[rows, :] = x_ref[rows, :] + jnp.dot(
                mix_ref[rows, :], wo_ref[...], preferred_element_type=F32)
        for rows in (slice(0, half), slice(half, 2 * half)):
            hn_scr[rows, :] = _rms(y_ref[rows, :], fn_ref[...]).astype(BF16)
        ffn_chunk()

    @pl.when(pl.program_id(1) > 0)
    def _():
        ffn_chunk()


def _tail(x, mix, wo, ffn_norm, wg, wu, wd):
    rows, d_model = x.shape
    d_ff = wg.shape[1]
    tm = min(ROW_TILE, rows)
    tf = FF_TILE
    n_tiles, n_chunks = rows // tm, d_ff // tf
    assert n_tiles * n_chunks >= WEIGHT_SLOTS
    return pl.pallas_call(
        functools.partial(_tail_body, n_tiles=n_tiles, n_chunks=n_chunks),
        grid=(n_tiles, n_chunks),
        in_specs=[
            pl.BlockSpec((tm, d_model), lambda i, j: (i, 0)),
            pl.BlockSpec((tm, mix.shape[1]), lambda i, j: (i, 0)),
            pl.BlockSpec(wo.shape, lambda i, j: (0, 0), pipeline_mode=pl.Buffered(1)),
            pl.BlockSpec((1, d_model), lambda i, j: (0, 0)),
            pl.BlockSpec(memory_space=pltpu.HBM),
            pl.BlockSpec(memory_space=pltpu.HBM),
            pl.BlockSpec(memory_space=pltpu.HBM),
        ],
        out_specs=pl.BlockSpec((tm, d_model), lambda i, j: (i, 0)),
        out_shape=jax.ShapeDtypeStruct(x.shape, F32),
        scratch_shapes=[
            pltpu.VMEM((tm, d_model), BF16),
            pltpu.VMEM((WEIGHT_SLOTS, d_model, tf), BF16),
            pltpu.VMEM((WEIGHT_SLOTS, d_model, tf), BF16),
            pltpu.VMEM((WEIGHT_SLOTS, tf, d_model), BF16),
            pltpu.SemaphoreType.DMA((3, WEIGHT_SLOTS)),
        ],
        compiler_params=pltpu.CompilerParams(
            dimension_semantics=("arbitrary", "arbitrary"), vmem_limit_bytes=VMEM_LIMIT_BYTES),
        name="tail",
    )(x, mix, wo, ffn_norm, wg, wu, wd)


def _rope_tables(pos):
    half = HEAD_DIM // 2
    lane = jnp.arange(LANES)
    inv = ROPE_THETA ** (-(lane % half).astype(F32) / half)
    ang = pos.astype(F32)[:, None] * inv[None, :]
    sign = jnp.where(lane % HEAD_DIM < half, -1.0, 1.0).astype(F32)
    return jnp.cos(ang), jnp.sin(ang) * sign[None, :]


def _windows_to_native(w):
    return jnp.transpose(w, (0, 2, 3, 1)).reshape(w.shape[0], KV_WIDTH, WINDOW)


def _windows_from_native(w):
    return jnp.transpose(w.reshape(w.shape[0], N_KV_HEADS, HEAD_DIM, WINDOW), (0, 3, 1, 2))[None]


def kernel(x_prompt, x_sample, cache_k_win, cache_v_win, attn_norm, w_in, q_norm, k_norm, sinks,
           sg_norm, sg_w, sg_b, attn_out_norm, sg_out_norm, w_o, ffn_norm, w_gate, w_up, w_down):
    assert w_in.shape[0] == 1, "single-layer step only"
    batch, seq, d_model = x_prompt.shape
    dec_batch, dec_seq, _ = x_sample.shape

    w_in_b = w_in[0].astype(BF16)
    sink_vec = sinks[0]
    sgw = sg_w[0][:, :CHUNK, :CHUNK]
    sgb = sg_b[0][:, :CHUNK]

    cos_p, sin_p = _rope_tables(jnp.arange(seq, dtype=jnp.int32))
    tm_s = min(ROW_TILE, dec_batch * dec_seq)
    cos_s, sin_s = _rope_tables(PAST_LEN + jnp.arange(tm_s, dtype=jnp.int32) % dec_seq)

    xp = x_prompt.reshape(batch * seq, d_model)
    q, k, v, u, g, wg, wu, kwin, vwin = _inproj(xp, attn_norm, w_in_b, q_norm, k_norm, sg_norm,
                                                cos_p, sin_p, BF16, seq,
                                                casts=(w_gate[0], w_up[0]))
    mix_p, wd, wo = _prompt_mix(sink_vec, q, k, v, u, g, sgw, sgb, attn_out_norm, sg_out_norm,
                                w_down[0], w_o[0], batch, seq)
    y_prompt = _tail(xp, mix_p, wo, ffn_norm, wg, wu, wd).reshape(x_prompt.shape)
    k_win_prompt = _windows_from_native(kwin)
    v_win_prompt = _windows_from_native(vwin)

    xs = x_sample.reshape(dec_batch * dec_seq, d_model)
    q, k, v, u, g = _inproj(xs, attn_norm, w_in_b, q_norm, k_norm, sg_norm, cos_s, sin_s, F32, None)
    mix_s, kw, vw = _sample_mix(sink_vec, q, k, v, _windows_to_native(cache_k_win[0]),
                                _windows_to_native(cache_v_win[0]), u, g, sgw, sgb,
                                attn_out_norm, sg_out_norm, dec_seq)
    y_sample = _tail(xs, mix_s, wo, ffn_norm, wg, wu, wd).reshape(x_sample.shape)
    k_win_sample = _windows_from_native(kw)
    v_win_sample = _windows_from_native(vw)
    sg_v_sample = g.reshape(1, dec_batch, dec_seq, SG_WIDTH)

    return (y_prompt, y_sample, k_win_prompt, v_win_prompt, k_win_sample, v_win_sample, sg_v_sample)
```
